```python
import math
import jax
import jax.numpy as jnp
from jax import lax
import numpy as np

D_MODEL = 1024
BATCH = 16
SEQ = 4096
DEPTH = 2

N_HEADS = 4
HEAD_DIM = 64
MIX_W = N_HEADS * HEAD_DIM
N_BRANCH = 4
Q_BLOCK = 128
GDN_CHUNK = 64
CONV_K = 4
RET_CHUNK = 128
ROPE_BASE = 10000.0
NSA_CMP_LEN = 32
NSA_CMP_STRIDE = 16
NSA_CMP_HIDDEN = 256
NSA_SEL_LEN = 64
NSA_TOP_N = 16
NSA_WINDOW = 512
NSA_Q_BLOCK = 64
NSA_FORCE_SCORE = 1.0e4
N_EXPERTS = 16
N_GROUPS = 4
EXPERTS_PER_GROUP = N_EXPERTS // N_GROUPS
TOP_K = 2
D_EXPERT = 512
DN_ALPHA = (2.0 * DEPTH) ** 0.25
DN_BETA = (8.0 * DEPTH) ** -0.25
LN_EPS = 1e-5

IN_WIDTHS = (
    MIX_W, MIX_W, MIX_W, N_HEADS, N_HEADS, MIX_W,
    MIX_W, MIX_W, MIX_W, MIX_W,
    MIX_W, HEAD_DIM, HEAD_DIM, HEAD_DIM, HEAD_DIM, HEAD_DIM, HEAD_DIM, 3 * N_HEADS,
    MIX_W, MIX_W, MIX_W, N_HEADS,
)
IN_WIDTH = int(sum(IN_WIDTHS))
IN_SPLITS = tuple(int(v) for v in np.cumsum(IN_WIDTHS)[:-1])

kernel_name = 'hybrid_gdn_ret_nsa_fox_moe_block'


def layer_norm(x, g, b):
    xf = x.astype(jnp.float32)
    mu = xf.mean(-1, keepdims=True)
    var = jnp.square(xf - mu).mean(-1, keepdims=True)
    return ((xf - mu) * lax.rsqrt(var + LN_EPS)).astype(x.dtype) * g + b


def rms_normalize(x):
    xf = x.astype(jnp.float32)
    return xf * lax.rsqrt(jnp.mean(jnp.square(xf), -1, keepdims=True) + 1e-6)


def l2_normalize(x):
    xf = x.astype(jnp.float32)
    return (xf * lax.rsqrt(jnp.sum(jnp.square(xf), -1, keepdims=True) + 1e-6)).astype(x.dtype)


def split_heads(x):
    b, s, _ = x.shape
    return x.reshape(b, s, N_HEADS, HEAD_DIM).transpose(0, 2, 1, 3)


def merge_heads(x):
    b, h, s, d = x.shape
    return x.transpose(0, 2, 1, 3).reshape(b, s, h * d)


def to_chunks(t, size):
    return t.reshape(t.shape[:2] + (t.shape[2] // size, size) + t.shape[3:])


def masked_softmax(s, mask):
    s = jnp.where(mask, s.astype(jnp.float32), -jnp.inf)
    m = jnp.max(s, -1, keepdims=True)
    m = jnp.where(jnp.isfinite(m), m, 0.0)
    e = jnp.exp(s - m)
    den = e.sum(-1, keepdims=True)
    return e / jnp.where(den > 0, den, 1.0)


def causal_depthwise_conv(x, w):
    k, ch = w.shape
    return lax.conv_general_dilated(x, w[:, None, :].astype(x.dtype), window_strides=(1,), padding=[(k - 1, 0)],
                                    dimension_numbers=('NWC', 'WIO', 'NWC'), feature_group_count=ch)


def rope(x, positions):
    half = HEAD_DIM // 2
    inv_freq = ROPE_BASE ** (-jnp.arange(half, dtype=jnp.float32) / half)
    ang = positions.astype(jnp.float32)[:, None, :, None] * inv_freq
    cos, sin = jnp.cos(ang), jnp.sin(ang)
    x1, x2 = x[..., :half], x[..., half:]
    return jnp.concatenate([x1 * cos - x2 * sin, x1 * sin + x2 * cos], -1).astype(x.dtype)


def ada_modulation(c, w, b):
    mod = jax.nn.silu(c) @ w + b
    shift, scale, gate = jnp.split(mod[:, None, :], 3, axis=-1)
    return shift, scale, gate


def gated_deltanet(q, k, v, a, b, z, conv_w, a_log, dt_bias, norm_w):
    bsz, s, _ = q.shape
    qkv = jax.nn.silu(causal_depthwise_conv(jnp.concatenate([q, k, v], -1), conv_w))
    q, k, v = jnp.split(qkv, 3, axis=-1)
    q = l2_normalize(split_heads(q)) * HEAD_DIM ** -0.5
    k = l2_normalize(split_heads(k))
    v = split_heads(v)
    beta = jax.nn.sigmoid(b.astype(jnp.float32)).transpose(0, 2, 1)
    log_a = (-jnp.exp(a_log.astype(jnp.float32)) * jax.nn.softplus(a.astype(jnp.float32) + dt_bias)).transpose(0, 2, 1)
    C = GDN_CHUNK
    q, k, v, beta, log_a = (to_chunks(t, C) for t in (q, k, v, beta, log_a))
    bcum = jnp.cumsum(log_a, -1)
    incl = jnp.tril(jnp.ones((C, C), bool))
    strict = jnp.tril(jnp.ones((C, C), bool), -1)
    decay_incl = jnp.exp(jnp.where(incl, bcum[..., :, None] - bcum[..., None, :], -jnp.inf))
    decay_strict = jnp.where(strict, decay_incl, 0.0)
    lower = beta[..., :, None] * jnp.einsum('bhnid,bhnjd->bhnij', k, k) * decay_strict
    rhs = jnp.concatenate([beta[..., None] * v, (beta * jnp.exp(bcum))[..., None] * k], -1)
    sol = lax.linalg.triangular_solve(lower, rhs, left_side=True, lower=True, unit_diagonal=True)
    u0, w = jnp.split(sol, 2, axis=-1)
    attn = jnp.einsum('bhnid,bhnjd->bhnij', q, k) * decay_incl
    q_dec = q * jnp.exp(bcum)[..., None]
    k_dec = k * jnp.exp(bcum[..., -1:] - bcum)[..., None]
    chunk_decay = jnp.exp(bcum[..., -1])

    def step(state, xs):
        u0_n, w_n, attn_n, q_n, k_n, g_n = xs
        u = u0_n - jnp.einsum('bhck,bhkv->bhcv', w_n, state)
        o = jnp.einsum('bhck,bhkv->bhcv', q_n, state) + jnp.einsum('bhcs,bhsv->bhcv', attn_n, u)
        state = g_n[..., None, None] * state + jnp.einsum('bhck,bhcv->bhkv', k_n, u)
        return state, o

    xs = tuple(jnp.moveaxis(t, 2, 0) for t in (u0, w, attn, q_dec, k_dec, chunk_decay))
    state0 = jnp.zeros((bsz, N_HEADS, HEAD_DIM, HEAD_DIM), jnp.float32)
    _, o = lax.scan(step, state0, xs)
    o = jnp.moveaxis(o, 0, 2).reshape(bsz, N_HEADS, s, HEAD_DIM)
    o = merge_heads(rms_normalize(o) * norm_w) * jax.nn.silu(z)
    return o.astype(z.dtype)


def retention(q, k, v, g, positions, gn_w):
    bsz, s, _ = q.shape
    q = rope(split_heads(q), positions)
    k = rope(split_heads(k), positions) * HEAD_DIM ** -0.5
    v = split_heads(v)
    C = RET_CHUNK
    log_gamma = jnp.log1p(-(2.0 ** (-5.0 - jnp.arange(N_HEADS, dtype=jnp.float32))))
    idx = jnp.arange(C, dtype=jnp.float32)
    rel = idx[:, None] - idx[None, :]
    inner_decay = jnp.where(rel >= 0, jnp.exp(jnp.maximum(rel, 0.0) * log_gamma[:, None, None]), 0.0)
    xi = jnp.exp((idx + 1.0) * log_gamma[:, None])
    zeta = jnp.exp((C - 1.0 - idx) * log_gamma[:, None])
    chunk_decay = jnp.exp(C * log_gamma)
    qc, kc, vc = (to_chunks(t, C) for t in (q, k, v))
    scores = jnp.einsum('bhnid,bhnjd->bhnij', qc, kc) * inner_decay[:, None]
    inner = jnp.einsum('bhnij,bhnjd->bhnid', scores, vc)
    kv = jnp.einsum('bhnck,bhncv->bhnkv', kc * zeta[:, None, :, None], vc)

    def step(state, kv_n):
        return chunk_decay[None, :, None, None] * state + kv_n, state

    state0 = jnp.zeros((bsz, N_HEADS, HEAD_DIM, HEAD_DIM), jnp.float32)
    _, r_prev = lax.scan(step, state0, jnp.moveaxis(kv, 2, 0))
    r_prev = jnp.moveaxis(r_prev, 0, 2)
    cross = jnp.einsum('bhnck,bhnkv->bhncv', qc, r_prev) * xi[:, None, :, None]
    o = (inner + cross).reshape(bsz, N_HEADS, s, HEAD_DIM).astype(jnp.float32)
    mu = o.mean(-1, keepdims=True)
    var = jnp.square(o - mu).mean(-1, keepdims=True)
    o = merge_heads((o - mu) * lax.rsqrt(var + LN_EPS)) * gn_w
    return (jax.nn.silu(g) * o).astype(g.dtype)


def native_sparse_attention(q, k_cmp_in, v_cmp_in, k_sel_in, v_sel_in, k_win_in, v_win_in, gate_logits,
                            cmp_pe, ck_w1, ck_w2, cv_w1, cv_w2):
    bsz, s, _ = q.shape
    q = split_heads(q)
    scale = HEAD_DIM ** -0.5
    n_cmp = (s - NSA_CMP_LEN) // NSA_CMP_STRIDE + 1
    cmp_start = jnp.arange(n_cmp) * NSA_CMP_STRIDE
    cmp_tok = cmp_start[:, None] + jnp.arange(NSA_CMP_LEN)[None, :]

    def compress(t, w1, w2):
        blocks = (t[:, cmp_tok] + cmp_pe).reshape(bsz, n_cmp, NSA_CMP_LEN * HEAD_DIM)
        return jax.nn.silu(blocks @ w1) @ w2

    k_cmp = compress(k_cmp_in, ck_w1, ck_w2)
    v_cmp = compress(v_cmp_in, cv_w1, cv_w2)
    cmp_end = cmp_start + NSA_CMP_LEN - 1
    n_sel = s // NSA_SEL_LEN
    sel_start = jnp.arange(n_sel) * NSA_SEL_LEN
    overlap = jnp.clip(jnp.minimum(cmp_start[:, None] + NSA_CMP_LEN, sel_start[None, :] + NSA_SEL_LEN)
                       - jnp.maximum(cmp_start[:, None], sel_start[None, :]), 0, NSA_CMP_LEN).astype(jnp.float32) / NSA_CMP_LEN
    n_top = min(NSA_TOP_N, n_sel)
    k_blk = k_sel_in.reshape(bsz, n_sel, NSA_SEL_LEN, HEAD_DIM)
    v_blk = v_sel_in.reshape(bsz, n_sel, NSA_SEL_LEN, HEAD_DIM)
    k_win = jnp.pad(k_win_in, ((0, 0), (NSA_WINDOW, 0), (0, 0)))
    v_win = jnp.pad(v_win_in, ((0, 0), (NSA_WINDOW, 0), (0, 0)))
    gates = jax.nn.sigmoid(gate_logits.astype(jnp.float32)).reshape(bsz, s, N_HEADS, 3)
    blk_ids = jnp.arange(n_sel)
    blk_off = jnp.arange(NSA_SEL_LEN)
    win_off = jnp.arange(NSA_WINDOW + NSA_Q_BLOCK)
    gather = jax.vmap(lambda blocks, idx: blocks[idx])

    def block(i):
        t0 = i * NSA_Q_BLOCK
        tpos = t0 + jnp.arange(NSA_Q_BLOCK)
        qb = lax.dynamic_slice_in_dim(q, t0, NSA_Q_BLOCK, axis=2)
        p_cmp = masked_softmax(jnp.einsum('bhqd,bnd->bhqn', qb, k_cmp) * scale, cmp_end[None, :] <= tpos[:, None])
        o_cmp = jnp.einsum('bhqn,bnd->bhqd', p_cmp, v_cmp)
        importance = jnp.einsum('bhqn,nj->bqj', p_cmp, overlap)
        cur = tpos // NSA_SEL_LEN
        forced = (blk_ids[None, :] == 0) | (blk_ids[None, :] == cur[:, None]) | (blk_ids[None, :] == cur[:, None] - 1)
        causal_blk = sel_start[None, :] <= tpos[:, None]
        score = jnp.where(forced, NSA_FORCE_SCORE, jnp.where(causal_blk, importance, -1.0))
        _, sel = lax.top_k(score, n_top)
        k_g = gather(k_blk, sel).reshape(bsz, NSA_Q_BLOCK, n_top * NSA_SEL_LEN, HEAD_DIM)
        v_g = gather(v_blk, sel).reshape(bsz, NSA_Q_BLOCK, n_top * NSA_SEL_LEN, HEAD_DIM)
        kpos = (sel[..., None] * NSA_SEL_LEN + blk_off).reshape(bsz, NSA_Q_BLOCK, n_top * NSA_SEL_LEN)
        p_sel = masked_softmax(jnp.einsum('bhqd,bqkd->bhqk', qb, k_g) * scale, (kpos <= tpos[None, :, None])[:, None])
        o_sel = jnp.einsum('bhqk,bqkd->bhqd', p_sel, v_g)
        kw = lax.dynamic_slice_in_dim(k_win, t0, NSA_WINDOW + NSA_Q_BLOCK, axis=1)
        vw = lax.dynamic_slice_in_dim(v_win, t0, NSA_WINDOW + NSA_Q_BLOCK, axis=1)
        wpos = t0 - NSA_WINDOW + win_off
        wmask = (wpos[None, :] >= 0) & (wpos[None, :] <= tpos[:, None]) & (wpos[None, :] > tpos[:, None] - NSA_WINDOW)
        p_win = masked_softmax(jnp.einsum('bhqd,bkd->bhqk', qb, kw) * scale, wmask)
        o_win = jnp.einsum('bhqk,bkd->bhqd', p_win, vw)
        g = lax.dynamic_slice_in_dim(gates, t0, NSA_Q_BLOCK, axis=1).transpose(0, 2, 1, 3)
        return g[..., 0:1] * o_cmp + g[..., 1:2] * o_sel + g[..., 2:3] * o_win

    o = lax.map(block, jnp.arange(s // NSA_Q_BLOCK))
    return o.transpose(1, 0, 3, 2, 4).reshape(bsz, s, MIX_W).astype(k_cmp_in.dtype)


def forgetting_attention(q, k, v, f_logit, f_bias):
    bsz, s, _ = q.shape
    q, k, v = split_heads(q), split_heads(k), split_heads(v)
    scale = HEAD_DIM ** -0.5
    log_f = jax.nn.log_sigmoid(f_logit.astype(jnp.float32) + f_bias).transpose(0, 2, 1)
    cum = jnp.cumsum(log_f, -1)
    kpos = jnp.arange(s)

    def block(i):
        t0 = i * Q_BLOCK
        tpos = t0 + jnp.arange(Q_BLOCK)
        qb = lax.dynamic_slice_in_dim(q, t0, Q_BLOCK, axis=2)
        cb = lax.dynamic_slice_in_dim(cum, t0, Q_BLOCK, axis=2)
        logits = jnp.einsum('bhqd,bhkd->bhqk', qb, k) * scale + (cb[..., :, None] - cum[..., None, :])
        p = masked_softmax(logits, kpos[None, :] <= tpos[:, None])
        return jnp.einsum('bhqk,bhkd->bhqd', p, v)

    o = lax.map(block, jnp.arange(s // Q_BLOCK))
    return o.transpose(1, 0, 3, 2, 4).reshape(bsz, s, MIX_W).astype(v.dtype)


def hybrid_mixer(h, positions, w_in, gdn_conv_w, gdn_a_log, gdn_dt_bias, gdn_norm_w, ret_gn_w,
                 nsa_cmp_pe, nsa_ck_w1, nsa_ck_w2, nsa_cv_w1, nsa_cv_w2, fox_f_bias, branch_proj, w_gate, w_out):
    proj = jnp.einsum('bsd,de->bse', h, w_in)
    (gq, gk, gv, ga, gb, gz, rq, rk, rv, rg, nq, nkc, nvc, nks, nvs, nkw, nvw, ngate,
     fq, fk, fv, ff) = jnp.split(proj, IN_SPLITS, axis=-1)
    o_gdn = gated_deltanet(gq, gk, gv, ga, gb, gz, gdn_conv_w, gdn_a_log, gdn_dt_bias, gdn_norm_w)
    o_ret = retention(rq, rk, rv, rg, positions, ret_gn_w)
    o_nsa = native_sparse_attention(nq, nkc, nvc, nks, nvs, nkw, nvw, ngate, nsa_cmp_pe,
                                    nsa_ck_w1, nsa_ck_w2, nsa_cv_w1, nsa_cv_w2)
    o_fox = forgetting_attention(fq, fk, fv, ff, fox_f_bias)
    merged = [jax.nn.sigmoid(h @ w_gate[i]) * (o.astype(h.dtype) @ branch_proj[i])
              for i, o in enumerate((o_gdn, o_ret, o_nsa, o_fox))]
    y = merged[0] + merged[1] + merged[2] + merged[3]
    return y @ w_out


def moe(h, router_w, router_b, w1, w3, w2):
    bsz, s, _ = h.shape
    scores = jax.nn.sigmoid(jnp.einsum('bsd,de->bse', h, router_w).astype(jnp.float32))
    biased = scores + router_b.astype(jnp.float32)
    group_score = lax.top_k(biased.reshape(bsz, s, N_GROUPS, EXPERTS_PER_GROUP), TOP_K)[0].sum(-1)
    group = jnp.argmax(group_score, axis=-1)
    in_group = (jnp.arange(N_EXPERTS) // EXPERTS_PER_GROUP) == group[..., None]
    _, top_idx = lax.top_k(jnp.where(in_group, biased, -jnp.inf), TOP_K)
    top_w = jnp.take_along_axis(scores, top_idx, axis=-1)
    top_w = top_w / top_w.sum(-1, keepdims=True)
    combine = jnp.sum(jax.nn.one_hot(top_idx, N_EXPERTS, dtype=jnp.float32) * top_w[..., None], axis=-2).astype(h.dtype)

    def per_sequence(args):
        hs, cs = args
        hidden = jax.nn.silu(jnp.einsum('sd,edf->sef', hs, w1)) * jnp.einsum('sd,edf->sef', hs, w3)
        return jnp.einsum('sef,efd->sd', hidden * cs[..., None], w2)

    return lax.map(per_sequence, (h, combine))


def setup_inputs(seed: int = 0) -> dict:
    key = jax.random.key(seed)
    ks = jax.random.split(key, 32)
    f32 = jnp.float32

    def nrm(k, shape, scale):
        return jax.random.normal(k, shape, f32) * scale

    dt = jnp.exp(jax.random.uniform(ks[7], (DEPTH, N_HEADS), f32, math.log(1e-3), math.log(1e-1)))
    return {
        'x': nrm(ks[0], (BATCH, SEQ, D_MODEL), 1.0),
        'c': nrm(ks[1], (BATCH, D_MODEL), 1.0),
        'positions': jnp.tile(jnp.arange(SEQ, dtype=jnp.int32)[None, :], (BATCH, 1)),
        'ada_w': nrm(ks[2], (DEPTH, 2, D_MODEL, 3 * D_MODEL), 0.5 * D_MODEL ** -0.5),
        'ada_b': nrm(ks[3], (DEPTH, 2, 3 * D_MODEL), 0.02),
        'w_in': nrm(ks[4], (DEPTH, D_MODEL, IN_WIDTH), D_MODEL ** -0.5),
        'gdn_conv_w': nrm(ks[5], (DEPTH, CONV_K, 3 * MIX_W), CONV_K ** -0.5),
        'gdn_a_log': jnp.log(jax.random.uniform(ks[6], (DEPTH, N_HEADS), f32, 1.0, 16.0)),
        'gdn_dt_bias': dt + jnp.log(-jnp.expm1(-dt)),
        'gdn_norm_w': 1.0 + nrm(ks[8], (DEPTH, HEAD_DIM), 0.02),
        'ret_gn_w': 1.0 + nrm(ks[9], (DEPTH, MIX_W), 0.02),
        'nsa_cmp_pe': nrm(ks[10], (DEPTH, NSA_CMP_LEN, HEAD_DIM), 0.02),
        'nsa_ck_w1': nrm(ks[11], (DEPTH, NSA_CMP_LEN * HEAD_DIM, NSA_CMP_HIDDEN), (NSA_CMP_LEN * HEAD_DIM) ** -0.5),
        'nsa_ck_w2': nrm(ks[12], (DEPTH, NSA_CMP_HIDDEN, HEAD_DIM), NSA_CMP_HIDDEN ** -0.5),
        'nsa_cv_w1': nrm(ks[13], (DEPTH, NSA_CMP_LEN * HEAD_DIM, NSA_CMP_HIDDEN), (NSA_CMP_LEN * HEAD_DIM) ** -0.5),
        'nsa_cv_w2': nrm(ks[14], (DEPTH, NSA_CMP_HIDDEN, HEAD_DIM), NSA_CMP_HIDDEN ** -0.5),
        'fox_f_bias': jnp.linspace(3.0, 6.0, N_HEADS, dtype=f32)[None, :] + nrm(ks[15], (DEPTH, N_HEADS), 0.1),
        'branch_proj': nrm(ks[16], (DEPTH, N_BRANCH, MIX_W, D_MODEL), DN_BETA * MIX_W ** -0.5),
        'w_gate': nrm(ks[17], (DEPTH, N_BRANCH, D_MODEL, D_MODEL), D_MODEL ** -0.5),
        'w_out': nrm(ks[18], (DEPTH, D_MODEL, D_MODEL), DN_BETA * D_MODEL ** -0.5),
        'ln_g': 1.0 + nrm(ks[19], (DEPTH, 2, D_MODEL), 0.02),
        'ln_b': nrm(ks[20], (DEPTH, 2, D_MODEL), 0.02),
        'router_w': nrm(ks[21], (D_MODEL, N_EXPERTS), D_MODEL ** -0.5),
        'router_b': nrm(ks[22], (N_EXPERTS,), 0.01),
        'exp_w1': nrm(ks[23], (DEPTH, N_EXPERTS, D_MODEL, D_EXPERT), D_MODEL ** -0.5),
        'exp_w3': nrm(ks[24], (DEPTH, N_EXPERTS, D_MODEL, D_EXPERT), D_MODEL ** -0.5),
        'exp_w2': nrm(ks[25], (DEPTH, N_EXPERTS, D_EXPERT, D_MODEL), DN_BETA * D_EXPERT ** -0.5),
    }


def reference(x, c, positions, ada_w, ada_b, w_in, gdn_conv_w, gdn_a_log, gdn_dt_bias, gdn_norm_w,
              ret_gn_w, nsa_cmp_pe, nsa_ck_w1, nsa_ck_w2, nsa_cv_w1, nsa_cv_w2, fox_f_bias,
              branch_proj, w_gate, w_out, ln_g, ln_b, router_w, router_b, exp_w1, exp_w3, exp_w2):
    for l in range(DEPTH):
        shift, scale, gate = ada_modulation(c, ada_w[l, 0], ada_b[l, 0])
        h = x * (1.0 + scale) + shift
        y = hybrid_mixer(h, positions, w_in[l], gdn_conv_w[l], gdn_a_log[l], gdn_dt_bias[l], gdn_norm_w[l],
                         ret_gn_w[l], nsa_cmp_pe[l], nsa_ck_w1[l], nsa_ck_w2[l], nsa_cv_w1[l], nsa_cv_w2[l],
                         fox_f_bias[l], branch_proj[l], w_gate[l], w_out[l])
        x = layer_norm(DN_ALPHA * x + gate * y, ln_g[l, 0], ln_b[l, 0])
        shift, scale, gate = ada_modulation(c, ada_w[l, 1], ada_b[l, 1])
        h = x * (1.0 + scale) + shift
        y = moe(h, router_w, router_b, exp_w1[l], exp_w3[l], exp_w2[l])
        x = layer_norm(DN_ALPHA * x + gate * y, ln_g[l, 1], ln_b[l, 1])
    return x
```

```python
import functools
import math

import jax
import jax.numpy as jnp
import numpy as np
from jax import lax
from jax.experimental import pallas as pl
from jax.experimental.pallas import tpu as pltpu

F32 = jnp.float32
BF16 = jnp.bfloat16

N_HEADS = 4
HEAD_DIM = 64
MIX_W = N_HEADS * HEAD_DIM
GDN_CHUNK = 64
CONV_K = 4
RET_CHUNK = 128
ROPE_BASE = 10000.0
NSA_CMP_LEN = 32
NSA_CMP_STRIDE = 16
NSA_SEL_LEN = 64
NSA_TOP_N = 16
NSA_WINDOW = 512
NSA_FORCE_SCORE = 1.0e4
N_EXPERTS = 16
N_GROUPS = 4
EXPERTS_PER_GROUP = N_EXPERTS // N_GROUPS
LN_EPS = 1e-5
LANES = 128
NEG_BIG = -1.0e30
SEL_MASK_BIAS = -30000.0

IN_WIDTHS = (
    MIX_W, MIX_W, MIX_W, N_HEADS, N_HEADS, MIX_W,
    MIX_W, MIX_W, MIX_W, MIX_W,
    MIX_W, HEAD_DIM, HEAD_DIM, HEAD_DIM, HEAD_DIM, HEAD_DIM, HEAD_DIM, 3 * N_HEADS,
    MIX_W, MIX_W, MIX_W, N_HEADS,
)
_IN_OFF = np.concatenate([[0], np.cumsum(IN_WIDTHS)]).astype(int)
(_GQ, _GK, _GV, _GA, _GB, _GZ, _RQ, _RK, _RV, _RG, _NQ, _NKC, _NVC, _NKS, _NVS, _NKW, _NVW, _NGATE,
 _FQ, _FK, _FV, _FF) = range(22)
SM_A, SM_B, SM_GATE, SM_F = 0, 4, 8, 20

VMEM_LIMIT = 56 * 1024 * 1024


def _cp(sem, vmem=None):
    return pltpu.CompilerParams(dimension_semantics=sem, vmem_limit_bytes=vmem)


def _sigmoid(x):
    return 1.0 / (1.0 + jnp.exp(-x))


def _silu(x):
    return x * _sigmoid(x)


def _softplus(x):
    return jnp.maximum(x, 0.0) + jnp.log1p(jnp.exp(-jnp.abs(x)))


def _dot(a, b):
    return jnp.dot(a, b, preferred_element_type=F32)


def _dot_nt(a, b):
    return lax.dot_general(a, b, (((1,), (1,)), ((), ())), preferred_element_type=F32)


def _split3(a):
    a1 = a.astype(BF16)
    r = a - a1.astype(F32)
    a2 = r.astype(BF16)
    a3 = (r - a2.astype(F32)).astype(BF16)
    return a1, a2, a3


def _dot_x(a, m):
    a1, a2, a3 = _split3(a)
    return _dot(a1, m) + _dot(a2, m) + _dot(a3, m)


def _xdot(m, a):
    a1, a2, a3 = _split3(a)
    return _dot(m, a1) + _dot(m, a2) + _dot(m, a3)


def _dot_hp(a, b):
    a1, a2, _ = _split3(a)
    b1, b2, _ = _split3(b)
    return _dot(a1, b1) + (_dot(a1, b2) + _dot(a2, b1))


def _iota(shape, dim):
    return lax.broadcasted_iota(jnp.int32, shape, dim)


def _head_ones():
    return (_iota((MIX_W, MIX_W), 0) // HEAD_DIM == _iota((MIX_W, MIX_W), 1) // HEAD_DIM)


def _head_sum(x):
    return _dot_x(x, _head_ones().astype(BF16))


def _ada_body(c_ref, w_ref, b_ref, o_ref):
    mod = _dot_hp(_silu(c_ref[...]), w_ref[0]) + b_ref[0]
    o_ref[0] = jnp.where(pl.program_id(1) == 1, 1.0 + mod, mod)


def _ada_mod(c, ada_w, ada_b):
    depth = ada_w.shape[0]
    bsz, d = c.shape
    n = depth * 2
    w = ada_w.reshape(n, d, 3 * d)
    b = ada_b.reshape(n, 1, 3 * d)
    out = pl.pallas_call(
        _ada_body, grid=(n, 3),
        in_specs=[pl.BlockSpec((bsz, d), lambda i, j: (0, 0)),
                  pl.BlockSpec((1, d, d), lambda i, j: (i, 0, j)),
                  pl.BlockSpec((1, 1, d), lambda i, j: (i, 0, j))],
        out_specs=pl.BlockSpec((1, bsz, d), lambda i, j: (i, 0, j)),
        out_shape=jax.ShapeDtypeStruct((n, bsz, 3 * d), F32),
        compiler_params=_cp(("arbitrary", "arbitrary")), name="ada_mod",
    )(c, w, b)
    return out.reshape(depth, 2, bsz, 1, 3 * d)


IN_TS = 512
_W_GROUPS = (768, 256, 1024, 256, 384, 768, 128)
_W_OFF = np.concatenate([[0], np.cumsum(_W_GROUPS)]).astype(int)
IN_CAT_W = int(_W_OFF[-1])


def _cat_in_weights(w_in):
    def col(i):
        return w_in[:, _IN_OFF[i]:_IN_OFF[i + 1]]
    small = jnp.concatenate([col(_GA), col(_GB), col(_NGATE), col(_FF)], axis=1)
    small = jnp.pad(small, ((0, 0), (0, LANES - small.shape[1])))
    scale = HEAD_DIM ** -0.5
    cat = jnp.concatenate([
        col(_GQ), col(_GK), col(_GV), col(_GZ),
        col(_RQ), col(_RK) * scale, col(_RV), col(_RG),
        col(_NQ) * scale,
        col(_NKC), col(_NVC), col(_NKS), col(_NVS), col(_NKW), col(_NVW),
        col(_FQ) * scale, col(_FK), col(_FV),
        small], axis=1)
    return cat.astype(BF16), small.T.astype(BF16)


def _inproj_body(x_ref, sc_ref, sh_ref, w_ref, wst_ref,
                 gqkv_ref, gz_ref, r_ref, nq_ref, kc_ref, vc_ref, ks_ref, vs_ref, kw_ref, vw_ref,
                 fq_ref, fk_ref, fv_ref, sm_ref, smt_ref):
    h = (x_ref[0] * sc_ref[0] + sh_ref[0]).astype(BF16)

    def proj(g):
        return _dot(h, w_ref[:, _W_OFF[g]:_W_OFF[g + 1]])

    gqkv_ref[0] = proj(0)
    gz_ref[0] = proj(1)
    r_ref[0] = proj(2)
    nq = proj(3)
    for hd in range(N_HEADS):
        nq_ref[0, hd] = nq[:, hd * HEAD_DIM:(hd + 1) * HEAD_DIM].astype(BF16)
    nkv = proj(4)
    for i, ref in enumerate((kc_ref, vc_ref, ks_ref, vs_ref, kw_ref, vw_ref)):
        ref[0] = nkv[:, i * HEAD_DIM:(i + 1) * HEAD_DIM].astype(ref.dtype)
    f = proj(5)
    for j, ref in enumerate((fq_ref, fk_ref, fv_ref)):
        for hd in range(N_HEADS):
            lo = j * MIX_W + hd * HEAD_DIM
            ref[0, hd] = f[:, lo:lo + HEAD_DIM].astype(BF16)
    sm_ref[0] = proj(6)
    smt_ref[0] = _dot_nt(wst_ref[...], h)


def _in_proj(x, sc1p, shift, wcat, wst):
    bsz, s, d = x.shape
    ts = min(IN_TS, s)
    tok = lambda w, dt: jax.ShapeDtypeStruct((bsz, s, w), dt)
    hm = jax.ShapeDtypeStruct((bsz, N_HEADS, s, HEAD_DIM), BF16)
    out_shape = [tok(768, F32), tok(256, F32), tok(1024, F32), hm,
                 tok(64, F32), tok(64, F32), tok(64, BF16), tok(64, BF16), tok(64, BF16), tok(64, BF16),
                 hm, hm, hm, tok(LANES, F32), jax.ShapeDtypeStruct((bsz, LANES, s), F32)]
    tspec = lambda w: pl.BlockSpec((1, ts, w), lambda b, i: (b, i, 0))
    hspec = pl.BlockSpec((1, N_HEADS, ts, HEAD_DIM), lambda b, i: (b, 0, i, 0))
    out_specs = [tspec(768), tspec(256), tspec(1024), hspec,
                 tspec(64), tspec(64), tspec(64), tspec(64), tspec(64), tspec(64),
                 hspec, hspec, hspec, tspec(LANES), pl.BlockSpec((1, LANES, ts), lambda b, i: (b, 0, i))]
    vec = pl.BlockSpec((1, 1, d), lambda b, i: (b, 0, 0))
    return pl.pallas_call(
        _inproj_body, grid=(bsz, s // ts),
        in_specs=[tspec(d), vec, vec,
                  pl.BlockSpec((d, IN_CAT_W), lambda b, i: (0, 0)),
                  pl.BlockSpec((LANES, d), lambda b, i: (0, 0))],
        out_specs=out_specs, out_shape=out_shape,
        compiler_params=_cp(("arbitrary", "arbitrary"), VMEM_LIMIT), name="in_proj",
    )(x, sc1p, shift, wcat, wst)


CUM_T = 512


def _foxcum_body(sm_ref, smt_ref, brow_ref, bcol_ref, ccol_ref, crow_ref, carry_r, carry_c):
    @pl.when(pl.program_id(1) == 0)
    def _():
        carry_r[...] = jnp.zeros_like(carry_r)
        carry_c[...] = jnp.zeros_like(carry_c)

    t = sm_ref.shape[1]

    def log_sigmoid(v):
        return jnp.minimum(v, 0.0) - jnp.log1p(jnp.exp(-jnp.abs(v)))

    lower = (_iota((t, t), 0) >= _iota((t, t), 1)).astype(BF16)
    upper = (_iota((t, t), 0) <= _iota((t, t), 1)).astype(BF16)
    cc = _xdot(lower, log_sigmoid(sm_ref[0] + brow_ref[...])) + carry_r[...]
    ccol_ref[0] = cc
    carry_r[...] = cc[t - 1:t, :]
    cr = _dot_x(log_sigmoid(smt_ref[0] + bcol_ref[...]), upper) + carry_c[...]
    crow_ref[0] = cr[16:24, :]
    carry_c[...] = cr[:, t - 1:t]


def _fox_cum(small, small_t, f_bias):
    bsz, s, _ = small.shape
    t = min(CUM_T, s)
    brow = jnp.zeros((1, LANES), F32).at[0, SM_F:SM_F + N_HEADS].set(f_bias)
    bcol = brow.reshape(LANES, 1)
    return pl.pallas_call(
        _foxcum_body, grid=(bsz, s // t),
        in_specs=[pl.BlockSpec((1, t, LANES), lambda b, i: (b, i, 0)),
                  pl.BlockSpec((1, LANES, t), lambda b, i: (b, 0, i)),
                  pl.BlockSpec((1, LANES), lambda b, i: (0, 0)),
                  pl.BlockSpec((LANES, 1), lambda b, i: (0, 0))],
        out_specs=[pl.BlockSpec((1, t, LANES), lambda b, i: (b, i, 0)),
                   pl.BlockSpec((1, 8, t), lambda b, i: (b, 0, i))],
        out_shape=[jax.ShapeDtypeStruct((bsz, s, LANES), F32), jax.ShapeDtypeStruct((bsz, 8, s), F32)],
        scratch_shapes=[pltpu.VMEM((1, LANES), F32), pltpu.VMEM((LANES, 1), F32)],
        compiler_params=_cp(("arbitrary", "arbitrary")), name="fox_cum",
    )(small, small_t, brow, bcol)


FOX_T = 512


def _fox_body(q_ref, k_ref, v_ref, ccol_ref, crow_ref, o_ref):
    i = pl.program_id(1)
    t = q_ref.shape[2]
    for hd in range(N_HEADS):
        q = q_ref[0, hd]
        cq = ccol_ref[0][:, SM_F + hd:SM_F + hd + 1]

        def scores(j):
            k = k_ref[0, hd, pl.ds(j * t, t), :]
            ck = crow_ref[0, SM_F - 16 + hd:SM_F - 15 + hd, pl.ds(j * t, t)]
            return _dot_nt(q, k) + (cq - ck)

        def update(j, s, carry):
            m, l, acc = carry
            m_new = jnp.maximum(m, jnp.max(s, axis=-1, keepdims=True))
            alpha = jnp.exp(m - m_new)
            p = jnp.exp(s - m_new)
            v = v_ref[0, hd, pl.ds(j * t, t), :]
            return m_new, alpha * l + jnp.sum(p, axis=-1, keepdims=True), alpha * acc + _dot(p.astype(BF16), v)

        carry = (jnp.full((t, 1), NEG_BIG, F32), jnp.zeros((t, 1), F32), jnp.zeros((t, HEAD_DIM), F32))
        carry = lax.fori_loop(0, i, lambda j, c: update(j, scores(j), c), carry)
        causal = _iota((t, t), 1) <= _iota((t, t), 0)
        m, l, acc = update(i, jnp.where(causal, scores(i), NEG_BIG), carry)
        o_ref[0, hd] = (acc / l).astype(o_ref.dtype)


def _fox_attn(q, k, v, ccol, crow):
    bsz, nh, s, hd = q.shape
    t = min(FOX_T, s)
    full = pl.BlockSpec((1, nh, s, hd), lambda b, i: (b, 0, 0, 0))
    tile = pl.BlockSpec((1, nh, t, hd), lambda b, i: (b, 0, i, 0))
    return pl.pallas_call(
        _fox_body, grid=(bsz, s // t),
        in_specs=[tile, full, full,
                  pl.BlockSpec((1, t, LANES), lambda b, i: (b, i, 0)),
                  pl.BlockSpec((1, 8, s), lambda b, i: (b, 0, 0))],
        out_specs=tile, out_shape=jax.ShapeDtypeStruct((bsz, nh, s, hd), BF16),
        compiler_params=_cp(("arbitrary", "arbitrary")), name="fox_attn",
    )(q, k, v, ccol, crow)


GDN_T = 256


def _gdn_body(qkv_ref, sm_ref, smt_ref, z_ref, cw_ref, alr_ref, dtr_ref, alc_ref, dtc_ref, nw_ref,
              o_ref, xbuf, state):
    t, c = GDN_T, GDN_CHUNK

    @pl.when(pl.program_id(1) == 0)
    def _():
        xbuf[0:8, :] = jnp.zeros((8, 3 * MIX_W), F32)
        state[...] = jnp.zeros_like(state)

    xbuf[8:8 + t, :] = qkv_ref[0]
    conv = cw_ref[0:1, :] * xbuf[pl.ds(8 - CONV_K + 1, t), :]
    for kk in range(1, CONV_K):
        conv = conv + cw_ref[kk:kk + 1, :] * xbuf[pl.ds(8 - CONV_K + 1 + kk, t), :]
    xbuf[0:8, :] = xbuf[t:t + 8, :]
    act = _silu(conv)
    q, k, v = act[:, :MIX_W], act[:, MIX_W:2 * MIX_W], act[:, 2 * MIX_W:]
    q = q * lax.rsqrt(_head_sum(q * q) + 1e-6) * HEAD_DIM ** -0.5
    k = k * lax.rsqrt(_head_sum(k * k) + 1e-6)

    ri, ci = _iota((t, t), 0), _iota((t, t), 1)
    same = (ri // c) == (ci // c)
    incl = same & (ci <= ri)
    strict = same & (ci < ri)
    sm, smt = sm_ref[0], smt_ref[0]
    la_col = -jnp.exp(alr_ref[...]) * _softplus(sm + dtr_ref[...])
    la_row = -jnp.exp(alc_ref[...]) * _softplus(smt + dtc_ref[...])
    beta_col = _sigmoid(sm)
    bc_col = _xdot(incl.astype(BF16), la_col)
    last_col = _xdot(same.astype(BF16), la_col)
    bc_row = _dot_x(la_row, (same & (ri <= ci)).astype(BF16))
    er, ec = _iota((LANES, MIX_W), 0), _iota((LANES, MIX_W), 1) // HEAD_DIM
    exp_a = (er == ec + SM_A).astype(BF16)
    exp_b = (er == ec + SM_B).astype(BF16)
    bc_x, last_x, beta_x = _dot_x(bc_col, exp_a), _dot_x(last_col, exp_a), _dot_x(beta_col, exp_b)
    eb = jnp.exp(bc_x)
    q_dec = q * eb
    k_dec = k * jnp.exp(last_x - bc_x)
    rhs_v = beta_x * v
    rhs_k = beta_x * eb * k
    rhs = jnp.concatenate([rhs_v, rhs_k], axis=1).astype(BF16)

    kb = k.astype(BF16)
    lane_head = _iota((1, MIX_W), 1) // HEAD_DIM
    u0, w, attn = rhs_v, rhs_k, []
    for hd in range(N_HEADS):
        mh = lane_head == hd
        gram = _dot_nt(jnp.where(mh, k, 0.0).astype(BF16), kb)
        qk = _dot_nt(jnp.where(mh, q, 0.0).astype(BF16), kb)
        dec = jnp.exp(jnp.minimum(bc_col[:, SM_A + hd:SM_A + hd + 1] - bc_row[SM_A + hd:SM_A + hd + 1, :], 0.0))
        attn.append(jnp.where(incl, qk * dec, 0.0))
        p = jnp.where(strict, -(beta_col[:, SM_B + hd:SM_B + hd + 1] * gram * dec), 0.0)
        tr = p
        for _ in range(5):
            pb = p.astype(BF16)
            p = _dot(pb, pb)
            tr = tr + p + _dot(tr.astype(BF16), p.astype(BF16))
        sol = _dot(tr.astype(BF16), rhs)
        u0 = u0 + jnp.where(mh, sol[:, :MIX_W], 0.0)
        w = w + jnp.where(mh, sol[:, MIX_W:], 0.0)

    head_blk = _head_ones()
    k_dec_t = k_dec.T.astype(BF16)
    outs = []
    for n in range(t // c):
        rows = slice(n * c, (n + 1) * c)
        s_old = state[...]
        sb = s_old.astype(BF16)
        u_n = u0[rows] - _dot(w[rows].astype(BF16), sb)
        parts = ([jnp.zeros((n * c, MIX_W), F32)] if n else []) + [u_n]
        if t - (n + 1) * c:
            parts.append(jnp.zeros((t - (n + 1) * c, MIX_W), F32))
        u_full = jnp.concatenate(parts, axis=0).astype(BF16)
        oa = _dot(jnp.concatenate([a[rows] for a in attn], axis=0).astype(BF16), u_full)
        o_n = _dot(q_dec[rows].astype(BF16), sb)
        for hd in range(N_HEADS):
            o_n = o_n + jnp.where(lane_head == hd, oa[hd * c:(hd + 1) * c], 0.0)
        outs.append(o_n)
        g_row = jnp.exp(last_x[n * c:n * c + 1, :])
        state[...] = g_row * s_old + jnp.where(head_blk, _dot(k_dec_t, u_full), 0.0)
    o = jnp.concatenate(outs, axis=0)
    o = o * lax.rsqrt(_head_sum(o * o) * (1.0 / HEAD_DIM) + 1e-6) * nw_ref[...]
    o_ref[0] = (o * _silu(z_ref[0])).astype(o_ref.dtype)


def _lane_vec(vals, off):
    return jnp.zeros((1, LANES), F32).at[0, off:off + vals.shape[0]].set(vals)


def _gdn(qkv, small, small_t, z, conv_w, a_log, dt_bias, norm_w):
    bsz, s, _ = qkv.shape
    t = GDN_T
    alr, dtr = _lane_vec(a_log, SM_A), _lane_vec(dt_bias, SM_A)
    nw = jnp.tile(norm_w, N_HEADS).reshape(1, MIX_W)
    const = lambda shape: pl.BlockSpec(shape, lambda b, i: (0,) * len(shape))
    return pl.pallas_call(
        _gdn_body, grid=(bsz, s // t),
        in_specs=[pl.BlockSpec((1, t, 3 * MIX_W), lambda b, i: (b, i, 0)),
                  pl.BlockSpec((1, t, LANES), lambda b, i: (b, i, 0)),
                  pl.BlockSpec((1, LANES, t), lambda b, i: (b, 0, i)),
                  pl.BlockSpec((1, t, MIX_W), lambda b, i: (b, i, 0)),
                  const((CONV_K, 3 * MIX_W)), const((1, LANES)), const((1, LANES)),
                  const((LANES, 1)), const((LANES, 1)), const((1, MIX_W))],
        out_specs=pl.BlockSpec((1, t, MIX_W), lambda b, i: (b, i, 0)),
        out_shape=jax.ShapeDtypeStruct((bsz, s, MIX_W), BF16),
        scratch_shapes=[pltpu.VMEM((t + 8, 3 * MIX_W), F32), pltpu.VMEM((MIX_W, MIX_W), F32)],
        compiler_params=_cp(("arbitrary", "arbitrary")), name="gdn",
    )(qkv, small, small_t, z, conv_w, alr, dtr, alr.reshape(LANES, 1), dtr.reshape(LANES, 1), nw)


RET_T = 512
_RET_LOG_GAMMA = tuple(math.log1p(-(2.0 ** (-5.0 - h))) for h in range(N_HEADS))


def _per_head(lane_head, vals):
    out = jnp.full(lane_head.shape, vals[0], F32)
    for hd in range(1, N_HEADS):
        out = jnp.where(lane_head == hd, vals[hd], out)
    return out


def _ret_body(r_ref, pos_ref, gnw_ref, o_ref, state):
    t, c = r_ref.shape[1], RET_CHUNK

    @pl.when(pl.program_id(1) == 0)
    def _():
        state[...] = jnp.zeros_like(state)

    x = r_ref[0]
    lane = _iota((1, MIX_W), 1)
    lane_head = lane // HEAD_DIM
    half = HEAD_DIM // 2
    inv_freq = jnp.exp((lane % half).astype(F32) * (-math.log(ROPE_BASE) / half))
    ang = pos_ref[0].astype(F32) * inv_freq
    cos, sin = jnp.cos(ang), jnp.sin(ang)
    first = (lane % HEAD_DIM) < half

    def rope(a):
        rot = jnp.where(first, -pltpu.roll(a, MIX_W - half, 1), pltpu.roll(a, half, 1))
        return a * cos + rot * sin

    q, k = rope(x[:, :MIX_W]), rope(x[:, MIX_W:2 * MIX_W])
    v, g = x[:, 2 * MIX_W:3 * MIX_W], x[:, 3 * MIX_W:]

    lg = _per_head(lane_head, _RET_LOG_GAMMA)
    cidx = _iota((c, 1), 0).astype(F32)
    xi = jnp.exp((cidx + 1.0) * lg)
    zeta = jnp.exp((c - 1.0 - cidx) * lg)
    chunk_decay = jnp.exp(float(c) * lg)
    rel = (_iota((c, c), 0) - _iota((c, c), 1)).astype(F32)
    dec = jnp.concatenate([jnp.where(rel >= 0, jnp.exp(jnp.maximum(rel, 0.0) * _RET_LOG_GAMMA[hd]), 0.0)
                           for hd in range(N_HEADS)], axis=0)
    head_blk = _head_ones()
    outs = []
    for n in range(t // c):
        rows = slice(n * c, (n + 1) * c)
        qn, kn, vn = q[rows], k[rows], v[rows].astype(BF16)
        qs = jnp.concatenate([jnp.where(lane_head == hd, qn, 0.0) for hd in range(N_HEADS)], axis=0).astype(BF16)
        sc = _dot_nt(qs, kn.astype(BF16)) * dec
        res = _dot(sc.astype(BF16), vn)
        s_old = state[...]
        o_n = _dot(qn.astype(BF16), s_old.astype(BF16)) * xi
        for hd in range(N_HEADS):
            o_n = o_n + jnp.where(lane_head == hd, res[hd * c:(hd + 1) * c], 0.0)
        outs.append(o_n)
        kv = _dot((kn * zeta).T.astype(BF16), vn)
        state[...] = chunk_decay * s_old + jnp.where(head_blk, kv, 0.0)
    o = jnp.concatenate(outs, axis=0)
    mu = _head_sum(o) * (1.0 / HEAD_DIM)
    xc = o - mu
    var = _head_sum(xc * xc) * (1.0 / HEAD_DIM)
    y = xc * lax.rsqrt(var + LN_EPS) * gnw_ref[...]
    o_ref[0] = (_silu(g) * y).astype(o_ref.dtype)


def _retention(r, positions, gn_w):
    bsz, s, _ = r.shape
    t = min(RET_T, s)
    return pl.pallas_call(
        _ret_body, grid=(bsz, s // t),
        in_specs=[pl.BlockSpec((1, t, 4 * MIX_W), lambda b, i: (b, i, 0)),
                  pl.BlockSpec((1, t, 1), lambda b, i: (b, i, 0)),
                  pl.BlockSpec((1, MIX_W), lambda b, i: (0, 0))],
        out_specs=pl.BlockSpec((1, t, MIX_W), lambda b, i: (b, i, 0)),
        out_shape=jax.ShapeDtypeStruct((bsz, s, MIX_W), BF16),
        scratch_shapes=[pltpu.VMEM((MIX_W, MIX_W), F32)],
        compiler_params=_cp(("arbitrary", "arbitrary")), name="retention",
    )(r, positions.reshape(bsz, s, 1), gn_w.reshape(1, MIX_W))


def _nsacmp_body(kc_ref, vc_ref, pe_ref, kw1_ref, kw2_ref, vw1_ref, vw2_ref, ko_ref, vo_ref):
    half = NSA_CMP_STRIDE * HEAD_DIM
    nc = kc_ref.shape[1]
    for x_ref, w1_ref, w2_ref, o_ref in ((kc_ref, kw1_ref, kw2_ref, ko_ref), (vc_ref, vw1_ref, vw2_ref, vo_ref)):
        x = x_ref[0]
        top = _dot_hp(x, w1_ref[0:half, :])
        bot = _dot_hp(x, w1_ref[half:2 * half, :])
        bias = _dot_hp(pe_ref[...], w1_ref[...])[0:1, :]
        hidden = top + pltpu.roll(bot, nc - 1, 0) + bias
        o_ref[0] = _dot_hp(_silu(hidden), w2_ref[...])


def _nsa_compress(kc, vc, pe, ck_w1, ck_w2, cv_w1, cv_w2):
    bsz, s, hd = kc.shape
    nc = s // NSA_CMP_STRIDE
    wide = NSA_CMP_STRIDE * hd
    pe8 = jnp.broadcast_to(pe.reshape(1, NSA_CMP_LEN * hd), (8, NSA_CMP_LEN * hd))
    const = lambda a: pl.BlockSpec(a.shape, lambda b: (0,) * a.ndim)
    xspec = pl.BlockSpec((1, nc, wide), lambda b: (b, 0, 0))
    ospec = pl.BlockSpec((1, nc, hd), lambda b: (b, 0, 0))
    return pl.pallas_call(
        _nsacmp_body, grid=(bsz,),
        in_specs=[xspec, xspec, const(pe8), const(ck_w1), const(ck_w2), const(cv_w1), const(cv_w2)],
        out_specs=[ospec, ospec],
        out_shape=[jax.ShapeDtypeStruct((bsz, nc, hd), F32)] * 2,
        compiler_params=_cp(("arbitrary",)), name="nsa_compress",
    )(kc.reshape(bsz, nc, wide), vc.reshape(bsz, nc, wide), pe8, ck_w1, ck_w2, cv_w1, cv_w2)


NSA_TQ = 128
NSA_TK = 256


def _softmax_rows(s, valid):
    m = jnp.max(jnp.where(valid, s, NEG_BIG), axis=-1, keepdims=True)
    m = jnp.where(m > 0.5 * NEG_BIG, m, 0.0)
    e = jnp.where(valid, jnp.exp(s - m), 0.0)
    den = jnp.sum(e, axis=-1, keepdims=True)
    return e / jnp.where(den > 0.0, den, 1.0)


def _nsa_body(q_ref, kc_ref, vc_ref, ks_ref, vs_ref, kw_ref, vw_ref, sm_ref, o_ref, kaug):
    i = pl.program_id(1)
    tq, hd = q_ref.shape[2], HEAD_DIM
    s, nc = ks_ref.shape[1], kc_ref.shape[1]
    nsel = s // NSA_SEL_LEN
    n_top = min(NSA_TOP_N, nsel)
    rows = N_HEADS * tq

    @pl.when(i == 0)
    def _():
        place = (_iota((hd, LANES), 0) == _iota((hd, LANES), 1)).astype(BF16)
        onehot = (_iota((s, LANES), 1) - hd) == (_iota((s, LANES), 0) // NSA_SEL_LEN)
        kaug[...] = (_dot(ks_ref[0], place) + onehot.astype(F32)).astype(BF16)

    t0 = i * tq
    q2 = q_ref[0].reshape(rows, hd)
    tpos = t0 + _iota((tq, 1), 0)
    tpos4 = t0 + _iota((rows, 1), 0) % tq

    cmp_end = _iota((1, nc), 1) * NSA_CMP_STRIDE + (NSA_CMP_LEN - 1)
    p_cmp = _softmax_rows(_dot_nt(q2, kc_ref[0].astype(BF16)), cmp_end <= tpos4)
    o_cmp = _dot(p_cmp.astype(BF16), vc_ref[0].astype(BF16))

    psum = p_cmp[0:tq] + p_cmp[tq:2 * tq] + p_cmp[2 * tq:3 * tq] + p_cmp[3 * tq:4 * tq]
    cs, ss = _iota((nc, hd), 0) * NSA_CMP_STRIDE, _iota((nc, hd), 1) * NSA_SEL_LEN
    overlap = jnp.clip(jnp.minimum(cs + NSA_CMP_LEN, ss + NSA_SEL_LEN) - jnp.maximum(cs, ss), 0, NSA_CMP_LEN)
    importance = _dot_x(psum, (overlap.astype(F32) * (1.0 / NSA_CMP_LEN)).astype(BF16))
    blk = _iota((1, hd), 1)
    cur = tpos // NSA_SEL_LEN
    forced = (blk == 0) | (blk == cur) | (blk == cur - 1)
    score = jnp.where(forced, NSA_FORCE_SCORE, jnp.where(blk * NSA_SEL_LEN <= tpos, importance, -1.0))
    score = jnp.where(blk < nsel, score, -2.0)
    cnt = jnp.zeros((tq, hd), jnp.int32)
    for j in range(nsel):
        col = score[:, j:j + 1]
        cnt = cnt + ((col > score) | ((col == score) & (blk > j))).astype(jnp.int32)
    bias = jnp.where(cnt < n_top, 0.0, SEL_MASK_BIAS).astype(BF16)
    q_aug = jnp.concatenate([q2, jnp.concatenate([bias] * N_HEADS, axis=0)], axis=1)

    def sel_step(c, carry):
        m, l, acc = carry
        sc = _dot_nt(q_aug, kaug[pl.ds(c * NSA_TK, NSA_TK), :])
        sc = jnp.where(c * NSA_TK + _iota((1, NSA_TK), 1) <= tpos4, sc, NEG_BIG)
        m_new = jnp.maximum(m, jnp.max(sc, axis=-1, keepdims=True))
        alpha = jnp.exp(m - m_new)
        p = jnp.exp(sc - m_new)
        v = vs_ref[0, pl.ds(c * NSA_TK, NSA_TK), :]
        return m_new, alpha * l + jnp.sum(p, axis=-1, keepdims=True), alpha * acc + _dot(p.astype(BF16), v)

    init = (jnp.full((rows, 1), NEG_BIG, F32), jnp.zeros((rows, 1), F32), jnp.zeros((rows, hd), F32))
    _, l_sel, acc_sel = lax.fori_loop(0, (t0 + tq + NSA_TK - 1) // NSA_TK, sel_step, init)
    o_sel = acc_sel / l_sel

    wlen = NSA_WINDOW + tq
    start = pl.multiple_of(jnp.maximum(t0 - NSA_WINDOW, 0), tq)
    kpos = start + _iota((1, wlen), 1)
    p_win = _softmax_rows(_dot_nt(q2, kw_ref[0, pl.ds(start, wlen), :]),
                          (kpos <= tpos4) & (kpos > tpos4 - NSA_WINDOW))
    o_win = _dot(p_win.astype(BF16), vw_ref[0, pl.ds(start, wlen), :])

    gates = _sigmoid(sm_ref[0])
    for h in range(N_HEADS):
        r = slice(h * tq, (h + 1) * tq)
        g0 = SM_GATE + 3 * h
        o = (gates[:, g0:g0 + 1] * o_cmp[r] + gates[:, g0 + 1:g0 + 2] * o_sel[r]
             + gates[:, g0 + 2:g0 + 3] * o_win[r])
        o_ref[0, h] = o.astype(o_ref.dtype)


def _nsa_attn(q, k_cmp, v_cmp, ks, vs, kw, vw, small):
    bsz, nh, s, hd = q.shape
    tq = NSA_TQ
    assert s % NSA_TK == 0 and s >= NSA_WINDOW + tq and s // NSA_SEL_LEN <= hd
    nc = k_cmp.shape[1]
    full = pl.BlockSpec((1, s, hd), lambda b, i: (b, 0, 0))
    cmp_spec = pl.BlockSpec((1, nc, hd), lambda b, i: (b, 0, 0))
    tile = pl.BlockSpec((1, nh, tq, hd), lambda b, i: (b, 0, i, 0))
    return pl.pallas_call(
        _nsa_body, grid=(bsz, s // tq),
        in_specs=[tile, cmp_spec, cmp_spec, full, full, full, full,
                  pl.BlockSpec((1, tq, LANES), lambda b, i: (b, i, 0))],
        out_specs=tile, out_shape=jax.ShapeDtypeStruct((bsz, nh, s, hd), BF16),
        scratch_shapes=[pltpu.VMEM((s, LANES), BF16)],
        compiler_params=_cp(("arbitrary", "arbitrary")), name="nsa_attn",
    )(q, k_cmp, v_cmp, ks, vs, kw, vw, small)


MERGE_T = 512


def _layer_norm(r, g, b):
    mu = jnp.mean(r, axis=-1, keepdims=True)
    xc = r - mu
    var = jnp.mean(xc * xc, axis=-1, keepdims=True)
    return xc * lax.rsqrt(var + LN_EPS) * g + b


def _merge_body(alpha, x_ref, sc_ref, sh_ref, gt_ref, og_ref, or_ref, on_ref, of_ref, wg_ref, bp_ref, wo_ref,
                lng_ref, lnb_ref, o_ref):
    x = x_ref[0]
    h = (x * sc_ref[0] + sh_ref[0]).astype(BF16)

    def head_major_proj(o_hm, br):
        acc = _dot(o_hm[0, 0], bp_ref[br, 0:HEAD_DIM, :])
        for hd in range(1, N_HEADS):
            acc = acc + _dot(o_hm[0, hd], bp_ref[br, hd * HEAD_DIM:(hd + 1) * HEAD_DIM, :])
        return acc

    projs = (lambda: _dot(og_ref[0], bp_ref[0]), lambda: _dot(or_ref[0], bp_ref[1]),
             lambda: head_major_proj(on_ref, 2), lambda: head_major_proj(of_ref, 3))
    merged = None
    for br, proj in enumerate(projs):
        term = _sigmoid(_dot(h, wg_ref[br])) * proj()
        merged = term if merged is None else merged + term
    y = _dot(merged.astype(BF16), wo_ref[...])
    o_ref[0] = _layer_norm(alpha * x + gt_ref[0] * y, lng_ref[...], lnb_ref[...])


def _merge(x, sc1p, shift, gate, o_gdn, o_ret, o_nsa, o_fox, w_gate, branch_proj, w_out, ln_g, ln_b, alpha):
    bsz, s, d = x.shape
    t = min(MERGE_T, s)
    tok = lambda w: pl.BlockSpec((1, t, w), lambda b, i: (b, i, 0))
    vec = pl.BlockSpec((1, 1, d), lambda b, i: (b, 0, 0))
    hm = pl.BlockSpec((1, N_HEADS, t, HEAD_DIM), lambda b, i: (b, 0, i, 0))
    const = lambda a: pl.BlockSpec(a.shape, lambda b, i: (0,) * a.ndim, pipeline_mode=pl.Buffered(1))
    lng, lnb = ln_g.reshape(1, d), ln_b.reshape(1, d)
    return pl.pallas_call(
        functools.partial(_merge_body, alpha), grid=(bsz, s // t),
        in_specs=[tok(d), vec, vec, vec, tok(MIX_W), tok(MIX_W), hm, hm,
                  const(w_gate), const(branch_proj), const(w_out), const(lng), const(lnb)],
        out_specs=tok(d), out_shape=jax.ShapeDtypeStruct((bsz, s, d), F32),
        compiler_params=_cp(("arbitrary", "arbitrary"), VMEM_LIMIT), name="merge",
    )(x, sc1p, shift, gate, o_gdn, o_ret, o_nsa, o_fox, w_gate, branch_proj, w_out, lng, lnb)


MOE_TM = 256
ROUTE_T = 512


def _router_body(tm, x_ref, sc_ref, sh_ref, rwt_ref, rb_ref, pos_ref, wt_ref, tile_ref, lg):
    i = pl.program_id(1)
    t = x_ref.shape[1]
    s = lg.shape[1]
    h = x_ref[0] * sc_ref[0] + sh_ref[0]
    a1, a2, _ = _split3(rwt_ref[...])
    b1, b2, _ = _split3(h)
    lg[:, pl.ds(pl.multiple_of(i * t, t), t)] = _dot_nt(a1, b1) + (_dot_nt(a1, b2) + _dot_nt(a2, b1))

    @pl.when(i == pl.num_programs(1) - 1)
    def _():
        scores = _sigmoid(lg[...])
        biased = scores + rb_ref[...]
        b = [biased[e:e + 1, :] for e in range(N_EXPERTS)]
        sc = [scores[e:e + 1, :] for e in range(N_EXPERTS)]
        gs = []
        for g in range(N_GROUPS):
            m = [b[EXPERTS_PER_GROUP * g + j] for j in range(EXPERTS_PER_GROUP)]
            best = m[0] + m[1]
            for u in range(EXPERTS_PER_GROUP):
                for v in range(u + 1, EXPERTS_PER_GROUP):
                    if (u, v) != (0, 1):
                        best = jnp.maximum(best, m[u] + m[v])
            gs.append(best)
        gsel, best = jnp.zeros((1, s), jnp.int32), gs[0]
        for g in range(1, N_GROUPS):
            take = gs[g] > best
            gsel = jnp.where(take, g, gsel)
            best = jnp.where(take, gs[g], best)
        first, second = [], []
        for e in range(N_EXPERTS):
            g = e // EXPERTS_PER_GROUP
            cnt = jnp.zeros((1, s), jnp.int32)
            for j in range(EXPERTS_PER_GROUP * g, EXPERTS_PER_GROUP * (g + 1)):
                if j != e:
                    ahead = (b[j] >= b[e]) if j < e else (b[j] > b[e])
                    cnt = cnt + ahead.astype(jnp.int32)
            first.append((gsel == g) & (cnt == 0))
            second.append((gsel == g) & (cnt == 1))
        s0 = sum(jnp.where(first[e], sc[e], 0.0) for e in range(N_EXPERTS))
        s1 = sum(jnp.where(second[e], sc[e], 0.0) for e in range(N_EXPERTS))
        den = s0 + s1
        wt_ref[0] = jnp.concatenate([s0 / den, s1 / den], axis=1)
        onehot = jnp.concatenate([(first[e] | second[e]).astype(F32) for e in range(N_EXPERTS)], axis=0)
        blk = min(512, s)
        before = (_iota((blk, blk), 0) < _iota((blk, blk), 1)).astype(BF16)
        carry, ranks = jnp.zeros((N_EXPERTS, 1), F32), []
        for j in range(s // blk):
            ob = onehot[:, j * blk:(j + 1) * blk]
            ranks.append(_dot(ob.astype(BF16), before) + carry)
            carry = carry + jnp.sum(ob, axis=1, keepdims=True)
        rank = jnp.concatenate(ranks, axis=1)
        ntile = jnp.floor((carry + (tm - 1.0)) * (1.0 / tm))
        lower = (_iota((N_EXPERTS, N_EXPERTS), 1) < _iota((N_EXPERTS, N_EXPERTS), 0)).astype(BF16)
        toff = _dot(lower, jnp.broadcast_to(ntile, (N_EXPERTS, LANES)).astype(BF16))[:, 0:1]
        slot = toff * float(tm) + rank
        pos0 = sum(jnp.where(first[e], slot[e:e + 1, :], 0.0) for e in range(N_EXPERTS))
        pos1 = sum(jnp.where(second[e], slot[e:e + 1, :], 0.0) for e in range(N_EXPERTS))
        pos_ref[0] = jnp.concatenate([pos0, pos1], axis=1).astype(jnp.int32)
        tend = toff + ntile
        tid = _iota((1, LANES), 1).astype(F32)
        texp = jnp.sum((tend <= tid).astype(F32), axis=0, keepdims=True)
        ntot = jnp.broadcast_to(jnp.sum(ntile, axis=0, keepdims=True), (1, LANES))
        tile_ref[0] = jnp.concatenate([jnp.minimum(texp, N_EXPERTS - 1.0), ntot], axis=0).astype(jnp.int32)


def _router(x, sc1p, shift, router_w, router_b, tm):
    bsz, s, d = x.shape
    t = min(ROUTE_T, s)
    vec = pl.BlockSpec((1, 1, d), lambda b, i: (b, 0, 0))
    return pl.pallas_call(
        functools.partial(_router_body, tm), grid=(bsz, s // t),
        in_specs=[pl.BlockSpec((1, t, d), lambda b, i: (b, i, 0)), vec, vec,
                  pl.BlockSpec((N_EXPERTS, d), lambda b, i: (0, 0)),
                  pl.BlockSpec((N_EXPERTS, 1), lambda b, i: (0, 0))],
        out_specs=[pl.BlockSpec((1, 1, 2 * s), lambda b, i: (b, 0, 0)),
                   pl.BlockSpec((1, 1, 2 * s), lambda b, i: (b, 0, 0)),
                   pl.BlockSpec((1, 2, LANES), lambda b, i: (b, 0, 0))],
        out_shape=[jax.ShapeDtypeStruct((bsz, 1, 2 * s), jnp.int32),
                   jax.ShapeDtypeStruct((bsz, 1, 2 * s), F32),
                   jax.ShapeDtypeStruct((bsz, 2, LANES), jnp.int32)],
        scratch_shapes=[pltpu.VMEM((N_EXPERTS, s), F32)],
        compiler_params=_cp(("arbitrary", "arbitrary")), name="router",
    )(x, sc1p, shift, router_w.T, router_b.reshape(N_EXPERTS, 1))


LN_ROWS = 256


def _moe_body(alpha, tm, texp_ref, ntot_ref, x_ref, sc_ref, sh_ref, gt_ref, pos_ref, wt_ref,
              w1_ref, w3_ref, w2_ref, lng_ref, lnb_ref, o_ref, src, wgt, xs, ys):
    b, i = pl.program_id(0), pl.program_id(1)
    s = x_ref.shape[1]

    @pl.when(i == 0)
    def _():
        def clear(p, c):
            src[p] = 0
            wgt[p] = 0.0
            return c
        lax.fori_loop(0, src.shape[0], clear, 0, unroll=8)

        def fill(t, c):
            for k in range(2):
                p = pos_ref[0, 0, k * s + t]
                src[p] = t
                wgt[p] = wt_ref[0, 0, k * s + t]
            return c
        lax.fori_loop(0, s, fill, 0, unroll=4)
        o_ref[0] = jnp.zeros(o_ref.shape[1:], F32)

    @pl.when(i < ntot_ref[b])
    def _():
        base = i * tm
        sc, sh = sc_ref[0], sh_ref[0]

        def gather(r, c):
            t = src[base + r]
            xs[pl.ds(r, 1), :] = x_ref[0, pl.ds(t, 1), :] * sc + sh
            return c
        lax.fori_loop(0, tm, gather, 0, unroll=8)
        xb = xs[...].astype(BF16)
        hid = _silu(_dot(xb, w1_ref[0])) * _dot(xb, w3_ref[0])
        ys[...] = _dot(hid.astype(BF16), w2_ref[0])

        def scatter(r, c):
            t = src[base + r]
            o_ref[0, pl.ds(t, 1), :] = o_ref[0, pl.ds(t, 1), :] + wgt[base + r] * ys[pl.ds(r, 1), :]
            return c
        lax.fori_loop(0, tm, scatter, 0, unroll=8)

    @pl.when(i == pl.num_programs(1) - 1)
    def _():
        def ln(j, c):
            rows = pl.ds(pl.multiple_of(j * LN_ROWS, LN_ROWS), LN_ROWS)
            r = alpha * x_ref[0, rows, :] + gt_ref[0] * o_ref[0, rows, :]
            o_ref[0, rows, :] = _layer_norm(r, lng_ref[...], lnb_ref[...])
            return c
        lax.fori_loop(0, s // LN_ROWS, ln, 0)


def _moe(x, sc1p, shift, gate, pos, wts, tiles, w1, w3, w2, ln_g, ln_b, alpha, tm):
    bsz, s, d = x.shape
    f = w1.shape[2]
    nt = 2 * s // tm + N_EXPERTS
    assert nt <= LANES
    texp = tiles[:, 0, :].reshape(-1)
    ntot = tiles[:, 1, 0]
    vec = pl.BlockSpec((1, 1, d), lambda b, i, te, nn: (b, 0, 0))
    seq = pl.BlockSpec((1, s, d), lambda b, i, te, nn: (b, 0, 0), pipeline_mode=pl.Buffered(1))
    smem = pl.BlockSpec((1, 1, 2 * s), lambda b, i, te, nn: (b, 0, 0), memory_space=pltpu.SMEM)
    row = pl.BlockSpec((1, d), lambda b, i, te, nn: (0, 0))
    grid_spec = pltpu.PrefetchScalarGridSpec(
        num_scalar_prefetch=2, grid=(bsz, nt),
        in_specs=[seq, vec, vec, vec, smem, smem,
                  pl.BlockSpec((1, d, f), lambda b, i, te, nn: (te[b * LANES + i], 0, 0)),
                  pl.BlockSpec((1, d, f), lambda b, i, te, nn: (te[b * LANES + i], 0, 0)),
                  pl.BlockSpec((1, f, d), lambda b, i, te, nn: (te[b * LANES + i], 0, 0)),
                  row, row],
        out_specs=seq,
        scratch_shapes=[pltpu.SMEM((nt * tm,), jnp.int32), pltpu.SMEM((nt * tm,), F32),
                        pltpu.VMEM((tm, d), F32), pltpu.VMEM((tm, d), F32)])
    return pl.pallas_call(
        functools.partial(_moe_body, alpha, tm), grid_spec=grid_spec,
        out_shape=jax.ShapeDtypeStruct((bsz, s, d), F32),
        compiler_params=_cp(("arbitrary", "arbitrary"), VMEM_LIMIT), name="moe_experts",
    )(texp, ntot, x, sc1p, shift, gate, pos, wts, w1, w3, w2, ln_g.reshape(1, d), ln_b.reshape(1, d))


def kernel(x, c, positions, ada_w, ada_b, w_in, gdn_conv_w, gdn_a_log, gdn_dt_bias, gdn_norm_w, ret_gn_w,
           nsa_cmp_pe, nsa_ck_w1, nsa_ck_w2, nsa_cv_w1, nsa_cv_w2, fox_f_bias, branch_proj, w_gate, w_out,
           ln_g, ln_b, router_w, router_b, exp_w1, exp_w3, exp_w2):
    depth, d = w_in.shape[0], x.shape[-1]
    alpha = (2.0 * depth) ** 0.25
    mod = _ada_mod(c, ada_w, ada_b)
    for l in range(depth):
        shift, sc1p, gate = mod[l, 0, :, :, :d], mod[l, 0, :, :, d:2 * d], mod[l, 0, :, :, 2 * d:]
        wcat, wst = _cat_in_weights(w_in[l])
        gqkv, gz, r, nq, kc, vc, ks, vs, kw, vw, fq, fk, fv, sm, smt = _in_proj(x, sc1p, shift, wcat, wst)
        o_gdn = _gdn(gqkv, sm, smt, gz, gdn_conv_w[l], gdn_a_log[l], gdn_dt_bias[l], gdn_norm_w[l])
        o_ret = _retention(r, positions, ret_gn_w[l])
        k_cmp, v_cmp = _nsa_compress(kc, vc, nsa_cmp_pe[l], nsa_ck_w1[l], nsa_ck_w2[l], nsa_cv_w1[l], nsa_cv_w2[l])
        o_nsa = _nsa_attn(nq, k_cmp, v_cmp, ks, vs, kw, vw, sm)
        ccol, crow = _fox_cum(sm, smt, fox_f_bias[l])
        o_fox = _fox_attn(fq, fk, fv, ccol, crow)
        x = _merge(x, sc1p, shift, gate, o_gdn, o_ret, o_nsa, o_fox, w_gate[l].astype(BF16),
                   branch_proj[l].astype(BF16), w_out[l].astype(BF16), ln_g[l, 0], ln_b[l, 0], alpha)
        shift, sc1p, gate = mod[l, 1, :, :, :d], mod[l, 1, :, :, d:2 * d], mod[l, 1, :, :, 2 * d:]
        pos, wts, tiles = _router(x, sc1p, shift, router_w, router_b, MOE_TM)
        x = _moe(x, sc1p, shift, gate, pos, wts, tiles, exp_w1[l].astype(BF16), exp_w3[l].astype(BF16),
                 exp_w2[l].astype(BF16), ln_g[l, 1], ln_b[l, 1], alpha, MOE_TM)
    return x
```

```python
import functools
import math

import jax
import jax.numpy as jnp
import numpy as np
from jax import lax
from jax.experimental import pallas as pl
from jax.experimental.pallas import tpu as pltpu

F32 = jnp.float32
BF16 = jnp.bfloat16

N_HEADS = 4
HEAD_DIM = 64
MIX_W = N_HEADS * HEAD_DIM
GDN_CHUNK = 64
CONV_K = 4
RET_CHUNK = 128
ROPE_BASE = 10000.0
NSA_CMP_LEN = 32
NSA_CMP_STRIDE = 16
NSA_SEL_LEN = 64
NSA_TOP_N = 16
NSA_WINDOW = 512
NSA_FORCE_SCORE = 1.0e4
N_EXPERTS = 16
N_GROUPS = 4
EXPERTS_PER_GROUP = N_EXPERTS // N_GROUPS
LN_EPS = 1e-5
LANES = 128
NEG_BIG = -1.0e30
SEL_MASK_BIAS = -30000.0

IN_WIDTHS = (
    MIX_W, MIX_W, MIX_W, N_HEADS, N_HEADS, MIX_W,
    MIX_W, MIX_W, MIX_W, MIX_W,
    MIX_W, HEAD_DIM, HEAD_DIM, HEAD_DIM, HEAD_DIM, HEAD_DIM, HEAD_DIM, 3 * N_HEADS,
    MIX_W, MIX_W, MIX_W, N_HEADS,
)
_IN_OFF = np.concatenate([[0], np.cumsum(IN_WIDTHS)]).astype(int)
(_GQ, _GK, _GV, _GA, _GB, _GZ, _RQ, _RK, _RV, _RG, _NQ, _NKC, _NVC, _NKS, _NVS, _NKW, _NVW, _NGATE,
 _FQ, _FK, _FV, _FF) = range(22)
SM_A, SM_B, SM_GATE, SM_F = 0, 4, 8, 20

VMEM_LIMIT = 56 * 1024 * 1024


def _cp(sem, vmem=None):
    return pltpu.CompilerParams(dimension_semantics=sem, vmem_limit_bytes=vmem)


def _sigmoid(x):
    return 1.0 / (1.0 + jnp.exp(-x))


def _silu(x):
    return x * _sigmoid(x)


def _softplus(x):
    return jnp.maximum(x, 0.0) + jnp.log1p(jnp.exp(-jnp.abs(x)))


def _dot(a, b):
    return jnp.dot(a, b, preferred_element_type=F32)


def _dot_nt(a, b):
    return lax.dot_general(a, b, (((1,), (1,)), ((), ())), preferred_element_type=F32)


def _split3(a):
    a1 = a.astype(BF16)
    r = a - a1.astype(F32)
    a2 = r.astype(BF16)
    a3 = (r - a2.astype(F32)).astype(BF16)
    return a1, a2, a3


def _dot_x(a, m):
    a1, a2, a3 = _split3(a)
    return _dot(a1, m) + _dot(a2, m) + _dot(a3, m)


def _xdot(m, a):
    a1, a2, a3 = _split3(a)
    return _dot(m, a1) + _dot(m, a2) + _dot(m, a3)


def _dot_hp(a, b):
    a1, a2, _ = _split3(a)
    b1, b2, _ = _split3(b)
    return _dot(a1, b1) + (_dot(a1, b2) + _dot(a2, b1))


def _iota(shape, dim):
    return lax.broadcasted_iota(jnp.int32, shape, dim)


def _head_ones():
    return (_iota((MIX_W, MIX_W), 0) // HEAD_DIM == _iota((MIX_W, MIX_W), 1) // HEAD_DIM)


def _head_sum(x):
    return _dot_x(x, _head_ones().astype(BF16))


def _ada_body(c_ref, w_ref, b_ref, o_ref):
    mod = _dot_hp(_silu(c_ref[...]), w_ref[0]) + b_ref[0]
    o_ref[0] = jnp.where(pl.program_id(1) == 1, 1.0 + mod, mod)


def _ada_mod(c, ada_w, ada_b):
    depth = ada_w.shape[0]
    bsz, d = c.shape
    n = depth * 2
    w = ada_w.reshape(n, d, 3 * d)
    b = ada_b.reshape(n, 1, 3 * d)
    out = pl.pallas_call(
        _ada_body, grid=(n, 3),
        in_specs=[pl.BlockSpec((bsz, d), lambda i, j: (0, 0)),
                  pl.BlockSpec((1, d, d), lambda i, j: (i, 0, j)),
                  pl.BlockSpec((1, 1, d), lambda i, j: (i, 0, j))],
        out_specs=pl.BlockSpec((1, bsz, d), lambda i, j: (i, 0, j)),
        out_shape=jax.ShapeDtypeStruct((n, bsz, 3 * d), F32),
        compiler_params=_cp(("arbitrary", "arbitrary")), name="ada_mod",
    )(c, w, b)
    return out.reshape(depth, 2, bsz, 1, 3 * d)


IN_TS = 512
_W_GROUPS = (768, 256, 1024, 256, 384, 768, 128)
_W_OFF = np.concatenate([[0], np.cumsum(_W_GROUPS)]).astype(int)
IN_CAT_W = int(_W_OFF[-1])


def _cat_in_weights(w_in):
    def col(i):
        return w_in[:, _IN_OFF[i]:_IN_OFF[i + 1]]
    small = jnp.concatenate([col(_GA), col(_GB), col(_NGATE), col(_FF)], axis=1)
    small = jnp.pad(small, ((0, 0), (0, LANES - small.shape[1])))
    scale = HEAD_DIM ** -0.5
    cat = jnp.concatenate([
        col(_GQ), col(_GK), col(_GV), col(_GZ),
        col(_RQ), col(_RK) * scale, col(_RV), col(_RG),
        col(_NQ) * scale,
        col(_NKC), col(_NVC), col(_NKS), col(_NVS), col(_NKW), col(_NVW),
        col(_FQ) * scale, col(_FK), col(_FV),
        small], axis=1)
    return cat.astype(BF16), small.T.astype(BF16)


def _inproj_body(x_ref, sc_ref, sh_ref, w_ref, wst_ref,
                 gqkv_ref, gz_ref, r_ref, nq_ref, kc_ref, vc_ref, ks_ref, vs_ref, kw_ref, vw_ref,
                 fq_ref, fk_ref, fv_ref, sm_ref, smt_ref):
    h = (x_ref[0] * sc_ref[0] + sh_ref[0]).astype(BF16)

    def proj(g):
        return _dot(h, w_ref[:, _W_OFF[g]:_W_OFF[g + 1]])

    gqkv_ref[0] = proj(0)
    gz_ref[0] = proj(1)
    r_ref[0] = proj(2)
    nq = proj(3)
    for hd in range(N_HEADS):
        nq_ref[0, hd] = nq[:, hd * HEAD_DIM:(hd + 1) * HEAD_DIM].astype(BF16)
    nkv = proj(4)
    for i, ref in enumerate((kc_ref, vc_ref, ks_ref, vs_ref, kw_ref, vw_ref)):
        ref[0] = nkv[:, i * HEAD_DIM:(i + 1) * HEAD_DIM].astype(ref.dtype)
    f = proj(5)
    for j, ref in enumerate((fq_ref, fk_ref, fv_ref)):
        for hd in range(N_HEADS):
            lo = j * MIX_W + hd * HEAD_DIM
            ref[0, hd] = f[:, lo:lo + HEAD_DIM].astype(BF16)
    sm_ref[0] = proj(6)
    smt_ref[0] = _dot_nt(wst_ref[...], h)


def _in_proj(x, sc1p, shift, wcat, wst):
    bsz, s, d = x.shape
    ts = min(IN_TS, s)
    tok = lambda w, dt: jax.ShapeDtypeStruct((bsz, s, w), dt)
    hm = jax.ShapeDtypeStruct((bsz, N_HEADS, s, HEAD_DIM), BF16)
    out_shape = [tok(768, F32), tok(256, F32), tok(1024, F32), hm,
                 tok(64, F32), tok(64, F32), tok(64, BF16), tok(64, BF16), tok(64, BF16), tok(64, BF16),
                 hm, hm, hm, tok(LANES, F32), jax.ShapeDtypeStruct((bsz, LANES, s), F32)]
    tspec = lambda w: pl.BlockSpec((1, ts, w), lambda b, i: (b, i, 0))
    hspec = pl.BlockSpec((1, N_HEADS, ts, HEAD_DIM), lambda b, i: (b, 0, i, 0))
    out_specs = [tspec(768), tspec(256), tspec(1024), hspec,
                 tspec(64), tspec(64), tspec(64), tspec(64), tspec(64), tspec(64),
                 hspec, hspec, hspec, tspec(LANES), pl.BlockSpec((1, LANES, ts), lambda b, i: (b, 0, i))]
    vec = pl.BlockSpec((1, 1, d), lambda b, i: (b, 0, 0))
    return pl.pallas_call(
        _inproj_body, grid=(bsz, s // ts),
        in_specs=[tspec(d), vec, vec,
                  pl.BlockSpec((d, IN_CAT_W), lambda b, i: (0, 0)),
                  pl.BlockSpec((LANES, d), lambda b, i: (0, 0))],
        out_specs=out_specs, out_shape=out_shape,
        compiler_params=_cp(("arbitrary", "arbitrary"), VMEM_LIMIT), name="in_proj",
    )(x, sc1p, shift, wcat, wst)


CUM_T = 512


def _foxcum_body(sm_ref, k_ref, brow_ref, kaug_ref, carry_r):
    @pl.when(pl.program_id(1) == 0)
    def _():
        carry_r[...] = jnp.zeros_like(carry_r)

    t = sm_ref.shape[1]

    def log_sigmoid(v):
        return jnp.minimum(v, 0.0) - jnp.log1p(jnp.exp(-jnp.abs(v)))

    lower = (_iota((t, t), 0) >= _iota((t, t), 1)).astype(BF16)
    cc = _xdot(lower, log_sigmoid(sm_ref[0] + brow_ref[...])) + carry_r[...]
    carry_r[...] = cc[t - 1:t, :]
    parts = _split3(-cc)
    place = (_iota((HEAD_DIM, LANES), 0) == _iota((HEAD_DIM, LANES), 1)).astype(BF16)
    src, dst = _iota((LANES, LANES), 0), _iota((LANES, LANES), 1)
    for hd in range(N_HEADS):
        aug = _dot(k_ref[0, hd], place)
        for j, part in enumerate(parts):
            aug = aug + _dot(part, ((src == SM_F + hd) & (dst == HEAD_DIM + j)).astype(BF16))
        kaug_ref[0, hd] = aug.astype(BF16)


def _fox_cum(small, k, f_bias):
    bsz, s, _ = small.shape
    t = min(CUM_T, s)
    brow = jnp.zeros((1, LANES), F32).at[0, SM_F:SM_F + N_HEADS].set(f_bias)
    return pl.pallas_call(
        _foxcum_body, grid=(bsz, s // t),
        in_specs=[pl.BlockSpec((1, t, LANES), lambda b, i: (b, i, 0)),
                  pl.BlockSpec((1, N_HEADS, t, HEAD_DIM), lambda b, i: (b, 0, i, 0)),
                  pl.BlockSpec((1, LANES), lambda b, i: (0, 0))],
        out_specs=pl.BlockSpec((1, N_HEADS, t, LANES), lambda b, i: (b, 0, i, 0)),
        out_shape=jax.ShapeDtypeStruct((bsz, N_HEADS, s, LANES), BF16),
        scratch_shapes=[pltpu.VMEM((1, LANES), F32)],
        compiler_params=_cp(("arbitrary", "arbitrary")), name="fox_cum",
    )(small, k, brow)


FOX_T = 512


FOX_UNROLL = 4


def _fox_body(q_ref, k_ref, v_ref, o_ref):
    i = pl.program_id(1)
    t = q_ref.shape[2]
    place = (_iota((HEAD_DIM, LANES), 0) == _iota((HEAD_DIM, LANES), 1)).astype(BF16)
    lane = _iota((1, LANES), 1)
    ones = ((lane >= HEAD_DIM) & (lane < HEAD_DIM + 3)).astype(F32)
    causal = _iota((t, t), 1) <= _iota((t, t), 0)
    outs = []
    for hd in range(N_HEADS):
        q_aug = (_dot(q_ref[0, hd], place) + ones).astype(BF16)

        def update(j, carry, masked=False):
            m, l, acc = carry
            rows = pl.ds(pl.multiple_of(j * t, t), t)
            s = _dot_nt(q_aug, k_ref[0, hd, rows, :])
            if masked:
                s = jnp.where(causal, s, NEG_BIG)
            m_new = jnp.maximum(m, jnp.max(s, axis=-1, keepdims=True))
            alpha = jnp.exp(m - m_new)
            p = jnp.exp(s - m_new)
            return (m_new, alpha * l + jnp.sum(p, axis=-1, keepdims=True),
                    alpha * acc + _dot(p.astype(BF16), v_ref[0, hd, rows, :]))

        def group(g, carry):
            for u in range(FOX_UNROLL):
                carry = update(g * FOX_UNROLL + u, carry)
            return carry

        carry = (jnp.full((t, 1), NEG_BIG, F32), jnp.zeros((t, 1), F32), jnp.zeros((t, HEAD_DIM), F32))
        carry = lax.fori_loop(0, i // FOX_UNROLL, group, carry)
        carry = lax.fori_loop((i // FOX_UNROLL) * FOX_UNROLL, i, update, carry)
        m, l, acc = update(i, carry, masked=True)
        outs.append(acc / l)
    o_ref[0] = jnp.concatenate(outs, axis=1).astype(o_ref.dtype)


def _fox_attn(q, kaug, v):
    bsz, nh, s, hd = q.shape
    t = min(FOX_T, s)
    return pl.pallas_call(
        _fox_body, grid=(bsz, s // t),
        in_specs=[pl.BlockSpec((1, nh, t, hd), lambda b, i: (b, 0, i, 0)),
                  pl.BlockSpec((1, nh, s, LANES), lambda b, i: (b, 0, 0, 0)),
                  pl.BlockSpec((1, nh, s, hd), lambda b, i: (b, 0, 0, 0))],
        out_specs=pl.BlockSpec((1, t, MIX_W), lambda b, i: (b, i, 0)),
        out_shape=jax.ShapeDtypeStruct((bsz, s, MIX_W), BF16),
        compiler_params=_cp(("arbitrary", "arbitrary")), name="fox_attn",
    )(q, kaug, v)


GDN_T = 256
GDN_NB = 2


def _gdn_body(qkv_ref, sm_ref, smt_ref, z_ref, cw_ref, alr_ref, dtr_ref, alc_ref, dtc_ref, nw_ref,
              o_ref, xbuf, state):
    tiles = [_gdn_tile(qkv_ref.at[bb], sm_ref.at[bb], smt_ref.at[bb], z_ref.at[bb], cw_ref, alr_ref, dtr_ref,
                       alc_ref, dtc_ref, nw_ref, o_ref.at[bb], xbuf.at[bb], state.at[bb])
             for bb in range(qkv_ref.shape[0])]
    while tiles:
        tiles = [g for g in tiles if next(g, True) is None]


def _gdn_tile(qkv_ref, sm_ref, smt_ref, z_ref, cw_ref, alr_ref, dtr_ref, alc_ref, dtc_ref, nw_ref,
              o_ref, xbuf, state):
    t, c = GDN_T, GDN_CHUNK

    @pl.when(pl.program_id(1) == 0)
    def _():
        xbuf[0:8, :] = jnp.zeros((8, 3 * MIX_W), F32)
        state[...] = jnp.zeros_like(state)

    xbuf[8:8 + t, :] = qkv_ref[...]
    conv = cw_ref[0:1, :] * xbuf[pl.ds(8 - CONV_K + 1, t), :]
    for kk in range(1, CONV_K):
        conv = conv + cw_ref[kk:kk + 1, :] * xbuf[pl.ds(8 - CONV_K + 1 + kk, t), :]
    xbuf[0:8, :] = xbuf[t:t + 8, :]
    act = _silu(conv)
    q, k, v = act[:, :MIX_W], act[:, MIX_W:2 * MIX_W], act[:, 2 * MIX_W:]
    q = q * lax.rsqrt(_head_sum(q * q) + 1e-6) * HEAD_DIM ** -0.5
    k = k * lax.rsqrt(_head_sum(k * k) + 1e-6)

    ri, ci = _iota((t, t), 0), _iota((t, t), 1)
    same = (ri // c) == (ci // c)
    incl = same & (ci <= ri)
    strict = same & (ci < ri)
    sm, smt = sm_ref[...], smt_ref[...]
    la_col = -jnp.exp(alr_ref[...]) * _softplus(sm + dtr_ref[...])
    la_row = -jnp.exp(alc_ref[...]) * _softplus(smt + dtc_ref[...])
    beta_col = _sigmoid(sm)
    bc_col = _xdot(incl.astype(BF16), la_col)
    last_col = _xdot(same.astype(BF16), la_col)
    bc_row = _dot_x(la_row, (same & (ri <= ci)).astype(BF16))
    er, ec = _iota((LANES, MIX_W), 0), _iota((LANES, MIX_W), 1) // HEAD_DIM
    exp_a = (er == ec + SM_A).astype(BF16)
    exp_b = (er == ec + SM_B).astype(BF16)
    bc_x, last_x, beta_x = _dot_x(bc_col, exp_a), _dot_x(last_col, exp_a), _dot_x(beta_col, exp_b)
    eb = jnp.exp(bc_x)
    q_dec = q * eb
    k_dec = k * jnp.exp(last_x - bc_x)
    rhs_v = beta_x * v
    rhs_k = beta_x * eb * k
    rhs = jnp.concatenate([rhs_v, rhs_k], axis=1).astype(BF16)

    kb = k.astype(BF16)
    lane_head = _iota((1, MIX_W), 1) // HEAD_DIM
    heads = range(N_HEADS)
    mh = [lane_head == hd for hd in heads]
    gram = [_dot_nt(jnp.where(mh[hd], k, 0.0).astype(BF16), kb) for hd in heads]
    qk = [_dot_nt(jnp.where(mh[hd], q, 0.0).astype(BF16), kb) for hd in heads]
    yield
    dec = [jnp.exp(jnp.minimum(bc_col[:, SM_A + hd:SM_A + hd + 1] - bc_row[SM_A + hd:SM_A + hd + 1, :], 0.0))
           for hd in heads]
    attn = [jnp.where(incl, qk[hd] * dec[hd], 0.0) for hd in heads]
    p = [jnp.where(strict, -(beta_col[:, SM_B + hd:SM_B + hd + 1] * gram[hd] * dec[hd]), 0.0) for hd in heads]
    tr = list(p)
    for _ in range(5):
        pb = [p[hd].astype(BF16) for hd in heads]
        p = [_dot(pb[hd], pb[hd]) for hd in heads]
        yield
        tr = [tr[hd] + p[hd] + _dot(tr[hd].astype(BF16), p[hd].astype(BF16)) for hd in heads]
    sol = [_dot(tr[hd].astype(BF16), rhs) for hd in heads]
    yield
    u0, w = rhs_v, rhs_k
    for hd in heads:
        u0 = u0 + jnp.where(mh[hd], sol[hd][:, :MIX_W], 0.0)
        w = w + jnp.where(mh[hd], sol[hd][:, MIX_W:], 0.0)

    head_blk = _head_ones()
    k_dec_t = k_dec.T.astype(BF16)
    outs = []
    for n in range(t // c):
        rows = slice(n * c, (n + 1) * c)
        s_old = state[...]
        sb = s_old.astype(BF16)
        u_n = u0[rows] - _dot(w[rows].astype(BF16), sb)
        yield
        parts = ([jnp.zeros((n * c, MIX_W), F32)] if n else []) + [u_n]
        if t - (n + 1) * c:
            parts.append(jnp.zeros((t - (n + 1) * c, MIX_W), F32))
        u_full = jnp.concatenate(parts, axis=0).astype(BF16)
        oa = _dot(jnp.concatenate([a[rows] for a in attn], axis=0).astype(BF16), u_full)
        o_n = _dot(q_dec[rows].astype(BF16), sb)
        for hd in range(N_HEADS):
            o_n = o_n + jnp.where(lane_head == hd, oa[hd * c:(hd + 1) * c], 0.0)
        outs.append(o_n)
        g_row = jnp.exp(last_x[n * c:n * c + 1, :])
        state[...] = g_row * s_old + jnp.where(head_blk, _dot(k_dec_t, u_full), 0.0)
        yield
    o = jnp.concatenate(outs, axis=0)
    o = o * lax.rsqrt(_head_sum(o * o) * (1.0 / HEAD_DIM) + 1e-6) * nw_ref[...]
    o_ref[...] = (o * _silu(z_ref[...])).astype(o_ref.dtype)


def _lane_vec(vals, off):
    return jnp.zeros((1, LANES), F32).at[0, off:off + vals.shape[0]].set(vals)


def _gdn(qkv, small, small_t, z, conv_w, a_log, dt_bias, norm_w):
    bsz, s, _ = qkv.shape
    t = GDN_T
    alr, dtr = _lane_vec(a_log, SM_A), _lane_vec(dt_bias, SM_A)
    nw = jnp.tile(norm_w, N_HEADS).reshape(1, MIX_W)
    const = lambda shape: pl.BlockSpec(shape, lambda b, i: (0,) * len(shape))
    nb = GDN_NB if bsz % GDN_NB == 0 else 1
    return pl.pallas_call(
        _gdn_body, grid=(bsz // nb, s // t),
        in_specs=[pl.BlockSpec((nb, t, 3 * MIX_W), lambda b, i: (b, i, 0)),
                  pl.BlockSpec((nb, t, LANES), lambda b, i: (b, i, 0)),
                  pl.BlockSpec((nb, LANES, t), lambda b, i: (b, 0, i)),
                  pl.BlockSpec((nb, t, MIX_W), lambda b, i: (b, i, 0)),
                  const((CONV_K, 3 * MIX_W)), const((1, LANES)), const((1, LANES)),
                  const((LANES, 1)), const((LANES, 1)), const((1, MIX_W))],
        out_specs=pl.BlockSpec((nb, t, MIX_W), lambda b, i: (b, i, 0)),
        out_shape=jax.ShapeDtypeStruct((bsz, s, MIX_W), BF16),
        scratch_shapes=[pltpu.VMEM((nb, t + 8, 3 * MIX_W), F32), pltpu.VMEM((nb, MIX_W, MIX_W), F32)],
        compiler_params=_cp(("arbitrary", "arbitrary")), name="gdn",
    )(qkv, small, small_t, z, conv_w, alr, dtr, alr.reshape(LANES, 1), dtr.reshape(LANES, 1), nw)


RET_T = 512
_RET_LOG_GAMMA = tuple(math.log1p(-(2.0 ** (-5.0 - h))) for h in range(N_HEADS))


def _per_head(lane_head, vals):
    out = jnp.full(lane_head.shape, vals[0], F32)
    for hd in range(1, N_HEADS):
        out = jnp.where(lane_head == hd, vals[hd], out)
    return out


def _ret_body(r_ref, pos_ref, gnw_ref, o_ref, state):
    t, c = r_ref.shape[1], RET_CHUNK

    @pl.when(pl.program_id(1) == 0)
    def _():
        state[...] = jnp.zeros_like(state)

    x = r_ref[0]
    lane = _iota((1, MIX_W), 1)
    lane_head = lane // HEAD_DIM
    half = HEAD_DIM // 2
    inv_freq = jnp.exp((lane % half).astype(F32) * (-math.log(ROPE_BASE) / half))
    ang = pos_ref[0].astype(F32) * inv_freq
    cos, sin = jnp.cos(ang), jnp.sin(ang)
    first = (lane % HEAD_DIM) < half

    def rope(a):
        rot = jnp.where(first, -pltpu.roll(a, MIX_W - half, 1), pltpu.roll(a, half, 1))
        return a * cos + rot * sin

    q, k = rope(x[:, :MIX_W]), rope(x[:, MIX_W:2 * MIX_W])
    v, g = x[:, 2 * MIX_W:3 * MIX_W], x[:, 3 * MIX_W:]

    lg = _per_head(lane_head, _RET_LOG_GAMMA)
    cidx = _iota((c, 1), 0).astype(F32)
    xi = jnp.exp((cidx + 1.0) * lg)
    zeta = jnp.exp((c - 1.0 - cidx) * lg)
    chunk_decay = jnp.exp(float(c) * lg)
    rel = (_iota((c, c), 0) - _iota((c, c), 1)).astype(F32)
    dec = jnp.concatenate([jnp.where(rel >= 0, jnp.exp(jnp.maximum(rel, 0.0) * _RET_LOG_GAMMA[hd]), 0.0)
                           for hd in range(N_HEADS)], axis=0)
    head_blk = _head_ones()
    outs = []
    for n in range(t // c):
        rows = slice(n * c, (n + 1) * c)
        qn, kn, vn = q[rows], k[rows], v[rows].astype(BF16)
        qs = jnp.concatenate([jnp.where(lane_head == hd, qn, 0.0) for hd in range(N_HEADS)], axis=0).astype(BF16)
        sc = _dot_nt(qs, kn.astype(BF16)) * dec
        res = _dot(sc.astype(BF16), vn)
        s_old = state[...]
        o_n = _dot(qn.astype(BF16), s_old.astype(BF16)) * xi
        for hd in range(N_HEADS):
            o_n = o_n + jnp.where(lane_head == hd, res[hd * c:(hd + 1) * c], 0.0)
        outs.append(o_n)
        kv = _dot((kn * zeta).T.astype(BF16), vn)
        state[...] = chunk_decay * s_old + jnp.where(head_blk, kv, 0.0)
    o = jnp.concatenate(outs, axis=0)
    mu = _head_sum(o) * (1.0 / HEAD_DIM)
    xc = o - mu
    var = _head_sum(xc * xc) * (1.0 / HEAD_DIM)
    y = xc * lax.rsqrt(var + LN_EPS) * gnw_ref[...]
    o_ref[0] = (_silu(g) * y).astype(o_ref.dtype)


def _retention(r, positions, gn_w):
    bsz, s, _ = r.shape
    t = min(RET_T, s)
    return pl.pallas_call(
        _ret_body, grid=(bsz, s // t),
        in_specs=[pl.BlockSpec((1, t, 4 * MIX_W), lambda b, i: (b, i, 0)),
                  pl.BlockSpec((1, t, 1), lambda b, i: (b, i, 0)),
                  pl.BlockSpec((1, MIX_W), lambda b, i: (0, 0))],
        out_specs=pl.BlockSpec((1, t, MIX_W), lambda b, i: (b, i, 0)),
        out_shape=jax.ShapeDtypeStruct((bsz, s, MIX_W), BF16),
        scratch_shapes=[pltpu.VMEM((MIX_W, MIX_W), F32)],
        compiler_params=_cp(("arbitrary", "arbitrary")), name="retention",
    )(r, positions.reshape(bsz, s, 1), gn_w.reshape(1, MIX_W))


def _nsacmp_body(kc_ref, vc_ref, pe_ref, kw1_ref, kw2_ref, vw1_ref, vw2_ref, ko_ref, vo_ref):
    half = NSA_CMP_STRIDE * HEAD_DIM
    nc = kc_ref.shape[1]
    for x_ref, w1_ref, w2_ref, o_ref in ((kc_ref, kw1_ref, kw2_ref, ko_ref), (vc_ref, vw1_ref, vw2_ref, vo_ref)):
        x = x_ref[0]
        top = _dot_hp(x, w1_ref[0:half, :])
        bot = _dot_hp(x, w1_ref[half:2 * half, :])
        bias = _dot_hp(pe_ref[...], w1_ref[...])[0:1, :]
        hidden = top + pltpu.roll(bot, nc - 1, 0) + bias
        o_ref[0] = _dot_hp(_silu(hidden), w2_ref[...])


def _nsa_compress(kc, vc, pe, ck_w1, ck_w2, cv_w1, cv_w2):
    bsz, s, hd = kc.shape
    nc = s // NSA_CMP_STRIDE
    wide = NSA_CMP_STRIDE * hd
    pe8 = jnp.broadcast_to(pe.reshape(1, NSA_CMP_LEN * hd), (8, NSA_CMP_LEN * hd))
    const = lambda a: pl.BlockSpec(a.shape, lambda b: (0,) * a.ndim)
    xspec = pl.BlockSpec((1, nc, wide), lambda b: (b, 0, 0))
    ospec = pl.BlockSpec((1, nc, hd), lambda b: (b, 0, 0))
    return pl.pallas_call(
        _nsacmp_body, grid=(bsz,),
        in_specs=[xspec, xspec, const(pe8), const(ck_w1), const(ck_w2), const(cv_w1), const(cv_w2)],
        out_specs=[ospec, ospec],
        out_shape=[jax.ShapeDtypeStruct((bsz, nc, hd), F32)] * 2,
        compiler_params=_cp(("arbitrary",)), name="nsa_compress",
    )(kc.reshape(bsz, nc, wide), vc.reshape(bsz, nc, wide), pe8, ck_w1, ck_w2, cv_w1, cv_w2)


NSA_TQ = 128
NSA_TK = 256
NSA_UNROLL = 4


def _softmax_rows(s, valid):
    m = jnp.max(jnp.where(valid, s, NEG_BIG), axis=-1, keepdims=True)
    m = jnp.where(m > 0.5 * NEG_BIG, m, 0.0)
    e = jnp.where(valid, jnp.exp(s - m), 0.0)
    den = jnp.sum(e, axis=-1, keepdims=True)
    return e / jnp.where(den > 0.0, den, 1.0)


def _nsa_body(q_ref, kc_ref, vc_ref, ks_ref, vs_ref, kw_ref, vw_ref, sm_ref, o_ref, kaug):
    i = pl.program_id(1)
    tq, hd = q_ref.shape[2], HEAD_DIM
    s, nc = ks_ref.shape[1], kc_ref.shape[1]
    nsel = s // NSA_SEL_LEN
    n_top = min(NSA_TOP_N, nsel)
    rows = N_HEADS * tq

    @pl.when(i == 0)
    def _():
        place = (_iota((hd, LANES), 0) == _iota((hd, LANES), 1)).astype(BF16)
        onehot = (_iota((s, LANES), 1) - hd) == (_iota((s, LANES), 0) // NSA_SEL_LEN)
        kaug[...] = (_dot(ks_ref[0], place) + onehot.astype(F32)).astype(BF16)

    t0 = i * tq
    q2 = q_ref[0].reshape(rows, hd)
    tpos = t0 + _iota((tq, 1), 0)
    tpos4 = t0 + _iota((rows, 1), 0) % tq

    cmp_end = _iota((1, nc), 1) * NSA_CMP_STRIDE + (NSA_CMP_LEN - 1)
    p_cmp = _softmax_rows(_dot_nt(q2, kc_ref[0].astype(BF16)), cmp_end <= tpos4)
    o_cmp = _dot(p_cmp.astype(BF16), vc_ref[0].astype(BF16))

    psum = p_cmp[0:tq] + p_cmp[tq:2 * tq] + p_cmp[2 * tq:3 * tq] + p_cmp[3 * tq:4 * tq]
    cs, ss = _iota((nc, hd), 0) * NSA_CMP_STRIDE, _iota((nc, hd), 1) * NSA_SEL_LEN
    overlap = jnp.clip(jnp.minimum(cs + NSA_CMP_LEN, ss + NSA_SEL_LEN) - jnp.maximum(cs, ss), 0, NSA_CMP_LEN)
    importance = _dot_x(psum, (overlap.astype(F32) * (1.0 / NSA_CMP_LEN)).astype(BF16))
    blk = _iota((1, hd), 1)
    cur = tpos // NSA_SEL_LEN
    forced = (blk == 0) | (blk == cur) | (blk == cur - 1)
    score = jnp.where(forced, NSA_FORCE_SCORE, jnp.where(blk * NSA_SEL_LEN <= tpos, importance, -1.0))
    score = jnp.where(blk < nsel, score, -2.0)
    cnt = jnp.zeros((tq, hd), jnp.int32)
    for j in range(nsel):
        col = score[:, j:j + 1]
        cnt = cnt + ((col > score) | ((col == score) & (blk > j))).astype(jnp.int32)
    bias = jnp.where(cnt < n_top, 0.0, SEL_MASK_BIAS).astype(BF16)
    q_aug = jnp.concatenate([q2, jnp.concatenate([bias] * N_HEADS, axis=0)], axis=1)

    def sel_step(c, carry, masked=False):
        m, l, acc = carry
        keys = pl.ds(pl.multiple_of(c * NSA_TK, NSA_TK), NSA_TK)
        sc = _dot_nt(q_aug, kaug[keys, :])
        if masked:
            sc = jnp.where(c * NSA_TK + _iota((1, NSA_TK), 1) <= tpos4, sc, NEG_BIG)
        m_new = jnp.maximum(m, jnp.max(sc, axis=-1, keepdims=True))
        alpha = jnp.exp(m - m_new)
        p = jnp.exp(sc - m_new)
        return (m_new, alpha * l + jnp.sum(p, axis=-1, keepdims=True),
                alpha * acc + _dot(p.astype(BF16), vs_ref[0, keys, :]))

    def sel_group(g, carry):
        for u in range(NSA_UNROLL):
            carry = sel_step(g * NSA_UNROLL + u, carry)
        return carry

    n_full = t0 // NSA_TK
    carry = (jnp.full((rows, 1), NEG_BIG, F32), jnp.zeros((rows, 1), F32), jnp.zeros((rows, hd), F32))
    carry = lax.fori_loop(0, n_full // NSA_UNROLL, sel_group, carry)
    carry = lax.fori_loop((n_full // NSA_UNROLL) * NSA_UNROLL, n_full, sel_step, carry)
    _, l_sel, acc_sel = sel_step(n_full, carry, masked=True)
    o_sel = acc_sel / l_sel

    wlen = NSA_WINDOW + tq
    start = pl.multiple_of(jnp.maximum(t0 - NSA_WINDOW, 0), tq)
    kpos = start + _iota((1, wlen), 1)
    p_win = _softmax_rows(_dot_nt(q2, kw_ref[0, pl.ds(start, wlen), :]),
                          (kpos <= tpos4) & (kpos > tpos4 - NSA_WINDOW))
    o_win = _dot(p_win.astype(BF16), vw_ref[0, pl.ds(start, wlen), :])

    gates = _sigmoid(sm_ref[0])
    outs = []
    for h in range(N_HEADS):
        r = slice(h * tq, (h + 1) * tq)
        g0 = SM_GATE + 3 * h
        outs.append(gates[:, g0:g0 + 1] * o_cmp[r] + gates[:, g0 + 1:g0 + 2] * o_sel[r]
                    + gates[:, g0 + 2:g0 + 3] * o_win[r])
    o_ref[0] = jnp.concatenate(outs, axis=1).astype(o_ref.dtype)


def _nsa_attn(q, k_cmp, v_cmp, ks, vs, kw, vw, small):
    bsz, nh, s, hd = q.shape
    tq = NSA_TQ
    assert s % NSA_TK == 0 and s >= NSA_WINDOW + tq and s // NSA_SEL_LEN <= hd
    nc = k_cmp.shape[1]
    full = pl.BlockSpec((1, s, hd), lambda b, i: (b, 0, 0))
    cmp_spec = pl.BlockSpec((1, nc, hd), lambda b, i: (b, 0, 0))
    tile = pl.BlockSpec((1, nh, tq, hd), lambda b, i: (b, 0, i, 0))
    return pl.pallas_call(
        _nsa_body, grid=(bsz, s // tq),
        in_specs=[tile, cmp_spec, cmp_spec, full, full, full, full,
                  pl.BlockSpec((1, tq, LANES), lambda b, i: (b, i, 0))],
        out_specs=pl.BlockSpec((1, tq, MIX_W), lambda b, i: (b, i, 0)),
        out_shape=jax.ShapeDtypeStruct((bsz, s, MIX_W), BF16),
        scratch_shapes=[pltpu.VMEM((s, LANES), BF16)],
        compiler_params=_cp(("arbitrary", "arbitrary")), name="nsa_attn",
    )(q, k_cmp, v_cmp, ks, vs, kw, vw, small)


MERGE_T = 512


def _layer_norm(r, g, b):
    mu = jnp.mean(r, axis=-1, keepdims=True)
    xc = r - mu
    var = jnp.mean(xc * xc, axis=-1, keepdims=True)
    return xc * lax.rsqrt(var + LN_EPS) * g + b


def _merge_body(alpha, x_ref, sc_ref, sh_ref, gt_ref, og_ref, or_ref, on_ref, of_ref, wg_ref, bp_ref, wo_ref,
                lng_ref, lnb_ref, o_ref):
    x = x_ref[0]
    h = (x * sc_ref[0] + sh_ref[0]).astype(BF16)

    merged = None
    for br, o_br in enumerate((og_ref, or_ref, on_ref, of_ref)):
        term = _sigmoid(_dot(h, wg_ref[br])) * _dot(o_br[0], bp_ref[br])
        merged = term if merged is None else merged + term
    y = _dot(merged.astype(BF16), wo_ref[...])
    o_ref[0] = _layer_norm(alpha * x + gt_ref[0] * y, lng_ref[...], lnb_ref[...])


def _merge(x, sc1p, shift, gate, o_gdn, o_ret, o_nsa, o_fox, w_gate, branch_proj, w_out, ln_g, ln_b, alpha):
    bsz, s, d = x.shape
    t = min(MERGE_T, s)
    tok = lambda w: pl.BlockSpec((1, t, w), lambda b, i: (b, i, 0))
    vec = pl.BlockSpec((1, 1, d), lambda b, i: (b, 0, 0))
    const = lambda a: pl.BlockSpec(a.shape, lambda b, i: (0,) * a.ndim, pipeline_mode=pl.Buffered(1))
    lng, lnb = ln_g.reshape(1, d), ln_b.reshape(1, d)
    return pl.pallas_call(
        functools.partial(_merge_body, alpha), grid=(bsz, s // t),
        in_specs=[tok(d), vec, vec, vec, tok(MIX_W), tok(MIX_W), tok(MIX_W), tok(MIX_W),
                  const(w_gate), const(branch_proj), const(w_out), const(lng), const(lnb)],
        out_specs=tok(d), out_shape=jax.ShapeDtypeStruct((bsz, s, d), F32),
        compiler_params=_cp(("arbitrary", "arbitrary"), VMEM_LIMIT), name="merge",
    )(x, sc1p, shift, gate, o_gdn, o_ret, o_nsa, o_fox, w_gate, branch_proj, w_out, lng, lnb)


MOE_TM = 256
ROUTE_T = 512


def _router_body(tm, x_ref, sc_ref, sh_ref, rwt_ref, rb_ref, pos_ref, wt_ref, tile_ref, lg):
    i = pl.program_id(1)
    t = x_ref.shape[1]
    s = lg.shape[1]
    h = x_ref[0] * sc_ref[0] + sh_ref[0]
    a1, a2, _ = _split3(rwt_ref[...])
    b1, b2, _ = _split3(h)
    lg[:, pl.ds(pl.multiple_of(i * t, t), t)] = _dot_nt(a1, b1) + (_dot_nt(a1, b2) + _dot_nt(a2, b1))

    @pl.when(i == pl.num_programs(1) - 1)
    def _():
        scores = _sigmoid(lg[...])
        biased = scores + rb_ref[...]
        b = [biased[e:e + 1, :] for e in range(N_EXPERTS)]
        sc = [scores[e:e + 1, :] for e in range(N_EXPERTS)]
        gs = []
        for g in range(N_GROUPS):
            m = [b[EXPERTS_PER_GROUP * g + j] for j in range(EXPERTS_PER_GROUP)]
            best = m[0] + m[1]
            for u in range(EXPERTS_PER_GROUP):
                for v in range(u + 1, EXPERTS_PER_GROUP):
                    if (u, v) != (0, 1):
                        best = jnp.maximum(best, m[u] + m[v])
            gs.append(best)
        gsel, best = jnp.zeros((1, s), jnp.int32), gs[0]
        for g in range(1, N_GROUPS):
            take = gs[g] > best
            gsel = jnp.where(take, g, gsel)
            best = jnp.where(take, gs[g], best)
        first, second = [], []
        for e in range(N_EXPERTS):
            g = e // EXPERTS_PER_GROUP
            cnt = jnp.zeros((1, s), jnp.int32)
            for j in range(EXPERTS_PER_GROUP * g, EXPERTS_PER_GROUP * (g + 1)):
                if j != e:
                    ahead = (b[j] >= b[e]) if j < e else (b[j] > b[e])
                    cnt = cnt + ahead.astype(jnp.int32)
            first.append((gsel == g) & (cnt == 0))
            second.append((gsel == g) & (cnt == 1))
        s0 = sum(jnp.where(first[e], sc[e], 0.0) for e in range(N_EXPERTS))
        s1 = sum(jnp.where(second[e], sc[e], 0.0) for e in range(N_EXPERTS))
        den = s0 + s1
        wt_ref[0] = jnp.concatenate([s0 / den, s1 / den, jnp.zeros((1, LANES), F32)], axis=1)
        onehot = jnp.concatenate([(first[e] | second[e]).astype(F32) for e in range(N_EXPERTS)], axis=0)
        blk = min(512, s)
        before = (_iota((blk, blk), 0) < _iota((blk, blk), 1)).astype(BF16)
        carry, ranks = jnp.zeros((N_EXPERTS, 1), F32), []
        for j in range(s // blk):
            ob = onehot[:, j * blk:(j + 1) * blk]
            ranks.append(_dot(ob.astype(BF16), before) + carry)
            carry = carry + jnp.sum(ob, axis=1, keepdims=True)
        rank = jnp.concatenate(ranks, axis=1)
        ntile = jnp.floor((carry + (tm - 1.0)) * (1.0 / tm))
        lower = (_iota((N_EXPERTS, N_EXPERTS), 1) < _iota((N_EXPERTS, N_EXPERTS), 0)).astype(BF16)
        toff = _dot(lower, jnp.broadcast_to(ntile, (N_EXPERTS, LANES)).astype(BF16))[:, 0:1]
        slot = toff * float(tm) + rank
        pos0 = sum(jnp.where(first[e], slot[e:e + 1, :], 0.0) for e in range(N_EXPERTS))
        pos1 = sum(jnp.where(second[e], slot[e:e + 1, :], 0.0) for e in range(N_EXPERTS))
        pos_ref[0] = jnp.concatenate([pos0, pos1], axis=1).astype(jnp.int32)
        tend = toff + ntile
        tid = _iota((1, LANES), 1).astype(F32)
        texp = jnp.sum((tend <= tid).astype(F32), axis=0, keepdims=True)
        ntot = jnp.broadcast_to(jnp.sum(ntile, axis=0, keepdims=True), (1, LANES))
        diag = _iota((N_EXPERTS, LANES), 0) == _iota((N_EXPERTS, LANES), 1)
        to_lanes = lambda col: jnp.sum(jnp.where(diag, col, 0.0), axis=0, keepdims=True)
        tile_ref[0] = jnp.concatenate([jnp.minimum(texp, N_EXPERTS - 1.0), ntot, to_lanes(carry), to_lanes(toff)],
                                      axis=0).astype(jnp.int32)


def _router(x, sc1p, shift, router_w, router_b, tm):
    bsz, s, d = x.shape
    t = min(ROUTE_T, s)
    vec = pl.BlockSpec((1, 1, d), lambda b, i: (b, 0, 0))
    return pl.pallas_call(
        functools.partial(_router_body, tm), grid=(bsz, s // t),
        in_specs=[pl.BlockSpec((1, t, d), lambda b, i: (b, i, 0)), vec, vec,
                  pl.BlockSpec((N_EXPERTS, d), lambda b, i: (0, 0)),
                  pl.BlockSpec((N_EXPERTS, 1), lambda b, i: (0, 0))],
        out_specs=[pl.BlockSpec((1, 1, 2 * s), lambda b, i: (b, 0, 0)),
                   pl.BlockSpec((1, 1, 2 * s + LANES), lambda b, i: (b, 0, 0)),
                   pl.BlockSpec((1, 4, LANES), lambda b, i: (b, 0, 0))],
        out_shape=[jax.ShapeDtypeStruct((bsz, 1, 2 * s), jnp.int32),
                   jax.ShapeDtypeStruct((bsz, 1, 2 * s + LANES), F32),
                   jax.ShapeDtypeStruct((bsz, 4, LANES), jnp.int32)],
        scratch_shapes=[pltpu.VMEM((N_EXPERTS, s), F32)],
        compiler_params=_cp(("arbitrary", "arbitrary")), name="router",
    )(x, sc1p, shift, router_w.T, router_b.reshape(N_EXPERTS, 1))


LN_ROWS = 256


def _moe_body(alpha, tm, tiles_ref, x_ref, sc_ref, sh_ref, gt_ref, pos_ref, wt_ref,
              w1_ref, w3_ref, w2_ref, lng_ref, lnb_ref, o_ref, src, xs, ys):
    b, i = pl.program_id(0), pl.program_id(1)
    s = x_ref.shape[1]
    tb = b * (4 * LANES)

    @pl.when(i == 0)
    def _():
        for e in range(N_EXPERTS):
            cnt, first = tiles_ref[tb + 2 * LANES + e], tiles_ref[tb + 3 * LANES + e]
            lo = first * tm + cnt
            hi = (first + lax.shift_right_logical(cnt + (tm - 1), int(math.log2(tm)))) * tm

            def pad(p, c):
                src[p] = 2 * s
                return c
            lax.fori_loop(lo, hi, pad, 0)

        def fill(t, c):
            for k in range(2):
                src[pos_ref[0, 0, k * s + t]] = k * s + t
            return c
        lax.fori_loop(0, s, fill, 0, unroll=8)
        o_ref[0] = jnp.zeros(o_ref.shape[1:], F32)

    @pl.when(i < tiles_ref[tb + LANES])
    def _():
        base = i * tm
        sc, sh = sc_ref[0], sh_ref[0]

        def gather(r, c):
            t = jnp.bitwise_and(src[base + r], s - 1)
            xs[pl.ds(r, 1), :] = x_ref[0, pl.ds(t, 1), :] * sc + sh
            return c
        lax.fori_loop(0, tm, gather, 0, unroll=8)
        xb = xs[...].astype(BF16)
        hid = _silu(_dot(xb, w1_ref[0])) * _dot(xb, w3_ref[0])
        ys[...] = _dot(hid.astype(BF16), w2_ref[0])

        def scatter(r, c):
            idx = src[base + r]
            t = jnp.bitwise_and(idx, s - 1)
            o_ref[0, pl.ds(t, 1), :] = o_ref[0, pl.ds(t, 1), :] + wt_ref[0, 0, idx] * ys[pl.ds(r, 1), :]
            return c
        lax.fori_loop(0, tm, scatter, 0, unroll=8)

    @pl.when(i == pl.num_programs(1) - 1)
    def _():
        def ln(j, c):
            rows = pl.ds(pl.multiple_of(j * LN_ROWS, LN_ROWS), LN_ROWS)
            r = alpha * x_ref[0, rows, :] + gt_ref[0] * o_ref[0, rows, :]
            o_ref[0, rows, :] = _layer_norm(r, lng_ref[...], lnb_ref[...])
            return c
        lax.fori_loop(0, s // LN_ROWS, ln, 0)


def _moe(x, sc1p, shift, gate, pos, wts, tiles, w1, w3, w2, ln_g, ln_b, alpha, tm):
    bsz, s, d = x.shape
    f = w1.shape[2]
    nt = 2 * s // tm + N_EXPERTS
    assert nt <= LANES and s & (s - 1) == 0 and tm & (tm - 1) == 0
    vec = pl.BlockSpec((1, 1, d), lambda b, i, tl: (b, 0, 0))
    seq = pl.BlockSpec((1, s, d), lambda b, i, tl: (b, 0, 0), pipeline_mode=pl.Buffered(1))
    smem = lambda n: pl.BlockSpec((1, 1, n), lambda b, i, tl: (b, 0, 0), memory_space=pltpu.SMEM)
    row = pl.BlockSpec((1, d), lambda b, i, tl: (0, 0))
    expert = lambda shape: pl.BlockSpec(shape, lambda b, i, tl: (tl[b * (4 * LANES) + i], 0, 0))
    grid_spec = pltpu.PrefetchScalarGridSpec(
        num_scalar_prefetch=1, grid=(bsz, nt),
        in_specs=[seq, vec, vec, vec, smem(2 * s), smem(2 * s + LANES),
                  expert((1, d, f)), expert((1, d, f)), expert((1, f, d)), row, row],
        out_specs=seq,
        scratch_shapes=[pltpu.SMEM((nt * tm,), jnp.int32), pltpu.VMEM((tm, d), F32), pltpu.VMEM((tm, d), F32)])
    return pl.pallas_call(
        functools.partial(_moe_body, alpha, tm), grid_spec=grid_spec,
        out_shape=jax.ShapeDtypeStruct((bsz, s, d), F32),
        compiler_params=_cp(("arbitrary", "arbitrary"), VMEM_LIMIT), name="moe_experts",
    )(tiles.reshape(-1), x, sc1p, shift, gate, pos, wts, w1, w3, w2, ln_g.reshape(1, d), ln_b.reshape(1, d))


def kernel(x, c, positions, ada_w, ada_b, w_in, gdn_conv_w, gdn_a_log, gdn_dt_bias, gdn_norm_w, ret_gn_w,
           nsa_cmp_pe, nsa_ck_w1, nsa_ck_w2, nsa_cv_w1, nsa_cv_w2, fox_f_bias, branch_proj, w_gate, w_out,
           ln_g, ln_b, router_w, router_b, exp_w1, exp_w3, exp_w2):
    depth, d = w_in.shape[0], x.shape[-1]
    alpha = (2.0 * depth) ** 0.25
    mod = _ada_mod(c, ada_w, ada_b)
    for l in range(depth):
        shift, sc1p, gate = mod[l, 0, :, :, :d], mod[l, 0, :, :, d:2 * d], mod[l, 0, :, :, 2 * d:]
        wcat, wst = _cat_in_weights(w_in[l])
        gqkv, gz, r, nq, kc, vc, ks, vs, kw, vw, fq, fk, fv, sm, smt = _in_proj(x, sc1p, shift, wcat, wst)
        o_gdn = _gdn(gqkv, sm, smt, gz, gdn_conv_w[l], gdn_a_log[l], gdn_dt_bias[l], gdn_norm_w[l])
        o_ret = _retention(r, positions, ret_gn_w[l])
        k_cmp, v_cmp = _nsa_compress(kc, vc, nsa_cmp_pe[l], nsa_ck_w1[l], nsa_ck_w2[l], nsa_cv_w1[l], nsa_cv_w2[l])
        o_nsa = _nsa_attn(nq, k_cmp, v_cmp, ks, vs, kw, vw, sm)
        o_fox = _fox_attn(fq, _fox_cum(sm, fk, fox_f_bias[l]), fv)
        x = _merge(x, sc1p, shift, gate, o_gdn, o_ret, o_nsa, o_fox, w_gate[l].astype(BF16),
                   branch_proj[l].astype(BF16), w_out[l].astype(BF16), ln_g[l, 0], ln_b[l, 0], alpha)
        shift, sc1p, gate = mod[l, 1, :, :, :d], mod[l, 1, :, :, d:2 * d], mod[l, 1, :, :, 2 * d:]
        pos, wts, tiles = _router(x, sc1p, shift, router_w, router_b, MOE_TM)
        x = _moe(x, sc1p, shift, gate, pos, wts, tiles, exp_w1[l].astype(BF16), exp_w3[l].astype(BF16),
                 exp_w2[l].astype(BF16), ln_g[l, 1], ln_b[l, 1], alpha, MOE_TM)
    return x
```

```python
import functools
import math

import jax
import jax.numpy as jnp
import numpy as np
from jax import lax
from jax.experimental import pallas as pl
from jax.experimental.pallas import tpu as pltpu

F32 = jnp.float32
BF16 = jnp.bfloat16

N_HEADS = 4
HEAD_DIM = 64
MIX_W = N_HEADS * HEAD_DIM
GDN_CHUNK = 64
CONV_K = 4
RET_CHUNK = 128
ROPE_BASE = 10000.0
NSA_CMP_LEN = 32
NSA_CMP_STRIDE = 16
NSA_SEL_LEN = 64
NSA_TOP_N = 16
NSA_WINDOW = 512
NSA_FORCE_SCORE = 1.0e4
N_EXPERTS = 16
N_GROUPS = 4
EXPERTS_PER_GROUP = N_EXPERTS // N_GROUPS
LN_EPS = 1e-5
LANES = 128
NEG_BIG = -1.0e30
SEL_MASK_BIAS = -30000.0

IN_WIDTHS = (
    MIX_W, MIX_W, MIX_W, N_HEADS, N_HEADS, MIX_W,
    MIX_W, MIX_W, MIX_W, MIX_W,
    MIX_W, HEAD_DIM, HEAD_DIM, HEAD_DIM, HEAD_DIM, HEAD_DIM, HEAD_DIM, 3 * N_HEADS,
    MIX_W, MIX_W, MIX_W, N_HEADS,
)
_IN_OFF = np.concatenate([[0], np.cumsum(IN_WIDTHS)]).astype(int)
(_GQ, _GK, _GV, _GA, _GB, _GZ, _RQ, _RK, _RV, _RG, _NQ, _NKC, _NVC, _NKS, _NVS, _NKW, _NVW, _NGATE,
 _FQ, _FK, _FV, _FF) = range(22)
SM_A, SM_B, SM_GATE, SM_F = 0, 4, 8, 20

VMEM_LIMIT = 56 * 1024 * 1024


def _cp(sem, vmem=None):
    return pltpu.CompilerParams(dimension_semantics=sem, vmem_limit_bytes=vmem)


def _sigmoid(x):
    return 1.0 / (1.0 + jnp.exp(-x))


def _silu(x):
    return x * _sigmoid(x)


def _softplus(x):
    return jnp.maximum(x, 0.0) + jnp.log1p(jnp.exp(-jnp.abs(x)))


def _dot(a, b):
    return jnp.dot(a, b, preferred_element_type=F32)


def _dot_nt(a, b):
    return lax.dot_general(a, b, (((1,), (1,)), ((), ())), preferred_element_type=F32)


def _split3(a):
    a1 = a.astype(BF16)
    r = a - a1.astype(F32)
    a2 = r.astype(BF16)
    a3 = (r - a2.astype(F32)).astype(BF16)
    return a1, a2, a3


def _dot_x(a, m):
    a1, a2, a3 = _split3(a)
    return _dot(a1, m) + _dot(a2, m) + _dot(a3, m)


def _xdot(m, a):
    a1, a2, a3 = _split3(a)
    return _dot(m, a1) + _dot(m, a2) + _dot(m, a3)


def _dot_hp(a, b):
    a1, a2, _ = _split3(a)
    b1, b2, _ = _split3(b)
    return _dot(a1, b1) + (_dot(a1, b2) + _dot(a2, b1))


def _iota(shape, dim):
    return lax.broadcasted_iota(jnp.int32, shape, dim)


def _div(x, n):
    return lax.shift_right_logical(x, int(math.log2(n)))


def _mod(x, n):
    return x & (n - 1)


def _head_ones():
    return _div(_iota((MIX_W, MIX_W), 0), HEAD_DIM) == _div(_iota((MIX_W, MIX_W), 1), HEAD_DIM)


def _head_sum(x, ones):
    return _dot_x(x, ones)


def _ada_body(c_ref, w_ref, b_ref, o_ref):
    mod = _dot_hp(_silu(c_ref[...]), w_ref[0]) + b_ref[0]
    o_ref[0] = jnp.where(pl.program_id(1) == 1, 1.0 + mod, mod)


def _ada_mod(c, ada_w, ada_b):
    depth = ada_w.shape[0]
    bsz, d = c.shape
    n = depth * 2
    w = ada_w.reshape(n, d, 3 * d)
    b = ada_b.reshape(n, 1, 3 * d)
    out = pl.pallas_call(
        _ada_body, grid=(n, 3),
        in_specs=[pl.BlockSpec((bsz, d), lambda i, j: (0, 0)),
                  pl.BlockSpec((1, d, d), lambda i, j: (i, 0, j)),
                  pl.BlockSpec((1, 1, d), lambda i, j: (i, 0, j))],
        out_specs=pl.BlockSpec((1, bsz, d), lambda i, j: (i, 0, j)),
        out_shape=jax.ShapeDtypeStruct((n, bsz, 3 * d), F32),
        compiler_params=_cp(("arbitrary", "arbitrary")), name="ada_mod",
    )(c, w, b)
    return out.reshape(depth, 2, bsz, 1, 3 * d)


IN_TS = 512
_W_GROUPS = (768, 256, 1024, 256, 512, 128)
_W_OFF = np.concatenate([[0], np.cumsum(_W_GROUPS)]).astype(int)
IN_CAT_W = int(_W_OFF[-1])
_WT_GROUPS = (LANES, MIX_W, HEAD_DIM, HEAD_DIM, MIX_W)
_WT_OFF = np.concatenate([[0], np.cumsum(_WT_GROUPS)]).astype(int)
IN_CAT_T = int(_WT_OFF[-1])


def _cat_in_weights(w_in):
    def col(i):
        return w_in[:, _IN_OFF[i]:_IN_OFF[i + 1]]
    small = jnp.concatenate([col(_GA), col(_GB), col(_NGATE), col(_FF)], axis=1)
    small = jnp.pad(small, ((0, 0), (0, LANES - small.shape[1])))
    scale = HEAD_DIM ** -0.5
    cat = jnp.concatenate([
        col(_GQ), col(_GK), col(_GV), col(_GZ),
        col(_RQ), col(_RK) * scale, col(_RV), col(_RG),
        col(_NKC), col(_NVC), col(_NKS), col(_NKW),
        col(_FQ) * scale, col(_FK),
        small], axis=1)
    cat_t = jnp.concatenate([small, col(_NQ) * scale, col(_NVS), col(_NVW), col(_FV)], axis=1).T
    return cat.astype(BF16), cat_t.astype(BF16)


def _inproj_body(x_ref, sc_ref, sh_ref, w_ref, wt_ref,
                 gqkv_ref, gz_ref, r_ref, kc_ref, vc_ref, ks_ref, kw_ref, fq_ref, fk_ref, sm_ref,
                 smt_ref, nqt_ref, vst_ref, vwt_ref, fvt_ref):
    h = (x_ref[0] * sc_ref[0] + sh_ref[0]).astype(BF16)

    def proj(g):
        return _dot(h, w_ref[:, _W_OFF[g]:_W_OFF[g + 1]])

    gqkv_ref[0] = proj(0)
    gz_ref[0] = proj(1)
    r_ref[0] = proj(2)
    nkv = proj(3)
    for i, ref in enumerate((kc_ref, vc_ref, ks_ref, kw_ref)):
        ref[0] = nkv[:, i * HEAD_DIM:(i + 1) * HEAD_DIM].astype(ref.dtype)
    f = proj(4)
    for j, ref in enumerate((fq_ref, fk_ref)):
        for hd in range(N_HEADS):
            lo = j * MIX_W + hd * HEAD_DIM
            ref[0, hd] = f[:, lo:lo + HEAD_DIM].astype(BF16)
    sm_ref[0] = proj(5)
    for g, ref in enumerate((smt_ref, nqt_ref, vst_ref, vwt_ref, fvt_ref)):
        ref[0] = _dot_nt(wt_ref[_WT_OFF[g]:_WT_OFF[g + 1], :], h).astype(ref.dtype)


def _in_proj(x, sc1p, shift, wcat, wcat_t):
    bsz, s, d = x.shape
    ts = min(IN_TS, s)
    tok = lambda w, dt: jax.ShapeDtypeStruct((bsz, s, w), dt)
    chan = lambda w, dt: jax.ShapeDtypeStruct((bsz, w, s), dt)
    hm = jax.ShapeDtypeStruct((bsz, N_HEADS, s, HEAD_DIM), BF16)
    out_shape = [tok(768, F32), tok(256, F32), tok(1024, F32),
                 tok(64, F32), tok(64, F32), tok(64, BF16), tok(64, BF16), hm, hm, tok(LANES, F32),
                 chan(LANES, F32), chan(MIX_W, BF16), chan(HEAD_DIM, BF16), chan(HEAD_DIM, BF16), chan(MIX_W, BF16)]
    tspec = lambda w: pl.BlockSpec((1, ts, w), lambda b, i: (b, i, 0))
    cspec = lambda w: pl.BlockSpec((1, w, ts), lambda b, i: (b, 0, i))
    hspec = pl.BlockSpec((1, N_HEADS, ts, HEAD_DIM), lambda b, i: (b, 0, i, 0))
    out_specs = [tspec(768), tspec(256), tspec(1024),
                 tspec(64), tspec(64), tspec(64), tspec(64), hspec, hspec, tspec(LANES),
                 cspec(LANES), cspec(MIX_W), cspec(HEAD_DIM), cspec(HEAD_DIM), cspec(MIX_W)]
    vec = pl.BlockSpec((1, 1, d), lambda b, i: (b, 0, 0))
    return pl.pallas_call(
        _inproj_body, grid=(bsz, s // ts),
        in_specs=[tspec(d), vec, vec,
                  pl.BlockSpec((d, IN_CAT_W), lambda b, i: (0, 0)),
                  pl.BlockSpec((IN_CAT_T, d), lambda b, i: (0, 0))],
        out_specs=out_specs, out_shape=out_shape,
        compiler_params=_cp(("arbitrary", "arbitrary"), VMEM_LIMIT), name="in_proj",
    )(x, sc1p, shift, wcat, wcat_t)


CUM_T = 512


def _foxcum_body(sm_ref, k_ref, brow_ref, kaug_ref, carry_r):
    @pl.when(pl.program_id(1) == 0)
    def _():
        carry_r[...] = jnp.zeros_like(carry_r)

    t = sm_ref.shape[1]

    def log_sigmoid(v):
        return jnp.minimum(v, 0.0) - jnp.log1p(jnp.exp(-jnp.abs(v)))

    lower = (_iota((t, t), 0) >= _iota((t, t), 1)).astype(BF16)
    cc = _xdot(lower, log_sigmoid(sm_ref[0] + brow_ref[...])) + carry_r[...]
    carry_r[...] = cc[t - 1:t, :]
    parts = _split3(-cc)
    place = (_iota((HEAD_DIM, LANES), 0) == _iota((HEAD_DIM, LANES), 1)).astype(BF16)
    src, dst = _iota((LANES, LANES), 0), _iota((LANES, LANES), 1)
    for hd in range(N_HEADS):
        aug = _dot(k_ref[0, hd], place)
        for j, part in enumerate(parts):
            aug = aug + _dot(part, ((src == SM_F + hd) & (dst == HEAD_DIM + j)).astype(BF16))
        kaug_ref[0, hd] = aug.astype(BF16)


def _fox_cum(small, k, f_bias):
    bsz, s, _ = small.shape
    t = min(CUM_T, s)
    brow = jnp.zeros((1, LANES), F32).at[0, SM_F:SM_F + N_HEADS].set(f_bias)
    return pl.pallas_call(
        _foxcum_body, grid=(bsz, s // t),
        in_specs=[pl.BlockSpec((1, t, LANES), lambda b, i: (b, i, 0)),
                  pl.BlockSpec((1, N_HEADS, t, HEAD_DIM), lambda b, i: (b, 0, i, 0)),
                  pl.BlockSpec((1, LANES), lambda b, i: (0, 0))],
        out_specs=pl.BlockSpec((1, N_HEADS, t, LANES), lambda b, i: (b, 0, i, 0)),
        out_shape=jax.ShapeDtypeStruct((bsz, N_HEADS, s, LANES), BF16),
        scratch_shapes=[pltpu.VMEM((1, LANES), F32)],
        compiler_params=_cp(("arbitrary", "arbitrary")), name="fox_cum",
    )(small, k, brow)


FOX_T = 512


def _lockstep(gens):
    out, live = [None] * len(gens), list(range(len(gens)))
    while live:
        still = []
        for g in live:
            try:
                next(gens[g])
                still.append(g)
            except StopIteration as stop:
                out[g] = stop.value
        live = still
    return out


def _fox_body(q_ref, k_ref, vt_ref, o_ref):
    i = pl.program_id(1)
    t = q_ref.shape[2]
    place = (_iota((HEAD_DIM, LANES), 0) == _iota((HEAD_DIM, LANES), 1)).astype(BF16)
    lane = _iota((1, LANES), 1)
    ones = ((lane >= HEAD_DIM) & (lane < HEAD_DIM + 3)).astype(F32)
    causal = _iota((t, t), 0) <= _iota((t, t), 1)
    q_aug = [(_dot(q_ref[0, hd], place) + ones).astype(BF16) for hd in range(N_HEADS)]

    def head_step(hd, j, carry, masked):
        m, l, acc = carry
        keys = pl.ds(pl.multiple_of(j * t, t), t)
        s = _dot_nt(k_ref[0, hd, keys, :], q_aug[hd])
        yield
        if masked:
            s = jnp.where(causal, s, NEG_BIG)
        m_new = jnp.maximum(m, jnp.max(s, axis=0, keepdims=True))
        yield
        alpha = jnp.exp(m - m_new)
        p = jnp.exp(s - m_new)
        yield
        return (m_new, alpha * l + jnp.sum(p, axis=0, keepdims=True),
                alpha * acc + _dot(vt_ref[0, hd * HEAD_DIM:(hd + 1) * HEAD_DIM, keys], p.astype(BF16)))

    def step(j, carries, masked=False):
        return tuple(_lockstep([head_step(hd, j, carries[hd], masked) for hd in range(N_HEADS)]))

    carries = tuple((jnp.full((1, t), NEG_BIG, F32), jnp.zeros((1, t), F32), jnp.zeros((HEAD_DIM, t), F32))
                    for _ in range(N_HEADS))
    carries = lax.fori_loop(0, i, step, carries)
    carries = step(i, carries, masked=True)
    o_ref[0] = jnp.concatenate([acc / l for (_, l, acc) in carries], axis=0).T.astype(o_ref.dtype)


def _fox_attn(q, kaug, vt):
    bsz, nh, s, hd = q.shape
    t = min(FOX_T, s)
    return pl.pallas_call(
        _fox_body, grid=(bsz, s // t),
        in_specs=[pl.BlockSpec((1, nh, t, hd), lambda b, i: (b, 0, i, 0)),
                  pl.BlockSpec((1, nh, s, LANES), lambda b, i: (b, 0, 0, 0)),
                  pl.BlockSpec((1, MIX_W, s), lambda b, i: (b, 0, 0))],
        out_specs=pl.BlockSpec((1, t, MIX_W), lambda b, i: (b, i, 0)),
        out_shape=jax.ShapeDtypeStruct((bsz, s, MIX_W), BF16),
        compiler_params=_cp(("arbitrary", "arbitrary")), name="fox_attn",
    )(q, kaug, vt)


GDN_T = 256
GDN_NB = 2


def _gdn_body(qkv_ref, sm_ref, smt_ref, z_ref, cw_ref, alr_ref, dtr_ref, alc_ref, dtc_ref, nw_ref,
              o_ref, xbuf, state):
    t, c = GDN_T, GDN_CHUNK
    ri, ci = _iota((t, t), 0), _iota((t, t), 1)
    same = _div(ri, c) == _div(ci, c)
    er, ec = _iota((LANES, MIX_W), 0), _div(_iota((LANES, MIX_W), 1), HEAD_DIM)
    head_blk = _head_ones()
    masks = dict(same=same, incl=same & (ci <= ri), strict=same & (ci < ri),
                 same_b=same.astype(BF16), incl_b=(same & (ci <= ri)).astype(BF16),
                 incl_t_b=(same & (ri <= ci)).astype(BF16),
                 exp_a=(er == ec + SM_A).astype(BF16),
                 exp_b=(er == ec + SM_B).astype(BF16),
                 head_blk=head_blk, head_ones=head_blk.astype(BF16),
                 lane_head=_div(_iota((1, MIX_W), 1), HEAD_DIM))
    _lockstep([_gdn_tile(masks, qkv_ref.at[bb], sm_ref.at[bb], smt_ref.at[bb], z_ref.at[bb], cw_ref, alr_ref,
                         dtr_ref, alc_ref, dtc_ref, nw_ref, o_ref.at[bb], xbuf.at[bb], state.at[bb])
               for bb in range(qkv_ref.shape[0])])


def _gdn_tile(masks, qkv_ref, sm_ref, smt_ref, z_ref, cw_ref, alr_ref, dtr_ref, alc_ref, dtc_ref, nw_ref,
              o_ref, xbuf, state):
    t, c = GDN_T, GDN_CHUNK
    incl, strict, head_ones = masks["incl"], masks["strict"], masks["head_ones"]

    @pl.when(pl.program_id(1) == 0)
    def _():
        xbuf[0:8, :] = jnp.zeros((8, 3 * MIX_W), F32)
        state[...] = jnp.zeros_like(state)

    xbuf[8:8 + t, :] = qkv_ref[...]
    conv = cw_ref[0:1, :] * xbuf[pl.ds(8 - CONV_K + 1, t), :]
    for kk in range(1, CONV_K):
        conv = conv + cw_ref[kk:kk + 1, :] * xbuf[pl.ds(8 - CONV_K + 1 + kk, t), :]
    xbuf[0:8, :] = xbuf[t:t + 8, :]
    act = _silu(conv)
    q, k, v = act[:, :MIX_W], act[:, MIX_W:2 * MIX_W], act[:, 2 * MIX_W:]
    q = q * lax.rsqrt(_head_sum(q * q, head_ones) + 1e-6) * HEAD_DIM ** -0.5
    k = k * lax.rsqrt(_head_sum(k * k, head_ones) + 1e-6)

    sm, smt = sm_ref[...], smt_ref[...]
    la_col = -jnp.exp(alr_ref[...]) * _softplus(sm + dtr_ref[...])
    la_row = -jnp.exp(alc_ref[...]) * _softplus(smt + dtc_ref[...])
    beta_col = _sigmoid(sm)
    bc_col = _xdot(masks["incl_b"], la_col)
    last_col = _xdot(masks["same_b"], la_col)
    bc_row = _dot_x(la_row, masks["incl_t_b"])
    exp_a, exp_b = masks["exp_a"], masks["exp_b"]
    bc_x, last_x, beta_x = _dot_x(bc_col, exp_a), _dot_x(last_col, exp_a), _dot_x(beta_col, exp_b)
    eb = jnp.exp(bc_x)
    q_dec = q * eb
    k_dec = k * jnp.exp(last_x - bc_x)
    rhs_v = beta_x * v
    rhs_k = beta_x * eb * k
    rhs = jnp.concatenate([rhs_v, rhs_k], axis=1).astype(BF16)

    kb = k.astype(BF16)
    lane_head = masks["lane_head"]
    heads = range(N_HEADS)
    mh = [lane_head == hd for hd in heads]
    gram = [_dot_nt(jnp.where(mh[hd], k, 0.0).astype(BF16), kb) for hd in heads]
    qk = [_dot_nt(jnp.where(mh[hd], q, 0.0).astype(BF16), kb) for hd in heads]
    yield
    dec = [jnp.exp(jnp.minimum(bc_col[:, SM_A + hd:SM_A + hd + 1] - bc_row[SM_A + hd:SM_A + hd + 1, :], 0.0))
           for hd in heads]
    attn = [jnp.where(incl, qk[hd] * dec[hd], 0.0) for hd in heads]
    p = [jnp.where(strict, -(beta_col[:, SM_B + hd:SM_B + hd + 1] * gram[hd] * dec[hd]), 0.0) for hd in heads]
    tr = list(p)
    for _ in range(5):
        pb = [p[hd].astype(BF16) for hd in heads]
        p = [_dot(pb[hd], pb[hd]) for hd in heads]
        yield
        tr = [tr[hd] + p[hd] + _dot(tr[hd].astype(BF16), p[hd].astype(BF16)) for hd in heads]
    sol = [_dot(tr[hd].astype(BF16), rhs) for hd in heads]
    yield
    u0, w = rhs_v, rhs_k
    for hd in heads:
        u0 = u0 + jnp.where(mh[hd], sol[hd][:, :MIX_W], 0.0)
        w = w + jnp.where(mh[hd], sol[hd][:, MIX_W:], 0.0)

    head_blk = masks["head_blk"]
    k_dec_t = k_dec.T.astype(BF16)
    outs = []
    for n in range(t // c):
        rows = slice(n * c, (n + 1) * c)
        s_old = state[...]
        sb = s_old.astype(BF16)
        u_n = u0[rows] - _dot(w[rows].astype(BF16), sb)
        yield
        parts = ([jnp.zeros((n * c, MIX_W), F32)] if n else []) + [u_n]
        if t - (n + 1) * c:
            parts.append(jnp.zeros((t - (n + 1) * c, MIX_W), F32))
        u_full = jnp.concatenate(parts, axis=0).astype(BF16)
        oa = _dot(jnp.concatenate([a[rows] for a in attn], axis=0).astype(BF16), u_full)
        o_n = _dot(q_dec[rows].astype(BF16), sb)
        for hd in range(N_HEADS):
            o_n = o_n + jnp.where(lane_head == hd, oa[hd * c:(hd + 1) * c], 0.0)
        outs.append(o_n)
        g_row = jnp.exp(last_x[n * c:n * c + 1, :])
        state[...] = g_row * s_old + jnp.where(head_blk, _dot(k_dec_t, u_full), 0.0)
        yield
    o = jnp.concatenate(outs, axis=0)
    o = o * lax.rsqrt(_head_sum(o * o, head_ones) * (1.0 / HEAD_DIM) + 1e-6) * nw_ref[...]
    o_ref[...] = (o * _silu(z_ref[...])).astype(o_ref.dtype)


def _lane_vec(vals, off):
    return jnp.zeros((1, LANES), F32).at[0, off:off + vals.shape[0]].set(vals)


def _gdn(qkv, small, small_t, z, conv_w, a_log, dt_bias, norm_w):
    bsz, s, _ = qkv.shape
    t = GDN_T
    alr, dtr = _lane_vec(a_log, SM_A), _lane_vec(dt_bias, SM_A)
    nw = jnp.tile(norm_w, N_HEADS).reshape(1, MIX_W)
    const = lambda shape: pl.BlockSpec(shape, lambda b, i: (0,) * len(shape))
    nb = GDN_NB if bsz % GDN_NB == 0 else 1
    return pl.pallas_call(
        _gdn_body, grid=(bsz // nb, s // t),
        in_specs=[pl.BlockSpec((nb, t, 3 * MIX_W), lambda b, i: (b, i, 0)),
                  pl.BlockSpec((nb, t, LANES), lambda b, i: (b, i, 0)),
                  pl.BlockSpec((nb, LANES, t), lambda b, i: (b, 0, i)),
                  pl.BlockSpec((nb, t, MIX_W), lambda b, i: (b, i, 0)),
                  const((CONV_K, 3 * MIX_W)), const((1, LANES)), const((1, LANES)),
                  const((LANES, 1)), const((LANES, 1)), const((1, MIX_W))],
        out_specs=pl.BlockSpec((nb, t, MIX_W), lambda b, i: (b, i, 0)),
        out_shape=jax.ShapeDtypeStruct((bsz, s, MIX_W), BF16),
        scratch_shapes=[pltpu.VMEM((nb, t + 8, 3 * MIX_W), F32), pltpu.VMEM((nb, MIX_W, MIX_W), F32)],
        compiler_params=_cp(("arbitrary", "arbitrary")), name="gdn",
    )(qkv, small, small_t, z, conv_w, alr, dtr, alr.reshape(LANES, 1), dtr.reshape(LANES, 1), nw)


RET_T = 512
_RET_LOG_GAMMA = tuple(math.log1p(-(2.0 ** (-5.0 - h))) for h in range(N_HEADS))


def _per_head(lane_head, vals):
    out = jnp.full(lane_head.shape, vals[0], F32)
    for hd in range(1, N_HEADS):
        out = jnp.where(lane_head == hd, vals[hd], out)
    return out


def _ret_body(r_ref, pos_ref, gnw_ref, o_ref, state):
    t, c = r_ref.shape[1], RET_CHUNK

    @pl.when(pl.program_id(1) == 0)
    def _():
        state[...] = jnp.zeros_like(state)

    x = r_ref[0]
    lane = _iota((1, MIX_W), 1)
    lane_head = _div(lane, HEAD_DIM)
    half = HEAD_DIM // 2
    inv_freq = jnp.exp(_mod(lane, half).astype(F32) * (-math.log(ROPE_BASE) / half))
    ang = pos_ref[0].astype(F32) * inv_freq
    cos, sin = jnp.cos(ang), jnp.sin(ang)
    first = _mod(lane, HEAD_DIM) < half

    def rope(a):
        rot = jnp.where(first, -pltpu.roll(a, MIX_W - half, 1), pltpu.roll(a, half, 1))
        return a * cos + rot * sin

    q, k = rope(x[:, :MIX_W]), rope(x[:, MIX_W:2 * MIX_W])
    v, g = x[:, 2 * MIX_W:3 * MIX_W], x[:, 3 * MIX_W:]

    lg = _per_head(lane_head, _RET_LOG_GAMMA)
    cidx = _iota((c, 1), 0).astype(F32)
    xi = jnp.exp((cidx + 1.0) * lg)
    zeta = jnp.exp((c - 1.0 - cidx) * lg)
    chunk_decay = jnp.exp(float(c) * lg)
    rel = (_iota((c, c), 0) - _iota((c, c), 1)).astype(F32)
    dec = jnp.concatenate([jnp.where(rel >= 0, jnp.exp(jnp.maximum(rel, 0.0) * _RET_LOG_GAMMA[hd]), 0.0)
                           for hd in range(N_HEADS)], axis=0)
    head_blk = _head_ones()
    outs = []
    for n in range(t // c):
        rows = slice(n * c, (n + 1) * c)
        qn, kn, vn = q[rows], k[rows], v[rows].astype(BF16)
        qs = jnp.concatenate([jnp.where(lane_head == hd, qn, 0.0) for hd in range(N_HEADS)], axis=0).astype(BF16)
        sc = _dot_nt(qs, kn.astype(BF16)) * dec
        res = _dot(sc.astype(BF16), vn)
        s_old = state[...]
        o_n = _dot(qn.astype(BF16), s_old.astype(BF16)) * xi
        for hd in range(N_HEADS):
            o_n = o_n + jnp.where(lane_head == hd, res[hd * c:(hd + 1) * c], 0.0)
        outs.append(o_n)
        kv = _dot((kn * zeta).T.astype(BF16), vn)
        state[...] = chunk_decay * s_old + jnp.where(head_blk, kv, 0.0)
    o = jnp.concatenate(outs, axis=0)
    head_ones = head_blk.astype(BF16)
    mu = _head_sum(o, head_ones) * (1.0 / HEAD_DIM)
    xc = o - mu
    var = _head_sum(xc * xc, head_ones) * (1.0 / HEAD_DIM)
    y = xc * lax.rsqrt(var + LN_EPS) * gnw_ref[...]
    o_ref[0] = (_silu(g) * y).astype(o_ref.dtype)


def _retention(r, positions, gn_w):
    bsz, s, _ = r.shape
    t = min(RET_T, s)
    return pl.pallas_call(
        _ret_body, grid=(bsz, s // t),
        in_specs=[pl.BlockSpec((1, t, 4 * MIX_W), lambda b, i: (b, i, 0)),
                  pl.BlockSpec((1, t, 1), lambda b, i: (b, i, 0)),
                  pl.BlockSpec((1, MIX_W), lambda b, i: (0, 0))],
        out_specs=pl.BlockSpec((1, t, MIX_W), lambda b, i: (b, i, 0)),
        out_shape=jax.ShapeDtypeStruct((bsz, s, MIX_W), BF16),
        scratch_shapes=[pltpu.VMEM((MIX_W, MIX_W), F32)],
        compiler_params=_cp(("arbitrary", "arbitrary")), name="retention",
    )(r, positions.reshape(bsz, s, 1), gn_w.reshape(1, MIX_W))


def _nsacmp_body(kc_ref, vc_ref, pe_ref, kw1_ref, kw2_ref, vw1_ref, vw2_ref, ko_ref, vo_ref):
    half = NSA_CMP_STRIDE * HEAD_DIM
    nc = kc_ref.shape[1]
    def hidden(x_ref, w1_ref):
        x = x_ref[0]
        top = _dot_hp(x, w1_ref[0:half, :])
        bot = _dot_hp(x, w1_ref[half:2 * half, :])
        bias = _dot_hp(pe_ref[...], w1_ref[...])[0:1, :]
        return _silu(top + pltpu.roll(bot, nc - 1, 0) + bias)

    ko_ref[0] = _dot_hp(hidden(kc_ref, kw1_ref), kw2_ref[...])
    a1, a2, _ = _split3(vw2_ref[...])
    b1, b2, _ = _split3(hidden(vc_ref, vw1_ref))
    vo_ref[0] = _dot_nt(a1, b1) + (_dot_nt(a1, b2) + _dot_nt(a2, b1))


def _nsa_compress(kc, vc, pe, ck_w1, ck_w2, cv_w1, cv_w2):
    bsz, s, hd = kc.shape
    nc = s // NSA_CMP_STRIDE
    wide = NSA_CMP_STRIDE * hd
    pe8 = jnp.broadcast_to(pe.reshape(1, NSA_CMP_LEN * hd), (8, NSA_CMP_LEN * hd))
    cv_w2t = cv_w2.T
    const = lambda a: pl.BlockSpec(a.shape, lambda b: (0,) * a.ndim)
    xspec = pl.BlockSpec((1, nc, wide), lambda b: (b, 0, 0))
    return pl.pallas_call(
        _nsacmp_body, grid=(bsz,),
        in_specs=[xspec, xspec, const(pe8), const(ck_w1), const(ck_w2), const(cv_w1), const(cv_w2t)],
        out_specs=[pl.BlockSpec((1, nc, hd), lambda b: (b, 0, 0)), pl.BlockSpec((1, hd, nc), lambda b: (b, 0, 0))],
        out_shape=[jax.ShapeDtypeStruct((bsz, nc, hd), F32), jax.ShapeDtypeStruct((bsz, hd, nc), F32)],
        compiler_params=_cp(("arbitrary",)), name="nsa_compress",
    )(kc.reshape(bsz, nc, wide), vc.reshape(bsz, nc, wide), pe8, ck_w1, ck_w2, cv_w1, cv_w2t)


NSA_TQ = 128
NSA_TK = 512
NSA_UNROLL = 2


def _softmax_cols(s, valid):
    m = jnp.max(jnp.where(valid, s, NEG_BIG), axis=0, keepdims=True)
    m = jnp.where(m > 0.5 * NEG_BIG, m, 0.0)
    e = jnp.where(valid, jnp.exp(s - m), 0.0)
    den = jnp.sum(e, axis=0, keepdims=True)
    return e / jnp.where(den > 0.0, den, 1.0)


def _nsa_body(qt_ref, kc_ref, vct_ref, ks_ref, vst_ref, kw_ref, vwt_ref, smt_ref, o_ref, kaug):
    i = pl.program_id(1)
    tq, hd = qt_ref.shape[2], HEAD_DIM
    s, nc = ks_ref.shape[1], kc_ref.shape[1]
    nsel = s // NSA_SEL_LEN
    n_top = min(NSA_TOP_N, nsel)
    cols = N_HEADS * tq

    @pl.when(i == 0)
    def _():
        place = (_iota((hd, LANES), 0) == _iota((hd, LANES), 1)).astype(BF16)
        onehot = (_iota((s, LANES), 1) - hd) == _div(_iota((s, LANES), 0), NSA_SEL_LEN)
        kaug[...] = (_dot(ks_ref[0], place) + onehot.astype(F32)).astype(BF16)

    t0 = i * tq
    qt = qt_ref[0]
    q4 = jnp.concatenate([qt[h * hd:(h + 1) * hd, :] for h in range(N_HEADS)], axis=1)
    tpos = t0 + _iota((1, tq), 1)
    tpos4 = t0 + _mod(_iota((1, cols), 1), tq)

    cmp_end = _iota((nc, 1), 0) * NSA_CMP_STRIDE + (NSA_CMP_LEN - 1)
    p_cmp = _softmax_cols(_dot(kc_ref[0].astype(BF16), q4), cmp_end <= tpos4)
    o_cmp = _dot(vct_ref[0].astype(BF16), p_cmp.astype(BF16))

    psum = p_cmp[:, 0:tq] + p_cmp[:, tq:2 * tq] + p_cmp[:, 2 * tq:3 * tq] + p_cmp[:, 3 * tq:4 * tq]
    ss, cs = _iota((hd, nc), 0) * NSA_SEL_LEN, _iota((hd, nc), 1) * NSA_CMP_STRIDE
    overlap = jnp.clip(jnp.minimum(cs + NSA_CMP_LEN, ss + NSA_SEL_LEN) - jnp.maximum(cs, ss), 0, NSA_CMP_LEN)
    importance = _xdot((overlap.astype(F32) * (1.0 / NSA_CMP_LEN)).astype(BF16), psum)
    blk = _iota((hd, 1), 0)
    cur = _div(tpos, NSA_SEL_LEN)
    forced = (blk == 0) | (blk == cur) | (blk == cur - 1)
    score = jnp.where(forced, NSA_FORCE_SCORE, jnp.where(blk * NSA_SEL_LEN <= tpos, importance, -1.0))
    score = jnp.where(blk < nsel, score, -2.0)
    groups = [score[8 * g:8 * g + 8, :] for g in range(hd // 8)]
    gblk = _iota((8, 1), 0)
    cnts = [jnp.zeros((8, tq), jnp.int32) for _ in groups]
    for j in range(nsel):
        row = score[j:j + 1, :]
        for g in range(hd // 8):
            if 8 * g > j:
                ahead = row >= groups[g]
            elif 8 * g + 7 <= j:
                ahead = row > groups[g]
            else:
                ahead = (row > groups[g]) | ((row == groups[g]) & (gblk + 8 * g > j))
            cnts[g] = cnts[g] + ahead.astype(jnp.int32)
    bias = jnp.where(jnp.concatenate(cnts, axis=0) < n_top, 0.0, SEL_MASK_BIAS).astype(BF16)
    q_aug = jnp.concatenate([q4, jnp.concatenate([bias] * N_HEADS, axis=1)], axis=0)

    def sel_step(c, carry, masked=False):
        m, l, acc = carry
        keys = pl.ds(pl.multiple_of(c * NSA_TK, NSA_TK), NSA_TK)
        sc = _dot(kaug[keys, :], q_aug)
        if masked:
            sc = jnp.where(c * NSA_TK + _iota((NSA_TK, 1), 0) <= tpos4, sc, NEG_BIG)
        m_new = jnp.maximum(m, jnp.max(sc, axis=0, keepdims=True))
        alpha = jnp.exp(m - m_new)
        p = jnp.exp(sc - m_new)
        return (m_new, alpha * l + jnp.sum(p, axis=0, keepdims=True),
                alpha * acc + _dot(vst_ref[0, :, keys], p.astype(BF16)))

    def sel_group(g, carry):
        for u in range(NSA_UNROLL):
            carry = sel_step(g * NSA_UNROLL + u, carry)
        return carry

    n_full = _div(t0, NSA_TK)
    n_groups = _div(n_full, NSA_UNROLL)
    carry = (jnp.full((1, cols), NEG_BIG, F32), jnp.zeros((1, cols), F32), jnp.zeros((hd, cols), F32))
    carry = lax.fori_loop(0, n_groups, sel_group, carry)
    carry = lax.fori_loop(n_groups * NSA_UNROLL, n_full, sel_step, carry)
    _, l_sel, acc_sel = sel_step(n_full, carry, masked=True)
    o_sel = acc_sel / l_sel

    wlen = NSA_WINDOW + tq
    start = pl.multiple_of(jnp.maximum(t0 - NSA_WINDOW, 0), tq)
    kpos = start + _iota((wlen, 1), 0)
    p_win = _softmax_cols(_dot(kw_ref[0, pl.ds(start, wlen), :], q4),
                          (kpos <= tpos4) & (kpos > tpos4 - NSA_WINDOW))
    o_win = _dot(vwt_ref[0, :, pl.ds(start, wlen)], p_win.astype(BF16))

    gates = _sigmoid(smt_ref[0][SM_GATE:SM_GATE + 16, :])
    outs = []
    for h in range(N_HEADS):
        c = slice(h * tq, (h + 1) * tq)
        outs.append(gates[3 * h:3 * h + 1, :] * o_cmp[:, c] + gates[3 * h + 1:3 * h + 2, :] * o_sel[:, c]
                    + gates[3 * h + 2:3 * h + 3, :] * o_win[:, c])
    o_ref[0] = jnp.concatenate(outs, axis=0).T.astype(o_ref.dtype)


def _nsa_attn(qt, k_cmp, v_cmp_t, ks, vst, kw, vwt, small_t):
    bsz, _, s = qt.shape
    tq, hd = NSA_TQ, HEAD_DIM
    assert s % NSA_TK == 0 and NSA_TK % tq == 0 and s >= NSA_WINDOW + tq and s // NSA_SEL_LEN <= hd
    nc = k_cmp.shape[1]
    keys = pl.BlockSpec((1, s, hd), lambda b, i: (b, 0, 0))
    vals = pl.BlockSpec((1, hd, s), lambda b, i: (b, 0, 0))
    return pl.pallas_call(
        _nsa_body, grid=(bsz, s // tq),
        in_specs=[pl.BlockSpec((1, MIX_W, tq), lambda b, i: (b, 0, i)),
                  pl.BlockSpec((1, nc, hd), lambda b, i: (b, 0, 0)),
                  pl.BlockSpec((1, hd, nc), lambda b, i: (b, 0, 0)),
                  keys, vals, keys, vals,
                  pl.BlockSpec((1, LANES, tq), lambda b, i: (b, 0, i))],
        out_specs=pl.BlockSpec((1, tq, MIX_W), lambda b, i: (b, i, 0)),
        out_shape=jax.ShapeDtypeStruct((bsz, s, MIX_W), BF16),
        scratch_shapes=[pltpu.VMEM((s, LANES), BF16)],
        compiler_params=_cp(("arbitrary", "arbitrary")), name="nsa_attn",
    )(qt, k_cmp, v_cmp_t, ks, vst, kw, vwt, small_t)


MERGE_T = 512


def _layer_norm(r, g, b):
    mu = jnp.mean(r, axis=-1, keepdims=True)
    xc = r - mu
    var = jnp.mean(xc * xc, axis=-1, keepdims=True)
    return xc * lax.rsqrt(var + LN_EPS) * g + b


def _merge_body(alpha, x_ref, sc_ref, sh_ref, gt_ref, og_ref, or_ref, on_ref, of_ref, wg_ref, bp_ref, wo_ref,
                lng_ref, lnb_ref, o_ref):
    x = x_ref[0]
    h = (x * sc_ref[0] + sh_ref[0]).astype(BF16)

    merged = None
    for br, o_br in enumerate((og_ref, or_ref, on_ref, of_ref)):
        term = _sigmoid(_dot(h, wg_ref[br])) * _dot(o_br[0], bp_ref[br])
        merged = term if merged is None else merged + term
    y = _dot(merged.astype(BF16), wo_ref[...])
    o_ref[0] = _layer_norm(alpha * x + gt_ref[0] * y, lng_ref[...], lnb_ref[...])


def _merge(x, sc1p, shift, gate, o_gdn, o_ret, o_nsa, o_fox, w_gate, branch_proj, w_out, ln_g, ln_b, alpha):
    bsz, s, d = x.shape
    t = min(MERGE_T, s)
    tok = lambda w: pl.BlockSpec((1, t, w), lambda b, i: (b, i, 0))
    vec = pl.BlockSpec((1, 1, d), lambda b, i: (b, 0, 0))
    const = lambda a: pl.BlockSpec(a.shape, lambda b, i: (0,) * a.ndim, pipeline_mode=pl.Buffered(1))
    lng, lnb = ln_g.reshape(1, d), ln_b.reshape(1, d)
    return pl.pallas_call(
        functools.partial(_merge_body, alpha), grid=(bsz, s // t),
        in_specs=[tok(d), vec, vec, vec, tok(MIX_W), tok(MIX_W), tok(MIX_W), tok(MIX_W),
                  const(w_gate), const(branch_proj), const(w_out), const(lng), const(lnb)],
        out_specs=tok(d), out_shape=jax.ShapeDtypeStruct((bsz, s, d), F32),
        compiler_params=_cp(("arbitrary", "arbitrary"), VMEM_LIMIT), name="merge",
    )(x, sc1p, shift, gate, o_gdn, o_ret, o_nsa, o_fox, w_gate, branch_proj, w_out, lng, lnb)


MOE_TM = 256
ROUTE_T = 512


def _pack_bf16_pair(a, b):
    hi = lax.bitcast_convert_type(a.astype(BF16).astype(F32), jnp.uint32)
    lo = lax.bitcast_convert_type(b.astype(BF16).astype(F32), jnp.uint32)
    return hi | lax.shift_right_logical(lo, jnp.uint32(16))


def _unpack_bf16_pair(w):
    a = lax.bitcast_convert_type(w & jnp.uint32(0xFFFF0000), F32)
    b = lax.bitcast_convert_type(lax.shift_left(w, jnp.uint32(16)), F32)
    return a, b


def _router_body(tm, x_ref, sc_ref, sh_ref, rwt_ref, rb_ref, hp_ref, pos_ref, wcol_ref, tile_ref, lg):
    i = pl.program_id(1)
    t = x_ref.shape[1]
    s = lg.shape[1]
    h = x_ref[0] * sc_ref[0] + sh_ref[0]
    half = h.shape[1] // 2
    hp_ref[0] = _pack_bf16_pair(h[:, :half], h[:, half:])
    a1, a2, _ = _split3(rwt_ref[...])
    b1, b2, _ = _split3(h)
    lg[:, pl.ds(pl.multiple_of(i * t, t), t)] = _dot_nt(a1, b1) + (_dot_nt(a1, b2) + _dot_nt(a2, b1))

    @pl.when(i == pl.num_programs(1) - 1)
    def _():
        scores = _sigmoid(lg[...])
        biased = scores + rb_ref[...]
        b = [biased[e:e + 1, :] for e in range(N_EXPERTS)]
        sc = [scores[e:e + 1, :] for e in range(N_EXPERTS)]
        gs = []
        for g in range(N_GROUPS):
            m = [b[EXPERTS_PER_GROUP * g + j] for j in range(EXPERTS_PER_GROUP)]
            best = m[0] + m[1]
            for u in range(EXPERTS_PER_GROUP):
                for v in range(u + 1, EXPERTS_PER_GROUP):
                    if (u, v) != (0, 1):
                        best = jnp.maximum(best, m[u] + m[v])
            gs.append(best)
        gsel, best = jnp.zeros((1, s), jnp.int32), gs[0]
        for g in range(1, N_GROUPS):
            take = gs[g] > best
            gsel = jnp.where(take, g, gsel)
            best = jnp.where(take, gs[g], best)
        first, second = [], []
        for e in range(N_EXPERTS):
            g = e // EXPERTS_PER_GROUP
            cnt = jnp.zeros((1, s), jnp.int32)
            for j in range(EXPERTS_PER_GROUP * g, EXPERTS_PER_GROUP * (g + 1)):
                if j != e:
                    ahead = (b[j] >= b[e]) if j < e else (b[j] > b[e])
                    cnt = cnt + ahead.astype(jnp.int32)
            first.append((gsel == g) & (cnt == 0))
            second.append((gsel == g) & (cnt == 1))
        s0 = sum(jnp.where(first[e], sc[e], 0.0) for e in range(N_EXPERTS))
        s1 = sum(jnp.where(second[e], sc[e], 0.0) for e in range(N_EXPERTS))
        den = s0 + s1
        w0, w1 = s0 / den, s1 / den
        blk = min(512, s)
        eye = (_iota((blk, blk), 0) == _iota((blk, blk), 1)).astype(BF16)
        rowid = _iota((LANES, blk), 0)
        for j in range(s // blk):
            cols = slice(j * blk, (j + 1) * blk)
            wb = jnp.where(rowid == 0, w0[:, cols], jnp.where(rowid == 1, w1[:, cols], 0.0))
            wcol_ref[0, cols, :] = sum(_dot_nt(eye, part) for part in _split3(wb))
        onehot = jnp.concatenate([(first[e] | second[e]).astype(F32) for e in range(N_EXPERTS)], axis=0)
        before = (_iota((blk, blk), 0) < _iota((blk, blk), 1)).astype(BF16)
        carry, ranks = jnp.zeros((N_EXPERTS, 1), F32), []
        for j in range(s // blk):
            ob = onehot[:, j * blk:(j + 1) * blk]
            ranks.append(_dot(ob.astype(BF16), before) + carry)
            carry = carry + jnp.sum(ob, axis=1, keepdims=True)
        rank = jnp.concatenate(ranks, axis=1)
        ntile = jnp.floor((carry + (tm - 1.0)) * (1.0 / tm))
        lower = (_iota((N_EXPERTS, N_EXPERTS), 1) < _iota((N_EXPERTS, N_EXPERTS), 0)).astype(BF16)
        toff = _dot(lower, jnp.broadcast_to(ntile, (N_EXPERTS, LANES)).astype(BF16))[:, 0:1]
        slot = toff * float(tm) + rank
        pos0 = sum(jnp.where(first[e], slot[e:e + 1, :], 0.0) for e in range(N_EXPERTS))
        pos1 = sum(jnp.where(second[e], slot[e:e + 1, :], 0.0) for e in range(N_EXPERTS))
        pos_ref[0] = jnp.concatenate([pos0, pos1], axis=1).astype(jnp.int32)
        tend = toff + ntile
        tid = _iota((1, LANES), 1).astype(F32)
        texp = jnp.sum((tend <= tid).astype(F32), axis=0, keepdims=True)
        ntot = jnp.broadcast_to(jnp.sum(ntile, axis=0, keepdims=True), (1, LANES))
        diag = _iota((N_EXPERTS, LANES), 0) == _iota((N_EXPERTS, LANES), 1)
        to_lanes = lambda col: jnp.sum(jnp.where(diag, col, 0.0), axis=0, keepdims=True)
        tile_ref[0] = jnp.concatenate([jnp.minimum(texp, N_EXPERTS - 1.0), ntot, to_lanes(carry), to_lanes(toff)],
                                      axis=0).astype(jnp.int32)


def _router(x, sc1p, shift, router_w, router_b, tm):
    bsz, s, d = x.shape
    t = min(ROUTE_T, s)
    vec = pl.BlockSpec((1, 1, d), lambda b, i: (b, 0, 0))
    return pl.pallas_call(
        functools.partial(_router_body, tm), grid=(bsz, s // t),
        in_specs=[pl.BlockSpec((1, t, d), lambda b, i: (b, i, 0)), vec, vec,
                  pl.BlockSpec((N_EXPERTS, d), lambda b, i: (0, 0)),
                  pl.BlockSpec((N_EXPERTS, 1), lambda b, i: (0, 0))],
        out_specs=[pl.BlockSpec((1, t, d // 2), lambda b, i: (b, i, 0)),
                   pl.BlockSpec((1, 1, 2 * s), lambda b, i: (b, 0, 0)),
                   pl.BlockSpec((1, s, LANES), lambda b, i: (b, 0, 0)),
                   pl.BlockSpec((1, 4, LANES), lambda b, i: (b, 0, 0))],
        out_shape=[jax.ShapeDtypeStruct((bsz, s, d // 2), jnp.uint32),
                   jax.ShapeDtypeStruct((bsz, 1, 2 * s), jnp.int32),
                   jax.ShapeDtypeStruct((bsz, s, LANES), F32),
                   jax.ShapeDtypeStruct((bsz, 4, LANES), jnp.int32)],
        scratch_shapes=[pltpu.VMEM((N_EXPERTS, s), F32)],
        compiler_params=_cp(("arbitrary", "arbitrary")), name="router",
    )(x, sc1p, shift, router_w.T, router_b.reshape(N_EXPERTS, 1))


LN_ROWS = 256


def _moe_body(alpha, tm, nt, tiles_ref, hp_ref, x_ref, wcol_ref, gt_ref, pos_ref,
              w1_ref, w3_ref, w2_ref, lng_ref, lnb_ref, o_ref, src, xs, ysall, st0, st1):
    b, i = pl.program_id(0), pl.program_id(1)
    s = hp_ref.shape[1]
    tb = b * (4 * LANES)

    @pl.when(i == 0)
    def _():
        for e in range(N_EXPERTS):
            cnt, first = tiles_ref[tb + 2 * LANES + e], tiles_ref[tb + 3 * LANES + e]
            lo = first * tm + cnt
            hi = (first + lax.shift_right_logical(cnt + (tm - 1), int(math.log2(tm)))) * tm

            def pad(p, c):
                src[p] = 0
                return c
            lax.fori_loop(lo, hi, pad, 0)

        def fill(t, c):
            for k in range(2):
                src[pos_ref[0, 0, k * s + t]] = t
            return c
        lax.fori_loop(0, s, fill, 0, unroll=8)

    @pl.when(i < tiles_ref[tb + LANES])
    def _():
        base = pl.multiple_of(i * tm, tm)

        def gather(r, c):
            xs[pl.ds(r, 1), :] = hp_ref[0, pl.ds(src[base + r], 1), :]
            return c
        lax.fori_loop(0, tm, gather, 0, unroll=8)
        xa, xb = _unpack_bf16_pair(xs[...])
        x_t = jnp.concatenate([xa, xb], axis=1).astype(BF16)
        hid = _silu(_dot(x_t, w1_ref[0])) * _dot(x_t, w3_ref[0])
        y = _dot(hid.astype(BF16), w2_ref[0])
        half = y.shape[1] // 2
        ysall[pl.ds(base, tm), :] = _pack_bf16_pair(y[:, :half], y[:, half:])

    @pl.when(i >= nt)
    def _():
        t0 = (i - nt) * LN_ROWS

        def fetch(r, c):
            st0[pl.ds(r, 1), :] = ysall[pl.ds(pos_ref[0, 0, t0 + r], 1), :]
            st1[pl.ds(r, 1), :] = ysall[pl.ds(pos_ref[0, 0, s + t0 + r], 1), :]
            return c
        lax.fori_loop(0, LN_ROWS, fetch, 0, unroll=8)
        a0, b0 = _unpack_bf16_pair(st0[...])
        a1, b1 = _unpack_bf16_pair(st1[...])
        w0, w1 = wcol_ref[0][:, 0:1], wcol_ref[0][:, 1:2]
        y = jnp.concatenate([w0 * a0 + w1 * a1, w0 * b0 + w1 * b1], axis=1)
        o_ref[0] = _layer_norm(alpha * x_ref[0] + gt_ref[0] * y, lng_ref[...], lnb_ref[...])


def _moe(x, gate, hp, pos, wcol, tiles, w1, w3, w2, ln_g, ln_b, alpha, tm):
    bsz, s, d = x.shape
    f = w1.shape[2]
    nt = 2 * s // tm + N_EXPERTS
    assert nt <= LANES and tm & (tm - 1) == 0 and s % LN_ROWS == 0
    blk = lambda i: jnp.maximum(i - nt, 0)
    vec = pl.BlockSpec((1, 1, d), lambda b, i, tl: (b, 0, 0))
    tok = lambda w: pl.BlockSpec((1, LN_ROWS, w), lambda b, i, tl: (b, blk(i), 0))
    row = pl.BlockSpec((1, d), lambda b, i, tl: (0, 0))
    expert = lambda shape: pl.BlockSpec(shape, lambda b, i, tl: (tl[b * (4 * LANES) + jnp.minimum(i, nt - 1)], 0, 0))
    grid_spec = pltpu.PrefetchScalarGridSpec(
        num_scalar_prefetch=1, grid=(bsz, nt + s // LN_ROWS),
        in_specs=[pl.BlockSpec((1, s, d // 2), lambda b, i, tl: (b, 0, 0), pipeline_mode=pl.Buffered(1)),
                  tok(d), tok(LANES), vec,
                  pl.BlockSpec((1, 1, 2 * s), lambda b, i, tl: (b, 0, 0), memory_space=pltpu.SMEM),
                  expert((1, d, f)), expert((1, d, f)), expert((1, f, d)), row, row],
        out_specs=tok(d),
        scratch_shapes=[pltpu.SMEM((nt * tm,), jnp.int32), pltpu.VMEM((tm, d // 2), jnp.uint32),
                        pltpu.VMEM((nt * tm, d // 2), jnp.uint32),
                        pltpu.VMEM((LN_ROWS, d // 2), jnp.uint32), pltpu.VMEM((LN_ROWS, d // 2), jnp.uint32)])
    return pl.pallas_call(
        functools.partial(_moe_body, alpha, tm, nt), grid_spec=grid_spec,
        out_shape=jax.ShapeDtypeStruct((bsz, s, d), F32),
        compiler_params=_cp(("arbitrary", "arbitrary"), VMEM_LIMIT), name="moe_experts",
    )(tiles.reshape(-1), hp, x, wcol, gate, pos, w1, w3, w2, ln_g.reshape(1, d), ln_b.reshape(1, d))


def kernel(x, c, positions, ada_w, ada_b, w_in, gdn_conv_w, gdn_a_log, gdn_dt_bias, gdn_norm_w, ret_gn_w,
           nsa_cmp_pe, nsa_ck_w1, nsa_ck_w2, nsa_cv_w1, nsa_cv_w2, fox_f_bias, branch_proj, w_gate, w_out,
           ln_g, ln_b, router_w, router_b, exp_w1, exp_w3, exp_w2):
    depth, d = w_in.shape[0], x.shape[-1]
    alpha = (2.0 * depth) ** 0.25
    mod = _ada_mod(c, ada_w, ada_b)
    for l in range(depth):
        shift, sc1p, gate = mod[l, 0, :, :, :d], mod[l, 0, :, :, d:2 * d], mod[l, 0, :, :, 2 * d:]
        wcat, wcat_t = _cat_in_weights(w_in[l])
        (gqkv, gz, r, kc, vc, ks, kw, fq, fk, sm,
         smt, nqt, vst, vwt, fvt) = _in_proj(x, sc1p, shift, wcat, wcat_t)
        o_gdn = _gdn(gqkv, sm, smt, gz, gdn_conv_w[l], gdn_a_log[l], gdn_dt_bias[l], gdn_norm_w[l])
        o_ret = _retention(r, positions, ret_gn_w[l])
        k_cmp, v_cmp_t = _nsa_compress(kc, vc, nsa_cmp_pe[l], nsa_ck_w1[l], nsa_ck_w2[l], nsa_cv_w1[l], nsa_cv_w2[l])
        o_nsa = _nsa_attn(nqt, k_cmp, v_cmp_t, ks, vst, kw, vwt, smt)
        o_fox = _fox_attn(fq, _fox_cum(sm, fk, fox_f_bias[l]), fvt)
        x = _merge(x, sc1p, shift, gate, o_gdn, o_ret, o_nsa, o_fox, w_gate[l].astype(BF16),
                   branch_proj[l].astype(BF16), w_out[l].astype(BF16), ln_g[l, 0], ln_b[l, 0], alpha)
        shift, sc1p, gate = mod[l, 1, :, :, :d], mod[l, 1, :, :, d:2 * d], mod[l, 1, :, :, 2 * d:]
        hp, pos, wcol, tiles = _router(x, sc1p, shift, router_w, router_b, MOE_TM)
        x = _moe(x, gate, hp, pos, wcol, tiles, exp_w1[l].astype(BF16), exp_w3[l].astype(BF16),
                 exp_w2[l].astype(BF16), ln_g[l, 1], ln_b[l, 1], alpha, MOE_TM)
    return x
```

```python
import functools
import math

import jax
import jax.numpy as jnp
import numpy as np
from jax import lax
from jax.experimental import pallas as pl
from jax.experimental.pallas import tpu as pltpu

F32 = jnp.float32
BF16 = jnp.bfloat16

N_HEADS = 4
HEAD_DIM = 64
MIX_W = N_HEADS * HEAD_DIM
GDN_CHUNK = 64
CONV_K = 4
RET_CHUNK = 128
ROPE_BASE = 10000.0
NSA_CMP_LEN = 32
NSA_CMP_STRIDE = 16
NSA_SEL_LEN = 64
NSA_TOP_N = 16
NSA_WINDOW = 512
NSA_FORCE_SCORE = 1.0e4
N_EXPERTS = 16
N_GROUPS = 4
EXPERTS_PER_GROUP = N_EXPERTS // N_GROUPS
LN_EPS = 1e-5
LANES = 128
NEG_BIG = -1.0e30
SEL_MASK_BIAS = -30000.0
LOG2E = math.log2(math.e)

IN_WIDTHS = (
    MIX_W, MIX_W, MIX_W, N_HEADS, N_HEADS, MIX_W,
    MIX_W, MIX_W, MIX_W, MIX_W,
    MIX_W, HEAD_DIM, HEAD_DIM, HEAD_DIM, HEAD_DIM, HEAD_DIM, HEAD_DIM, 3 * N_HEADS,
    MIX_W, MIX_W, MIX_W, N_HEADS,
)
_IN_OFF = np.concatenate([[0], np.cumsum(IN_WIDTHS)]).astype(int)
(_GQ, _GK, _GV, _GA, _GB, _GZ, _RQ, _RK, _RV, _RG, _NQ, _NKC, _NVC, _NKS, _NVS, _NKW, _NVW, _NGATE,
 _FQ, _FK, _FV, _FF) = range(22)
SM_A, SM_B, SM_GATE, SM_F = 0, 4, 8, 20

VMEM_LIMIT = 56 * 1024 * 1024


def _cp(sem, vmem=None):
    return pltpu.CompilerParams(dimension_semantics=sem, vmem_limit_bytes=vmem)


def _sigmoid(x):
    return 1.0 / (1.0 + jnp.exp(-x))


def _silu(x):
    return x * _sigmoid(x)


def _softplus(x):
    return jnp.maximum(x, 0.0) + jnp.log1p(jnp.exp(-jnp.abs(x)))


def _dot(a, b):
    return jnp.dot(a, b, preferred_element_type=F32)


def _dot_nt(a, b):
    return lax.dot_general(a, b, (((1,), (1,)), ((), ())), preferred_element_type=F32)


def _split3(a):
    a1 = a.astype(BF16)
    r = a - a1.astype(F32)
    a2 = r.astype(BF16)
    a3 = (r - a2.astype(F32)).astype(BF16)
    return a1, a2, a3


def _dot_x(a, m):
    a1, a2, a3 = _split3(a)
    return _dot(a1, m) + _dot(a2, m) + _dot(a3, m)


def _xdot(m, a):
    a1, a2, a3 = _split3(a)
    return _dot(m, a1) + _dot(m, a2) + _dot(m, a3)


def _dot_hp(a, b):
    a1, a2, _ = _split3(a)
    b1, b2, _ = _split3(b)
    return _dot(a1, b1) + (_dot(a1, b2) + _dot(a2, b1))


def _iota(shape, dim):
    return lax.broadcasted_iota(jnp.int32, shape, dim)


def _div(x, n):
    return lax.shift_right_logical(x, int(math.log2(n)))


def _mod(x, n):
    return x & (n - 1)


def _head_ones():
    return _div(_iota((MIX_W, MIX_W), 0), HEAD_DIM) == _div(_iota((MIX_W, MIX_W), 1), HEAD_DIM)


def _head_sum(x, ones):
    return _dot_x(x, ones)


def _ada_body(c_ref, w_ref, b_ref, o_ref):
    mod = _dot_hp(_silu(c_ref[...]), w_ref[0]) + b_ref[0]
    o_ref[0] = jnp.where(pl.program_id(1) == 1, 1.0 + mod, mod)


def _ada_mod(c, ada_w, ada_b):
    depth = ada_w.shape[0]
    bsz, d = c.shape
    n = depth * 2
    w = ada_w.reshape(n, d, 3 * d)
    b = ada_b.reshape(n, 1, 3 * d)
    out = pl.pallas_call(
        _ada_body, grid=(n, 3),
        in_specs=[pl.BlockSpec((bsz, d), lambda i, j: (0, 0)),
                  pl.BlockSpec((1, d, d), lambda i, j: (i, 0, j)),
                  pl.BlockSpec((1, 1, d), lambda i, j: (i, 0, j))],
        out_specs=pl.BlockSpec((1, bsz, d), lambda i, j: (i, 0, j)),
        out_shape=jax.ShapeDtypeStruct((n, bsz, 3 * d), F32),
        compiler_params=_cp(("arbitrary", "arbitrary")), name="ada_mod",
    )(c, w, b)
    return out.reshape(depth, 2, bsz, 1, 3 * d)


IN_TS = 512
_W_GROUPS = (768, 256, 1024, 256, 512, 128)
_W_OFF = np.concatenate([[0], np.cumsum(_W_GROUPS)]).astype(int)
IN_CAT_W = int(_W_OFF[-1])
_WT_GROUPS = (LANES, MIX_W, HEAD_DIM, HEAD_DIM, MIX_W)
_WT_OFF = np.concatenate([[0], np.cumsum(_WT_GROUPS)]).astype(int)
IN_CAT_T = int(_WT_OFF[-1])


def _cat_in_weights(w_in):
    def col(i):
        return w_in[:, _IN_OFF[i]:_IN_OFF[i + 1]]
    small = jnp.concatenate([col(_GA), col(_GB), col(_NGATE), col(_FF)], axis=1)
    small = jnp.pad(small, ((0, 0), (0, LANES - small.shape[1])))
    scale = HEAD_DIM ** -0.5
    cat = jnp.concatenate([
        col(_GQ), col(_GK), col(_GV), col(_GZ),
        col(_RQ), col(_RK) * scale, col(_RV), col(_RG),
        col(_NKC), col(_NVC), col(_NKS), col(_NKW),
        col(_FQ) * (scale * LOG2E), col(_FK),
        small], axis=1)
    cat_t = jnp.concatenate([small, col(_NQ) * (scale * LOG2E), col(_NVS), col(_NVW), col(_FV)], axis=1).T
    return cat.astype(BF16), cat_t.astype(BF16)


def _inproj_body(x_ref, sc_ref, sh_ref, w_ref, wt_ref,
                 gqkv_ref, gz_ref, r_ref, kc_ref, vc_ref, ks_ref, kw_ref, fq_ref, fk_ref, sm_ref,
                 smt_ref, nqt_ref, vst_ref, vwt_ref, fvt_ref):
    h = (x_ref[0] * sc_ref[0] + sh_ref[0]).astype(BF16)

    def proj(g):
        return _dot(h, w_ref[:, _W_OFF[g]:_W_OFF[g + 1]])

    gqkv_ref[0] = proj(0)
    gz_ref[0] = proj(1)
    r_ref[0] = proj(2)
    nkv = proj(3)
    for i, ref in enumerate((kc_ref, vc_ref, ks_ref, kw_ref)):
        ref[0] = nkv[:, i * HEAD_DIM:(i + 1) * HEAD_DIM].astype(ref.dtype)
    f = proj(4)
    for j, ref in enumerate((fq_ref, fk_ref)):
        for hd in range(N_HEADS):
            lo = j * MIX_W + hd * HEAD_DIM
            ref[0, hd] = f[:, lo:lo + HEAD_DIM].astype(BF16)
    sm_ref[0] = proj(5)
    for g, ref in enumerate((smt_ref, nqt_ref, vst_ref, vwt_ref, fvt_ref)):
        ref[0] = _dot_nt(wt_ref[_WT_OFF[g]:_WT_OFF[g + 1], :], h).astype(ref.dtype)


def _in_proj(x, sc1p, shift, wcat, wcat_t):
    bsz, s, d = x.shape
    ts = min(IN_TS, s)
    tok = lambda w, dt: jax.ShapeDtypeStruct((bsz, s, w), dt)
    chan = lambda w, dt: jax.ShapeDtypeStruct((bsz, w, s), dt)
    hm = jax.ShapeDtypeStruct((bsz, N_HEADS, s, HEAD_DIM), BF16)
    out_shape = [tok(768, F32), tok(256, F32), tok(1024, F32),
                 tok(64, F32), tok(64, F32), tok(64, BF16), tok(64, BF16), hm, hm, tok(LANES, F32),
                 chan(LANES, F32), chan(MIX_W, BF16), chan(HEAD_DIM, BF16), chan(HEAD_DIM, BF16), chan(MIX_W, BF16)]
    tspec = lambda w: pl.BlockSpec((1, ts, w), lambda b, i: (b, i, 0))
    cspec = lambda w: pl.BlockSpec((1, w, ts), lambda b, i: (b, 0, i))
    hspec = pl.BlockSpec((1, N_HEADS, ts, HEAD_DIM), lambda b, i: (b, 0, i, 0))
    out_specs = [tspec(768), tspec(256), tspec(1024),
                 tspec(64), tspec(64), tspec(64), tspec(64), hspec, hspec, tspec(LANES),
                 cspec(LANES), cspec(MIX_W), cspec(HEAD_DIM), cspec(HEAD_DIM), cspec(MIX_W)]
    vec = pl.BlockSpec((1, 1, d), lambda b, i: (b, 0, 0))
    return pl.pallas_call(
        _inproj_body, grid=(bsz, s // ts),
        in_specs=[tspec(d), vec, vec,
                  pl.BlockSpec((d, IN_CAT_W), lambda b, i: (0, 0)),
                  pl.BlockSpec((IN_CAT_T, d), lambda b, i: (0, 0))],
        out_specs=out_specs, out_shape=out_shape,
        compiler_params=_cp(("arbitrary", "arbitrary"), VMEM_LIMIT), name="in_proj",
    )(x, sc1p, shift, wcat, wcat_t)


CUM_T = 512


def _foxcum_body(sm_ref, k_ref, brow_ref, kaug_ref, carry_r):
    @pl.when(pl.program_id(1) == 0)
    def _():
        carry_r[...] = jnp.zeros_like(carry_r)

    t = sm_ref.shape[1]

    def log_sigmoid(v):
        return jnp.minimum(v, 0.0) - jnp.log1p(jnp.exp(-jnp.abs(v)))

    lower = (_iota((t, t), 0) >= _iota((t, t), 1)).astype(BF16)
    cc = _xdot(lower, log_sigmoid(sm_ref[0] + brow_ref[...])) + carry_r[...]
    carry_r[...] = cc[t - 1:t, :]
    parts = _split3(-cc * LOG2E)
    place = (_iota((HEAD_DIM, LANES), 0) == _iota((HEAD_DIM, LANES), 1)).astype(BF16)
    src, dst = _iota((LANES, LANES), 0), _iota((LANES, LANES), 1)
    for hd in range(N_HEADS):
        aug = _dot(k_ref[0, hd], place)
        for j, part in enumerate(parts):
            aug = aug + _dot(part, ((src == SM_F + hd) & (dst == HEAD_DIM + j)).astype(BF16))
        kaug_ref[0, hd] = aug.astype(BF16)


def _fox_cum(small, k, f_bias):
    bsz, s, _ = small.shape
    t = min(CUM_T, s)
    brow = jnp.zeros((1, LANES), F32).at[0, SM_F:SM_F + N_HEADS].set(f_bias)
    return pl.pallas_call(
        _foxcum_body, grid=(bsz, s // t),
        in_specs=[pl.BlockSpec((1, t, LANES), lambda b, i: (b, i, 0)),
                  pl.BlockSpec((1, N_HEADS, t, HEAD_DIM), lambda b, i: (b, 0, i, 0)),
                  pl.BlockSpec((1, LANES), lambda b, i: (0, 0))],
        out_specs=pl.BlockSpec((1, N_HEADS, t, LANES), lambda b, i: (b, 0, i, 0)),
        out_shape=jax.ShapeDtypeStruct((bsz, N_HEADS, s, LANES), BF16),
        scratch_shapes=[pltpu.VMEM((1, LANES), F32)],
        compiler_params=_cp(("arbitrary", "arbitrary")), name="fox_cum",
    )(small, k, brow)


FOX_T = 512


def _lockstep(gens):
    out, live = [None] * len(gens), list(range(len(gens)))
    while live:
        still = []
        for g in live:
            try:
                next(gens[g])
                still.append(g)
            except StopIteration as stop:
                out[g] = stop.value
        live = still
    return out


def _fox_body(q_ref, k_ref, vt_ref, o_ref):
    i = pl.program_id(1)
    t = q_ref.shape[2]
    place = (_iota((HEAD_DIM, LANES), 0) == _iota((HEAD_DIM, LANES), 1)).astype(BF16)
    lane = _iota((1, LANES), 1)
    ones = ((lane >= HEAD_DIM) & (lane < HEAD_DIM + 3)).astype(F32)
    causal = _iota((t, t), 0) <= _iota((t, t), 1)
    q_aug = [(_dot(q_ref[0, hd], place) + ones).astype(BF16) for hd in range(N_HEADS)]

    def head_step(hd, j, carry, masked):
        m, l, acc = carry
        keys = pl.ds(pl.multiple_of(j * t, t), t)
        s = _dot_nt(k_ref[0, hd, keys, :], q_aug[hd])
        yield
        if masked:
            s = jnp.where(causal, s, NEG_BIG)
        m_new = jnp.maximum(m, jnp.max(s, axis=0, keepdims=True))
        yield
        alpha = jnp.exp2(m - m_new)
        p = jnp.exp2(s - m_new)
        yield
        return (m_new, alpha * l + jnp.sum(p, axis=0, keepdims=True),
                alpha * acc + _dot(vt_ref[0, hd * HEAD_DIM:(hd + 1) * HEAD_DIM, keys], p.astype(BF16)))

    def step(j, carries, masked=False):
        return tuple(_lockstep([head_step(hd, j, carries[hd], masked) for hd in range(N_HEADS)]))

    carries = tuple((jnp.full((1, t), NEG_BIG, F32), jnp.zeros((1, t), F32), jnp.zeros((HEAD_DIM, t), F32))
                    for _ in range(N_HEADS))
    carries = lax.fori_loop(0, i, step, carries)
    carries = step(i, carries, masked=True)
    o_ref[0] = jnp.concatenate([acc / l for (_, l, acc) in carries], axis=0).T.astype(o_ref.dtype)


def _fox_attn(q, kaug, vt):
    bsz, nh, s, hd = q.shape
    t = min(FOX_T, s)
    return pl.pallas_call(
        _fox_body, grid=(bsz, s // t),
        in_specs=[pl.BlockSpec((1, nh, t, hd), lambda b, i: (b, 0, i, 0)),
                  pl.BlockSpec((1, nh, s, LANES), lambda b, i: (b, 0, 0, 0)),
                  pl.BlockSpec((1, MIX_W, s), lambda b, i: (b, 0, 0))],
        out_specs=pl.BlockSpec((1, t, MIX_W), lambda b, i: (b, i, 0)),
        out_shape=jax.ShapeDtypeStruct((bsz, s, MIX_W), BF16),
        compiler_params=_cp(("arbitrary", "arbitrary")), name="fox_attn",
    )(q, kaug, vt)


GDN_T = 256
GDN_NB = 2


def _gdn_body(qkv_ref, sm_ref, smt_ref, z_ref, cw_ref, alr_ref, dtr_ref, alc_ref, dtc_ref, nw_ref,
              o_ref, xbuf, state):
    t, c = GDN_T, GDN_CHUNK
    ri, ci = _iota((t, t), 0), _iota((t, t), 1)
    same = _div(ri, c) == _div(ci, c)
    er, ec = _iota((LANES, MIX_W), 0), _div(_iota((LANES, MIX_W), 1), HEAD_DIM)
    head_blk = _head_ones()
    masks = dict(same=same, incl=same & (ci <= ri), strict=same & (ci < ri),
                 same_b=same.astype(BF16), incl_b=(same & (ci <= ri)).astype(BF16),
                 incl_t_b=(same & (ri <= ci)).astype(BF16),
                 exp_a=(er == ec + SM_A).astype(BF16),
                 exp_b=(er == ec + SM_B).astype(BF16),
                 head_blk=head_blk, head_ones=head_blk.astype(BF16),
                 lane_head=_div(_iota((1, MIX_W), 1), HEAD_DIM))
    _lockstep([_gdn_tile(masks, qkv_ref.at[bb], sm_ref.at[bb], smt_ref.at[bb], z_ref.at[bb], cw_ref, alr_ref,
                         dtr_ref, alc_ref, dtc_ref, nw_ref, o_ref.at[bb], xbuf.at[bb], state.at[bb])
               for bb in range(qkv_ref.shape[0])])


def _gdn_tile(masks, qkv_ref, sm_ref, smt_ref, z_ref, cw_ref, alr_ref, dtr_ref, alc_ref, dtc_ref, nw_ref,
              o_ref, xbuf, state):
    t, c = GDN_T, GDN_CHUNK
    incl, strict, head_ones = masks["incl"], masks["strict"], masks["head_ones"]

    @pl.when(pl.program_id(1) == 0)
    def _():
        xbuf[0:8, :] = jnp.zeros((8, 3 * MIX_W), F32)
        state[...] = jnp.zeros_like(state)

    xbuf[8:8 + t, :] = qkv_ref[...]
    conv = cw_ref[0:1, :] * xbuf[pl.ds(8 - CONV_K + 1, t), :]
    for kk in range(1, CONV_K):
        conv = conv + cw_ref[kk:kk + 1, :] * xbuf[pl.ds(8 - CONV_K + 1 + kk, t), :]
    xbuf[0:8, :] = xbuf[t:t + 8, :]
    act = _silu(conv)
    q, k, v = act[:, :MIX_W], act[:, MIX_W:2 * MIX_W], act[:, 2 * MIX_W:]
    q = q * lax.rsqrt(_head_sum(q * q, head_ones) + 1e-6) * HEAD_DIM ** -0.5
    k = k * lax.rsqrt(_head_sum(k * k, head_ones) + 1e-6)

    sm, smt = sm_ref[...], smt_ref[...]
    la_col = -jnp.exp(alr_ref[...]) * _softplus(sm + dtr_ref[...])
    la_row = -jnp.exp(alc_ref[...]) * _softplus(smt + dtc_ref[...])
    beta_col = _sigmoid(sm)
    bc_col = _xdot(masks["incl_b"], la_col)
    last_col = _xdot(masks["same_b"], la_col)
    bc_row = _dot_x(la_row, masks["incl_t_b"])
    exp_a, exp_b = masks["exp_a"], masks["exp_b"]
    bc_x, last_x, beta_x = _dot_x(bc_col, exp_a), _dot_x(last_col, exp_a), _dot_x(beta_col, exp_b)
    eb = jnp.exp(bc_x)
    q_dec = q * eb
    k_dec = k * jnp.exp(last_x - bc_x)
    rhs_v = beta_x * v
    rhs_k = beta_x * eb * k
    rhs = jnp.concatenate([rhs_v, rhs_k], axis=1).astype(BF16)

    kb = k.astype(BF16)
    lane_head = masks["lane_head"]
    heads = range(N_HEADS)
    mh = [lane_head == hd for hd in heads]
    gram = [_dot_nt(jnp.where(mh[hd], k, 0.0).astype(BF16), kb) for hd in heads]
    qk = [_dot_nt(jnp.where(mh[hd], q, 0.0).astype(BF16), kb) for hd in heads]
    yield
    dec = [jnp.exp(jnp.minimum(bc_col[:, SM_A + hd:SM_A + hd + 1] - bc_row[SM_A + hd:SM_A + hd + 1, :], 0.0))
           for hd in heads]
    attn = [jnp.where(incl, qk[hd] * dec[hd], 0.0) for hd in heads]
    p = [jnp.where(strict, -(beta_col[:, SM_B + hd:SM_B + hd + 1] * gram[hd] * dec[hd]), 0.0) for hd in heads]
    tr = list(p)
    for _ in range(5):
        pb = [p[hd].astype(BF16) for hd in heads]
        p = [_dot(pb[hd], pb[hd]) for hd in heads]
        yield
        tr = [tr[hd] + p[hd] + _dot(tr[hd].astype(BF16), p[hd].astype(BF16)) for hd in heads]
    sol = [_dot(tr[hd].astype(BF16), rhs) for hd in heads]
    yield
    u0, w = rhs_v, rhs_k
    for hd in heads:
        u0 = u0 + jnp.where(mh[hd], sol[hd][:, :MIX_W], 0.0)
        w = w + jnp.where(mh[hd], sol[hd][:, MIX_W:], 0.0)

    head_blk = masks["head_blk"]
    k_dec_t = k_dec.T.astype(BF16)
    outs = []
    for n in range(t // c):
        rows = slice(n * c, (n + 1) * c)
        s_old = state[...]
        sb = s_old.astype(BF16)
        u_n = u0[rows] - _dot(w[rows].astype(BF16), sb)
        yield
        parts = ([jnp.zeros((n * c, MIX_W), F32)] if n else []) + [u_n]
        if t - (n + 1) * c:
            parts.append(jnp.zeros((t - (n + 1) * c, MIX_W), F32))
        u_full = jnp.concatenate(parts, axis=0).astype(BF16)
        oa = _dot(jnp.concatenate([a[rows] for a in attn], axis=0).astype(BF16), u_full)
        o_n = _dot(q_dec[rows].astype(BF16), sb)
        for hd in range(N_HEADS):
            o_n = o_n + jnp.where(lane_head == hd, oa[hd * c:(hd + 1) * c], 0.0)
        outs.append(o_n)
        g_row = jnp.exp(last_x[n * c:n * c + 1, :])
        state[...] = g_row * s_old + jnp.where(head_blk, _dot(k_dec_t, u_full), 0.0)
        yield
    o = jnp.concatenate(outs, axis=0)
    o = o * lax.rsqrt(_head_sum(o * o, head_ones) * (1.0 / HEAD_DIM) + 1e-6) * nw_ref[...]
    o_ref[...] = (o * _silu(z_ref[...])).astype(o_ref.dtype)


def _lane_vec(vals, off):
    return jnp.zeros((1, LANES), F32).at[0, off:off + vals.shape[0]].set(vals)


def _gdn(qkv, small, small_t, z, conv_w, a_log, dt_bias, norm_w):
    bsz, s, _ = qkv.shape
    t = GDN_T
    alr, dtr = _lane_vec(a_log, SM_A), _lane_vec(dt_bias, SM_A)
    nw = jnp.tile(norm_w, N_HEADS).reshape(1, MIX_W)
    const = lambda shape: pl.BlockSpec(shape, lambda b, i: (0,) * len(shape))
    nb = GDN_NB if bsz % GDN_NB == 0 else 1
    return pl.pallas_call(
        _gdn_body, grid=(bsz // nb, s // t),
        in_specs=[pl.BlockSpec((nb, t, 3 * MIX_W), lambda b, i: (b, i, 0)),
                  pl.BlockSpec((nb, t, LANES), lambda b, i: (b, i, 0)),
                  pl.BlockSpec((nb, LANES, t), lambda b, i: (b, 0, i)),
                  pl.BlockSpec((nb, t, MIX_W), lambda b, i: (b, i, 0)),
                  const((CONV_K, 3 * MIX_W)), const((1, LANES)), const((1, LANES)),
                  const((LANES, 1)), const((LANES, 1)), const((1, MIX_W))],
        out_specs=pl.BlockSpec((nb, t, MIX_W), lambda b, i: (b, i, 0)),
        out_shape=jax.ShapeDtypeStruct((bsz, s, MIX_W), BF16),
        scratch_shapes=[pltpu.VMEM((nb, t + 8, 3 * MIX_W), F32), pltpu.VMEM((nb, MIX_W, MIX_W), F32)],
        compiler_params=_cp(("arbitrary", "arbitrary")), name="gdn",
    )(qkv, small, small_t, z, conv_w, alr, dtr, alr.reshape(LANES, 1), dtr.reshape(LANES, 1), nw)


RET_T = 512
_RET_LOG_GAMMA = tuple(math.log1p(-(2.0 ** (-5.0 - h))) for h in range(N_HEADS))


def _per_head(lane_head, vals):
    out = jnp.full(lane_head.shape, vals[0], F32)
    for hd in range(1, N_HEADS):
        out = jnp.where(lane_head == hd, vals[hd], out)
    return out


def _ret_body(r_ref, pos_ref, gnw_ref, o_ref, state):
    t, c = r_ref.shape[1], RET_CHUNK

    @pl.when(pl.program_id(1) == 0)
    def _():
        state[...] = jnp.zeros_like(state)

    x = r_ref[0]
    lane = _iota((1, MIX_W), 1)
    lane_head = _div(lane, HEAD_DIM)
    half = HEAD_DIM // 2
    inv_freq = jnp.exp(_mod(_iota((1, LANES), 1), half).astype(F32) * (-math.log(ROPE_BASE) / half))
    ang = pos_ref[0].astype(F32) * inv_freq
    cos, sin = jnp.cos(ang), jnp.sin(ang)
    cos, sin = jnp.concatenate([cos, cos], axis=1), jnp.concatenate([sin, sin], axis=1)
    first = _mod(lane, HEAD_DIM) < half

    def rope(a):
        rot = jnp.where(first, -pltpu.roll(a, MIX_W - half, 1), pltpu.roll(a, half, 1))
        return a * cos + rot * sin

    q, k = rope(x[:, :MIX_W]), rope(x[:, MIX_W:2 * MIX_W])
    v, g = x[:, 2 * MIX_W:3 * MIX_W], x[:, 3 * MIX_W:]

    lg = _per_head(lane_head, _RET_LOG_GAMMA)
    cidx = _iota((c, 1), 0).astype(F32)
    xi = jnp.exp((cidx + 1.0) * lg)
    zeta = jnp.exp((c - 1.0 - cidx) * lg)
    chunk_decay = jnp.exp(float(c) * lg)
    rel = (_iota((c, c), 0) - _iota((c, c), 1)).astype(F32)
    dec = jnp.concatenate([jnp.where(rel >= 0, jnp.exp(jnp.maximum(rel, 0.0) * _RET_LOG_GAMMA[hd]), 0.0)
                           for hd in range(N_HEADS)], axis=0)
    head_blk = _head_ones()
    outs = []
    for n in range(t // c):
        rows = slice(n * c, (n + 1) * c)
        qn, kn, vn = q[rows], k[rows], v[rows].astype(BF16)
        qs = jnp.concatenate([jnp.where(lane_head == hd, qn, 0.0) for hd in range(N_HEADS)], axis=0).astype(BF16)
        sc = _dot_nt(qs, kn.astype(BF16)) * dec
        res = _dot(sc.astype(BF16), vn)
        s_old = state[...]
        o_n = _dot(qn.astype(BF16), s_old.astype(BF16)) * xi
        for hd in range(N_HEADS):
            o_n = o_n + jnp.where(lane_head == hd, res[hd * c:(hd + 1) * c], 0.0)
        outs.append(o_n)
        kv = _dot((kn * zeta).T.astype(BF16), vn)
        state[...] = chunk_decay * s_old + jnp.where(head_blk, kv, 0.0)
    o = jnp.concatenate(outs, axis=0)
    head_ones = head_blk.astype(BF16)
    mu = _head_sum(o, head_ones) * (1.0 / HEAD_DIM)
    xc = o - mu
    var = _head_sum(xc * xc, head_ones) * (1.0 / HEAD_DIM)
    y = xc * lax.rsqrt(var + LN_EPS) * gnw_ref[...]
    o_ref[0] = (_silu(g) * y).astype(o_ref.dtype)


def _retention(r, positions, gn_w):
    bsz, s, _ = r.shape
    t = min(RET_T, s)
    return pl.pallas_call(
        _ret_body, grid=(bsz, s // t),
        in_specs=[pl.BlockSpec((1, t, 4 * MIX_W), lambda b, i: (b, i, 0)),
                  pl.BlockSpec((1, t, 1), lambda b, i: (b, i, 0)),
                  pl.BlockSpec((1, MIX_W), lambda b, i: (0, 0))],
        out_specs=pl.BlockSpec((1, t, MIX_W), lambda b, i: (b, i, 0)),
        out_shape=jax.ShapeDtypeStruct((bsz, s, MIX_W), BF16),
        scratch_shapes=[pltpu.VMEM((MIX_W, MIX_W), F32)],
        compiler_params=_cp(("arbitrary", "arbitrary")), name="retention",
    )(r, positions.reshape(bsz, s, 1), gn_w.reshape(1, MIX_W))


def _nsacmp_body(kc_ref, vc_ref, pe_ref, kw1_ref, kw2_ref, vw1_ref, vw2_ref, ko_ref, vo_ref):
    half = NSA_CMP_STRIDE * HEAD_DIM
    nc = kc_ref.shape[1]
    def hidden(x_ref, w1_ref):
        x = x_ref[0]
        top = _dot_hp(x, w1_ref[0:half, :])
        bot = _dot_hp(x, w1_ref[half:2 * half, :])
        bias = _dot_hp(pe_ref[...], w1_ref[...])[0:1, :]
        return _silu(top + pltpu.roll(bot, nc - 1, 0) + bias)

    ko_ref[0] = _dot_hp(hidden(kc_ref, kw1_ref), kw2_ref[...])
    a1, a2, _ = _split3(vw2_ref[...])
    b1, b2, _ = _split3(hidden(vc_ref, vw1_ref))
    vo_ref[0] = _dot_nt(a1, b1) + (_dot_nt(a1, b2) + _dot_nt(a2, b1))


def _nsa_compress(kc, vc, pe, ck_w1, ck_w2, cv_w1, cv_w2):
    bsz, s, hd = kc.shape
    nc = s // NSA_CMP_STRIDE
    wide = NSA_CMP_STRIDE * hd
    pe8 = jnp.broadcast_to(pe.reshape(1, NSA_CMP_LEN * hd), (8, NSA_CMP_LEN * hd))
    cv_w2t = cv_w2.T
    const = lambda a: pl.BlockSpec(a.shape, lambda b: (0,) * a.ndim)
    xspec = pl.BlockSpec((1, nc, wide), lambda b: (b, 0, 0))
    return pl.pallas_call(
        _nsacmp_body, grid=(bsz,),
        in_specs=[xspec, xspec, const(pe8), const(ck_w1), const(ck_w2), const(cv_w1), const(cv_w2t)],
        out_specs=[pl.BlockSpec((1, nc, hd), lambda b: (b, 0, 0)), pl.BlockSpec((1, hd, nc), lambda b: (b, 0, 0))],
        out_shape=[jax.ShapeDtypeStruct((bsz, nc, hd), F32), jax.ShapeDtypeStruct((bsz, hd, nc), F32)],
        compiler_params=_cp(("arbitrary",)), name="nsa_compress",
    )(kc.reshape(bsz, nc, wide), vc.reshape(bsz, nc, wide), pe8, ck_w1, ck_w2, cv_w1, cv_w2t)


NSA_TQ = 128
NSA_TK = 512
NSA_UNROLL = 2


def _softmax_cols(s, valid):
    m = jnp.max(jnp.where(valid, s, NEG_BIG), axis=0, keepdims=True)
    m = jnp.where(m > 0.5 * NEG_BIG, m, 0.0)
    e = jnp.where(valid, jnp.exp2(s - m), 0.0)
    den = jnp.sum(e, axis=0, keepdims=True)
    return e / jnp.where(den > 0.0, den, 1.0)


def _nsa_body(qt_ref, kc_ref, vct_ref, ks_ref, vst_ref, kw_ref, vwt_ref, smt_ref, o_ref, kaug):
    i = pl.program_id(1)
    tq, hd = qt_ref.shape[2], HEAD_DIM
    s, nc = ks_ref.shape[1], kc_ref.shape[1]
    nsel = s // NSA_SEL_LEN
    n_top = min(NSA_TOP_N, nsel)
    cols = N_HEADS * tq

    @pl.when(i == 0)
    def _():
        place = (_iota((hd, LANES), 0) == _iota((hd, LANES), 1)).astype(BF16)
        onehot = (_iota((s, LANES), 1) - hd) == _div(_iota((s, LANES), 0), NSA_SEL_LEN)
        kaug[...] = (_dot(ks_ref[0], place) + onehot.astype(F32)).astype(BF16)

    t0 = i * tq
    qt = qt_ref[0]
    q4 = jnp.concatenate([qt[h * hd:(h + 1) * hd, :] for h in range(N_HEADS)], axis=1)
    tpos = t0 + _iota((1, tq), 1)
    tpos4 = t0 + _mod(_iota((1, cols), 1), tq)

    def cmp_branch():
        sc = _dot(kc_ref[0].astype(BF16), q4)
        yield
        cmp_end = _iota((nc, 1), 0) * NSA_CMP_STRIDE + (NSA_CMP_LEN - 1)
        p = _softmax_cols(sc, cmp_end <= tpos4)
        yield
        return p, _dot(vct_ref[0].astype(BF16), p.astype(BF16))

    def win_branch():
        wlen = NSA_WINDOW + tq
        start = pl.multiple_of(jnp.maximum(t0 - NSA_WINDOW, 0), tq)
        sc = _dot(kw_ref[0, pl.ds(start, wlen), :], q4)
        yield
        kpos = start + _iota((wlen, 1), 0)
        p = _softmax_cols(sc, (kpos <= tpos4) & (kpos > tpos4 - NSA_WINDOW))
        yield
        return _dot(vwt_ref[0, :, pl.ds(start, wlen)], p.astype(BF16))

    (p_cmp, o_cmp), o_win = _lockstep([cmp_branch(), win_branch()])

    psum = p_cmp[:, 0:tq] + p_cmp[:, tq:2 * tq] + p_cmp[:, 2 * tq:3 * tq] + p_cmp[:, 3 * tq:4 * tq]
    ss, cs = _iota((hd, nc), 0) * NSA_SEL_LEN, _iota((hd, nc), 1) * NSA_CMP_STRIDE
    overlap = jnp.clip(jnp.minimum(cs + NSA_CMP_LEN, ss + NSA_SEL_LEN) - jnp.maximum(cs, ss), 0, NSA_CMP_LEN)
    importance = _xdot((overlap.astype(F32) * (1.0 / NSA_CMP_LEN)).astype(BF16), psum)
    blk = _iota((hd, 1), 0)
    cur = _div(tpos, NSA_SEL_LEN)
    forced = (blk == 0) | (blk == cur) | (blk == cur - 1)
    score = jnp.where(forced, NSA_FORCE_SCORE, jnp.where(blk * NSA_SEL_LEN <= tpos, importance, -1.0))
    score = jnp.where(blk < nsel, score, -2.0)
    groups = [score[8 * g:8 * g + 8, :] for g in range(hd // 8)]
    gblk = _iota((8, 1), 0)
    cnts = [jnp.zeros((8, tq), jnp.int32) for _ in groups]
    for j in range(nsel):
        row = score[j:j + 1, :]
        for g in range(hd // 8):
            if 8 * g > j:
                ahead = row >= groups[g]
            elif 8 * g + 7 <= j:
                ahead = row > groups[g]
            else:
                ahead = (row > groups[g]) | ((row == groups[g]) & (gblk + 8 * g > j))
            cnts[g] = cnts[g] + ahead.astype(jnp.int32)
    bias = jnp.where(jnp.concatenate(cnts, axis=0) < n_top, 0.0, SEL_MASK_BIAS).astype(BF16)
    q_aug = jnp.concatenate([q4, jnp.concatenate([bias] * N_HEADS, axis=1)], axis=0)

    def sel_step(c, carry, masked=False):
        m, l, acc = carry
        keys = pl.ds(pl.multiple_of(c * NSA_TK, NSA_TK), NSA_TK)
        sc = _dot(kaug[keys, :], q_aug)
        if masked:
            sc = jnp.where(c * NSA_TK + _iota((NSA_TK, 1), 0) <= tpos4, sc, NEG_BIG)
        m_new = jnp.maximum(m, jnp.max(sc, axis=0, keepdims=True))
        alpha = jnp.exp2(m - m_new)
        p = jnp.exp2(sc - m_new)
        return (m_new, alpha * l + jnp.sum(p, axis=0, keepdims=True),
                alpha * acc + _dot(vst_ref[0, :, keys], p.astype(BF16)))

    def sel_group(g, carry):
        for u in range(NSA_UNROLL):
            carry = sel_step(g * NSA_UNROLL + u, carry)
        return carry

    n_full = _div(t0, NSA_TK)
    n_groups = _div(n_full, NSA_UNROLL)
    carry = (jnp.full((1, cols), NEG_BIG, F32), jnp.zeros((1, cols), F32), jnp.zeros((hd, cols), F32))
    carry = lax.fori_loop(0, n_groups, sel_group, carry)
    carry = lax.fori_loop(n_groups * NSA_UNROLL, n_full, sel_step, carry)
    _, l_sel, acc_sel = sel_step(n_full, carry, masked=True)
    o_sel = acc_sel / l_sel

    gates =_sigmoid(smt_ref[0][SM_GATE:SM_GATE + 16, :])
    outs = []
    for h in range(N_HEADS):
        c = slice(h * tq, (h + 1) * tq)
        outs.append(gates[3 * h:3 * h + 1, :] * o_cmp[:, c] + gates[3 * h + 1:3 * h + 2, :] * o_sel[:, c]
                    + gates[3 * h + 2:3 * h + 3, :] * o_win[:, c])
    o_ref[0] = jnp.concatenate(outs, axis=0).T.astype(o_ref.dtype)


def _nsa_attn(qt, k_cmp, v_cmp_t, ks, vst, kw, vwt, small_t):
    bsz, _, s = qt.shape
    tq, hd = NSA_TQ, HEAD_DIM
    assert s % NSA_TK == 0 and NSA_TK % tq == 0 and s >= NSA_WINDOW + tq and s // NSA_SEL_LEN <= hd
    nc = k_cmp.shape[1]
    keys = pl.BlockSpec((1, s, hd), lambda b, i: (b, 0, 0))
    vals = pl.BlockSpec((1, hd, s), lambda b, i: (b, 0, 0))
    return pl.pallas_call(
        _nsa_body, grid=(bsz, s // tq),
        in_specs=[pl.BlockSpec((1, MIX_W, tq), lambda b, i: (b, 0, i)),
                  pl.BlockSpec((1, nc, hd), lambda b, i: (b, 0, 0)),
                  pl.BlockSpec((1, hd, nc), lambda b, i: (b, 0, 0)),
                  keys, vals, keys, vals,
                  pl.BlockSpec((1, LANES, tq), lambda b, i: (b, 0, i))],
        out_specs=pl.BlockSpec((1, tq, MIX_W), lambda b, i: (b, i, 0)),
        out_shape=jax.ShapeDtypeStruct((bsz, s, MIX_W), BF16),
        scratch_shapes=[pltpu.VMEM((s, LANES), BF16)],
        compiler_params=_cp(("arbitrary", "arbitrary")), name="nsa_attn",
    )(qt, k_cmp, v_cmp_t, ks, vst, kw, vwt, small_t)


MERGE_T = 512


def _layer_norm(r, g, b):
    mu = jnp.mean(r, axis=-1, keepdims=True)
    xc = r - mu
    var = jnp.mean(xc * xc, axis=-1, keepdims=True)
    return xc * lax.rsqrt(var + LN_EPS) * g + b


def _merge_body(alpha, x_ref, sc_ref, sh_ref, gt_ref, og_ref, or_ref, on_ref, of_ref, wg_ref, bp_ref, wo_ref,
                lng_ref, lnb_ref, o_ref):
    x = x_ref[0]
    h = (x * sc_ref[0] + sh_ref[0]).astype(BF16)

    merged = None
    for br, o_br in enumerate((og_ref, or_ref, on_ref, of_ref)):
        term = _sigmoid(_dot(h, wg_ref[br])) * _dot(o_br[0], bp_ref[br])
        merged = term if merged is None else merged + term
    y = _dot(merged.astype(BF16), wo_ref[...])
    o_ref[0] = _layer_norm(alpha * x + gt_ref[0] * y, lng_ref[...], lnb_ref[...])


def _merge(x, sc1p, shift, gate, o_gdn, o_ret, o_nsa, o_fox, w_gate, branch_proj, w_out, ln_g, ln_b, alpha):
    bsz, s, d = x.shape
    t = min(MERGE_T, s)
    tok = lambda w: pl.BlockSpec((1, t, w), lambda b, i: (b, i, 0))
    vec = pl.BlockSpec((1, 1, d), lambda b, i: (b, 0, 0))
    const = lambda a: pl.BlockSpec(a.shape, lambda b, i: (0,) * a.ndim, pipeline_mode=pl.Buffered(1))
    lng, lnb = ln_g.reshape(1, d), ln_b.reshape(1, d)
    return pl.pallas_call(
        functools.partial(_merge_body, alpha), grid=(bsz, s // t),
        in_specs=[tok(d), vec, vec, vec, tok(MIX_W), tok(MIX_W), tok(MIX_W), tok(MIX_W),
                  const(w_gate), const(branch_proj), const(w_out), const(lng), const(lnb)],
        out_specs=tok(d), out_shape=jax.ShapeDtypeStruct((bsz, s, d), F32),
        compiler_params=_cp(("arbitrary", "arbitrary"), VMEM_LIMIT), name="merge",
    )(x, sc1p, shift, gate, o_gdn, o_ret, o_nsa, o_fox, w_gate, branch_proj, w_out, lng, lnb)


MOE_TM = 256
ROUTE_T = 512


def _pack_bf16_pair(a, b):
    hi = lax.bitcast_convert_type(a.astype(BF16).astype(F32), jnp.uint32)
    lo = lax.bitcast_convert_type(b.astype(BF16).astype(F32), jnp.uint32)
    return hi | lax.shift_right_logical(lo, jnp.uint32(16))


def _unpack_bf16_pair(w):
    a = lax.bitcast_convert_type(w & jnp.uint32(0xFFFF0000), F32)
    b = lax.bitcast_convert_type(lax.shift_left(w, jnp.uint32(16)), F32)
    return a, b


def _router_body(tm, x_ref, sc_ref, sh_ref, rwt_ref, rb_ref, hp_ref, pos_ref, wcol_ref, tile_ref, lg):
    i = pl.program_id(1)
    t = x_ref.shape[1]
    s = lg.shape[1]
    h = x_ref[0] * sc_ref[0] + sh_ref[0]
    half = h.shape[1] // 2
    hp_ref[0] = _pack_bf16_pair(h[:, :half], h[:, half:])
    a1, a2, _ = _split3(rwt_ref[...])
    b1, b2, _ = _split3(h)
    lg[:, pl.ds(pl.multiple_of(i * t, t), t)] = _dot_nt(a1, b1) + (_dot_nt(a1, b2) + _dot_nt(a2, b1))

    @pl.when(i == pl.num_programs(1) - 1)
    def _():
        scores = _sigmoid(lg[...])
        biased = scores + rb_ref[...]
        b = [biased[e:e + 1, :] for e in range(N_EXPERTS)]
        sc = [scores[e:e + 1, :] for e in range(N_EXPERTS)]
        gs = []
        for g in range(N_GROUPS):
            m = [b[EXPERTS_PER_GROUP * g + j] for j in range(EXPERTS_PER_GROUP)]
            best = m[0] + m[1]
            for u in range(EXPERTS_PER_GROUP):
                for v in range(u + 1, EXPERTS_PER_GROUP):
                    if (u, v) != (0, 1):
                        best = jnp.maximum(best, m[u] + m[v])
            gs.append(best)
        gsel, best = jnp.zeros((1, s), jnp.int32), gs[0]
        for g in range(1, N_GROUPS):
            take = gs[g] > best
            gsel = jnp.where(take, g, gsel)
            best = jnp.where(take, gs[g], best)
        first, second = [], []
        for e in range(N_EXPERTS):
            g = e // EXPERTS_PER_GROUP
            cnt = jnp.zeros((1, s), jnp.int32)
            for j in range(EXPERTS_PER_GROUP * g, EXPERTS_PER_GROUP * (g + 1)):
                if j != e:
                    ahead = (b[j] >= b[e]) if j < e else (b[j] > b[e])
                    cnt = cnt + ahead.astype(jnp.int32)
            first.append((gsel == g) & (cnt == 0))
            second.append((gsel == g) & (cnt == 1))
        s0 = sum(jnp.where(first[e], sc[e], 0.0) for e in range(N_EXPERTS))
        s1 = sum(jnp.where(second[e], sc[e], 0.0) for e in range(N_EXPERTS))
        den = s0 + s1
        w0, w1 = s0 / den, s1 / den
        blk = min(512, s)
        eye = (_iota((blk, blk), 0) == _iota((blk, blk), 1)).astype(BF16)
        rowid = _iota((LANES, blk), 0)
        for j in range(s // blk):
            cols = slice(j * blk, (j + 1) * blk)
            wb = jnp.where(rowid == 0, w0[:, cols], jnp.where(rowid == 1, w1[:, cols], 0.0))
            wcol_ref[0, cols, :] = sum(_dot_nt(eye, part) for part in _split3(wb))
        onehot = jnp.concatenate([(first[e] | second[e]).astype(F32) for e in range(N_EXPERTS)], axis=0)
        before = (_iota((blk, blk), 0) < _iota((blk, blk), 1)).astype(BF16)
        carry, ranks = jnp.zeros((N_EXPERTS, 1), F32), []
        for j in range(s // blk):
            ob = onehot[:, j * blk:(j + 1) * blk]
            ranks.append(_dot(ob.astype(BF16), before) + carry)
            carry = carry + jnp.sum(ob, axis=1, keepdims=True)
        rank = jnp.concatenate(ranks, axis=1)
        ntile = jnp.floor((carry + (tm - 1.0)) * (1.0 / tm))
        lower = (_iota((N_EXPERTS, N_EXPERTS), 1) < _iota((N_EXPERTS, N_EXPERTS), 0)).astype(BF16)
        toff = _dot(lower, jnp.broadcast_to(ntile, (N_EXPERTS, LANES)).astype(BF16))[:, 0:1]
        slot = toff * float(tm) + rank
        pos0 = sum(jnp.where(first[e], slot[e:e + 1, :], 0.0) for e in range(N_EXPERTS))
        pos1 = sum(jnp.where(second[e], slot[e:e + 1, :], 0.0) for e in range(N_EXPERTS))
        pos_ref[0] = jnp.concatenate([pos0, pos1], axis=1).astype(jnp.int32)
        tend = toff + ntile
        tid = _iota((1, LANES), 1).astype(F32)
        texp = jnp.sum((tend <= tid).astype(F32), axis=0, keepdims=True)
        ntot = jnp.broadcast_to(jnp.sum(ntile, axis=0, keepdims=True), (1, LANES))
        diag = _iota((N_EXPERTS, LANES), 0) == _iota((N_EXPERTS, LANES), 1)
        to_lanes = lambda col: jnp.sum(jnp.where(diag, col, 0.0), axis=0, keepdims=True)
        tile_ref[0] = jnp.concatenate([jnp.minimum(texp, N_EXPERTS - 1.0), ntot, to_lanes(carry), to_lanes(toff)],
                                      axis=0).astype(jnp.int32)


def _router(x, sc1p, shift, router_w, router_b, tm):
    bsz, s, d = x.shape
    t = min(ROUTE_T, s)
    vec = pl.BlockSpec((1, 1, d), lambda b, i: (b, 0, 0))
    return pl.pallas_call(
        functools.partial(_router_body, tm), grid=(bsz, s // t),
        in_specs=[pl.BlockSpec((1, t, d), lambda b, i: (b, i, 0)), vec, vec,
                  pl.BlockSpec((N_EXPERTS, d), lambda b, i: (0, 0)),
                  pl.BlockSpec((N_EXPERTS, 1), lambda b, i: (0, 0))],
        out_specs=[pl.BlockSpec((1, t, d // 2), lambda b, i: (b, i, 0)),
                   pl.BlockSpec((1, 1, 2 * s), lambda b, i: (b, 0, 0)),
                   pl.BlockSpec((1, s, LANES), lambda b, i: (b, 0, 0)),
                   pl.BlockSpec((1, 4, LANES), lambda b, i: (b, 0, 0))],
        out_shape=[jax.ShapeDtypeStruct((bsz, s, d // 2), jnp.uint32),
                   jax.ShapeDtypeStruct((bsz, 1, 2 * s), jnp.int32),
                   jax.ShapeDtypeStruct((bsz, s, LANES), F32),
                   jax.ShapeDtypeStruct((bsz, 4, LANES), jnp.int32)],
        scratch_shapes=[pltpu.VMEM((N_EXPERTS, s), F32)],
        compiler_params=_cp(("arbitrary", "arbitrary")), name="router",
    )(x, sc1p, shift, router_w.T, router_b.reshape(N_EXPERTS, 1))


LN_ROWS = 512


def _moe_body(alpha, tm, nt, tiles_ref, hp_ref, x_ref, wcol_ref, gt_ref, pos_ref,
              w1_ref, w3_ref, w2_ref, lng_ref, lnb_ref, o_ref, src, xs, ysall, st0, st1):
    b, i = pl.program_id(0), pl.program_id(1)
    s = hp_ref.shape[1]
    tb = b * (4 * LANES)

    @pl.when(i == 0)
    def _():
        for e in range(N_EXPERTS):
            cnt, first = tiles_ref[tb + 2 * LANES + e], tiles_ref[tb + 3 * LANES + e]
            lo = first * tm + cnt
            hi = (first + lax.shift_right_logical(cnt + (tm - 1), int(math.log2(tm)))) * tm

            def pad(p, c):
                src[p] = 0
                return c
            lax.fori_loop(lo, hi, pad, 0)

        def fill(t, c):
            for k in range(2):
                src[pos_ref[0, 0, k * s + t]] = t
            return c
        lax.fori_loop(0, s, fill, 0, unroll=8)

    @pl.when(i < tiles_ref[tb + LANES])
    def _():
        base = pl.multiple_of(i * tm, tm)

        def gather(r, c):
            xs[pl.ds(r, 1), :] = hp_ref[0, pl.ds(src[base + r], 1), :]
            return c
        lax.fori_loop(0, tm, gather, 0, unroll=True)
        xa, xb = _unpack_bf16_pair(xs[...])
        x_t = jnp.concatenate([xa, xb], axis=1).astype(BF16)
        hid = _silu(_dot(x_t, w1_ref[0])) * _dot(x_t, w3_ref[0])
        y = _dot(hid.astype(BF16), w2_ref[0])
        half = y.shape[1] // 2
        ysall[pl.ds(base, tm), :] = _pack_bf16_pair(y[:, :half], y[:, half:])

    @pl.when(i >= nt)
    def _():
        t0 = (i - nt) * LN_ROWS

        def fetch(r, c):
            st0[pl.ds(r, 1), :] = ysall[pl.ds(pos_ref[0, 0, t0 + r], 1), :]
            st1[pl.ds(r, 1), :] = ysall[pl.ds(pos_ref[0, 0, s + t0 + r], 1), :]
            return c
        lax.fori_loop(0, LN_ROWS, fetch, 0, unroll=True)
        a0, b0 = _unpack_bf16_pair(st0[...])
        a1, b1 = _unpack_bf16_pair(st1[...])
        w0, w1 = wcol_ref[0][:, 0:1], wcol_ref[0][:, 1:2]
        y = jnp.concatenate([w0 * a0 + w1 * a1, w0 * b0 + w1 * b1], axis=1)
        o_ref[0] = _layer_norm(alpha * x_ref[0] + gt_ref[0] * y, lng_ref[...], lnb_ref[...])


def _moe(x, gate, hp, pos, wcol, tiles, w1, w3, w2, ln_g, ln_b, alpha, tm):
    bsz, s, d = x.shape
    f = w1.shape[2]
    nt = 2 * s // tm + N_EXPERTS
    assert nt <= LANES and tm & (tm - 1) == 0 and s % LN_ROWS == 0
    blk = lambda i: jnp.maximum(i - nt, 0)
    vec = pl.BlockSpec((1, 1, d), lambda b, i, tl: (b, 0, 0))
    tok = lambda w: pl.BlockSpec((1, LN_ROWS, w), lambda b, i, tl: (b, blk(i), 0))
    row = pl.BlockSpec((1, d), lambda b, i, tl: (0, 0))
    expert = lambda shape: pl.BlockSpec(shape, lambda b, i, tl: (tl[b * (4 * LANES) + jnp.minimum(i, nt - 1)], 0, 0))
    grid_spec = pltpu.PrefetchScalarGridSpec(
        num_scalar_prefetch=1, grid=(bsz, nt + s // LN_ROWS),
        in_specs=[pl.BlockSpec((1, s, d // 2), lambda b, i, tl: (b, 0, 0), pipeline_mode=pl.Buffered(1)),
                  tok(d), tok(LANES), vec,
                  pl.BlockSpec((1, 1, 2 * s), lambda b, i, tl: (b, 0, 0), memory_space=pltpu.SMEM),
                  expert((1, d, f)), expert((1, d, f)), expert((1, f, d)), row, row],
        out_specs=tok(d),
        scratch_shapes=[pltpu.SMEM((nt * tm,), jnp.int32), pltpu.VMEM((tm, d // 2), jnp.uint32),
                        pltpu.VMEM((nt * tm, d // 2), jnp.uint32),
                        pltpu.VMEM((LN_ROWS, d // 2), jnp.uint32), pltpu.VMEM((LN_ROWS, d // 2), jnp.uint32)])
    return pl.pallas_call(
        functools.partial(_moe_body, alpha, tm, nt), grid_spec=grid_spec,
        out_shape=jax.ShapeDtypeStruct((bsz, s, d), F32),
        compiler_params=_cp(("arbitrary", "arbitrary"), VMEM_LIMIT), name="moe_experts",
    )(tiles.reshape(-1), hp, x, wcol, gate, pos, w1, w3, w2, ln_g.reshape(1, d), ln_b.reshape(1, d))


def kernel(x, c, positions, ada_w, ada_b, w_in, gdn_conv_w, gdn_a_log, gdn_dt_bias, gdn_norm_w, ret_gn_w,
           nsa_cmp_pe, nsa_ck_w1, nsa_ck_w2, nsa_cv_w1, nsa_cv_w2, fox_f_bias, branch_proj, w_gate, w_out,
           ln_g, ln_b, router_w, router_b, exp_w1, exp_w3, exp_w2):
    depth, d = w_in.shape[0], x.shape[-1]
    alpha = (2.0 * depth) ** 0.25
    mod = _ada_mod(c, ada_w, ada_b)
    for l in range(depth):
        shift, sc1p, gate = mod[l, 0, :, :, :d], mod[l, 0, :, :, d:2 * d], mod[l, 0, :, :, 2 * d:]
        wcat, wcat_t = _cat_in_weights(w_in[l])
        (gqkv, gz, r, kc, vc, ks, kw, fq, fk, sm,
         smt, nqt, vst, vwt, fvt) = _in_proj(x, sc1p, shift, wcat, wcat_t)
        o_gdn = _gdn(gqkv, sm, smt, gz, gdn_conv_w[l], gdn_a_log[l], gdn_dt_bias[l], gdn_norm_w[l])
        o_ret = _retention(r, positions, ret_gn_w[l])
        k_cmp, v_cmp_t = _nsa_compress(kc, vc, nsa_cmp_pe[l], nsa_ck_w1[l], nsa_ck_w2[l], nsa_cv_w1[l], nsa_cv_w2[l])
        o_nsa = _nsa_attn(nqt, k_cmp, v_cmp_t, ks, vst, kw, vwt, smt)
        o_fox = _fox_attn(fq, _fox_cum(sm, fk, fox_f_bias[l]), fvt)
        x = _merge(x, sc1p, shift, gate, o_gdn, o_ret, o_nsa, o_fox, w_gate[l].astype(BF16),
                   branch_proj[l].astype(BF16), w_out[l].astype(BF16), ln_g[l, 0], ln_b[l, 0], alpha)
        shift, sc1p, gate = mod[l, 1, :, :, :d], mod[l, 1, :, :, d:2 * d], mod[l, 1, :, :, 2 * d:]
        hp, pos, wcol, tiles = _router(x, sc1p, shift, router_w, router_b, MOE_TM)
        x = _moe(x, gate, hp, pos, wcol, tiles, exp_w1[l].astype(BF16), exp_w3[l].astype(BF16),
                 exp_w2[l].astype(BF16), ln_g[l, 1], ln_b[l, 1], alpha, MOE_TM)
    return x
```

```python
import functools
import math

import jax
import jax.numpy as jnp
import numpy as np
from jax import lax
from jax.experimental import pallas as pl
from jax.experimental.pallas import tpu as pltpu

F32 = jnp.float32
BF16 = jnp.bfloat16

N_HEADS = 4
HEAD_DIM = 64
MIX_W = N_HEADS * HEAD_DIM
GDN_CHUNK = 64
CONV_K = 4
RET_CHUNK = 128
ROPE_BASE = 10000.0
NSA_CMP_LEN = 32
NSA_CMP_STRIDE = 16
NSA_SEL_LEN = 64
NSA_TOP_N = 16
NSA_WINDOW = 512
NSA_FORCE_SCORE = 1.0e4
N_EXPERTS = 16
N_GROUPS = 4
EXPERTS_PER_GROUP = N_EXPERTS // N_GROUPS
LN_EPS = 1e-5
LANES = 128
NEG_BIG = -1.0e30
SEL_MASK_BIAS = -30000.0
LOG2E = math.log2(math.e)

IN_WIDTHS = (
    MIX_W, MIX_W, MIX_W, N_HEADS, N_HEADS, MIX_W,
    MIX_W, MIX_W, MIX_W, MIX_W,
    MIX_W, HEAD_DIM, HEAD_DIM, HEAD_DIM, HEAD_DIM, HEAD_DIM, HEAD_DIM, 3 * N_HEADS,
    MIX_W, MIX_W, MIX_W, N_HEADS,
)
_IN_OFF = np.concatenate([[0], np.cumsum(IN_WIDTHS)]).astype(int)
(_GQ, _GK, _GV, _GA, _GB, _GZ, _RQ, _RK, _RV, _RG, _NQ, _NKC, _NVC, _NKS, _NVS, _NKW, _NVW, _NGATE,
 _FQ, _FK, _FV, _FF) = range(22)
SM_A, SM_B, SM_GATE, SM_F = 0, 4, 8, 20

VMEM_LIMIT = 56 * 1024 * 1024


def _cp(sem, vmem=None):
    return pltpu.CompilerParams(dimension_semantics=sem, vmem_limit_bytes=vmem)


def _sigmoid(x):
    return 1.0 / (1.0 + jnp.exp(-x))


def _silu(x):
    return x * _sigmoid(x)


def _softplus(x):
    return jnp.maximum(x, 0.0) + jnp.log1p(jnp.exp(-jnp.abs(x)))


def _dot(a, b):
    return jnp.dot(a, b, preferred_element_type=F32)


def _dot_nt(a, b):
    return lax.dot_general(a, b, (((1,), (1,)), ((), ())), preferred_element_type=F32)


def _split3(a):
    a1 = a.astype(BF16)
    r = a - a1.astype(F32)
    a2 = r.astype(BF16)
    a3 = (r - a2.astype(F32)).astype(BF16)
    return a1, a2, a3


def _dot_x(a, m):
    a1, a2, a3 = _split3(a)
    return _dot(a1, m) + _dot(a2, m) + _dot(a3, m)


def _xdot(m, a):
    a1, a2, a3 = _split3(a)
    return _dot(m, a1) + _dot(m, a2) + _dot(m, a3)


def _dot_hp(a, b):
    a1, a2, _ = _split3(a)
    b1, b2, _ = _split3(b)
    return _dot(a1, b1) + (_dot(a1, b2) + _dot(a2, b1))


def _iota(shape, dim):
    return lax.broadcasted_iota(jnp.int32, shape, dim)


def _div(x, n):
    return lax.shift_right_logical(x, int(math.log2(n)))


def _mod(x, n):
    return x & (n - 1)


def _head_ones():
    return _div(_iota((MIX_W, MIX_W), 0), HEAD_DIM) == _div(_iota((MIX_W, MIX_W), 1), HEAD_DIM)


def _head_sum(x, ones):
    return _dot_x(x, ones)


def _ada_body(c_ref, w_ref, b_ref, o_ref):
    mod = _dot_hp(_silu(c_ref[...]), w_ref[0]) + b_ref[0]
    o_ref[0] = jnp.where(pl.program_id(1) == 1, 1.0 + mod, mod)


def _ada_mod(c, ada_w, ada_b):
    depth = ada_w.shape[0]
    bsz, d = c.shape
    n = depth * 2
    w = ada_w.reshape(n, d, 3 * d)
    b = ada_b.reshape(n, 1, 3 * d)
    out = pl.pallas_call(
        _ada_body, grid=(n, 3),
        in_specs=[pl.BlockSpec((bsz, d), lambda i, j: (0, 0)),
                  pl.BlockSpec((1, d, d), lambda i, j: (i, 0, j)),
                  pl.BlockSpec((1, 1, d), lambda i, j: (i, 0, j))],
        out_specs=pl.BlockSpec((1, bsz, d), lambda i, j: (i, 0, j)),
        out_shape=jax.ShapeDtypeStruct((n, bsz, 3 * d), F32),
        compiler_params=_cp(("arbitrary", "arbitrary")), name="ada_mod",
    )(c, w, b)
    return out.reshape(depth, 2, bsz, 1, 3 * d)


IN_TS = 512
_W_GROUPS = (768, 256, 1024, 256, 512, 128)
_W_OFF = np.concatenate([[0], np.cumsum(_W_GROUPS)]).astype(int)
IN_CAT_W = int(_W_OFF[-1])
_WT_GROUPS = (LANES, MIX_W, HEAD_DIM, HEAD_DIM, MIX_W)
_WT_OFF = np.concatenate([[0], np.cumsum(_WT_GROUPS)]).astype(int)
IN_CAT_T = int(_WT_OFF[-1])


def _cat_in_weights(w_in):
    def col(i):
        return w_in[:, _IN_OFF[i]:_IN_OFF[i + 1]]
    small = jnp.concatenate([col(_GA), col(_GB), col(_NGATE), col(_FF)], axis=1)
    small = jnp.pad(small, ((0, 0), (0, LANES - small.shape[1])))
    scale = HEAD_DIM ** -0.5
    cat = jnp.concatenate([
        col(_GQ), col(_GK), col(_GV), col(_GZ),
        col(_RQ), col(_RK) * scale, col(_RV), col(_RG),
        col(_NKC), col(_NVC), col(_NKS), col(_NKW),
        col(_FQ) * (scale * LOG2E), col(_FK),
        small], axis=1)
    cat_t = jnp.concatenate([small, col(_NQ) * (scale * LOG2E), col(_NVS), col(_NVW), col(_FV)], axis=1).T
    return cat.astype(BF16), cat_t.astype(BF16)


def _inproj_body(x_ref, sc_ref, sh_ref, w_ref, wt_ref,
                 gqkv_ref, gz_ref, r_ref, kc_ref, vc_ref, ks_ref, kw_ref, fq_ref, fk_ref, sm_ref,
                 smt_ref, nqt_ref, vst_ref, vwt_ref, fvt_ref):
    h = (x_ref[0] * sc_ref[0] + sh_ref[0]).astype(BF16)

    def proj(g):
        return _dot(h, w_ref[:, _W_OFF[g]:_W_OFF[g + 1]])

    gqkv_ref[0] = proj(0)
    gz_ref[0] = proj(1)
    r_ref[0] = proj(2)
    nkv = proj(3)
    for i, ref in enumerate((kc_ref, vc_ref, ks_ref, kw_ref)):
        ref[0] = nkv[:, i * HEAD_DIM:(i + 1) * HEAD_DIM].astype(ref.dtype)
    f = proj(4)
    for j, ref in enumerate((fq_ref, fk_ref)):
        for hd in range(N_HEADS):
            lo = j * MIX_W + hd * HEAD_DIM
            ref[0, hd] = f[:, lo:lo + HEAD_DIM].astype(BF16)
    sm_ref[0] = proj(5)
    for g, ref in enumerate((smt_ref, nqt_ref, vst_ref, vwt_ref, fvt_ref)):
        ref[0] = _dot_nt(wt_ref[_WT_OFF[g]:_WT_OFF[g + 1], :], h).astype(ref.dtype)


def _in_proj(x, sc1p, shift, wcat, wcat_t):
    bsz, s, d = x.shape
    ts = min(IN_TS, s)
    tok = lambda w, dt: jax.ShapeDtypeStruct((bsz, s, w), dt)
    chan = lambda w, dt: jax.ShapeDtypeStruct((bsz, w, s), dt)
    hm = jax.ShapeDtypeStruct((bsz, N_HEADS, s, HEAD_DIM), BF16)
    out_shape = [tok(768, F32), tok(256, F32), tok(1024, F32),
                 tok(64, F32), tok(64, F32), tok(64, BF16), tok(64, BF16), hm, hm, tok(LANES, F32),
                 chan(LANES, F32), chan(MIX_W, BF16), chan(HEAD_DIM, BF16), chan(HEAD_DIM, BF16), chan(MIX_W, BF16)]
    tspec = lambda w: pl.BlockSpec((1, ts, w), lambda b, i: (b, i, 0))
    cspec = lambda w: pl.BlockSpec((1, w, ts), lambda b, i: (b, 0, i))
    hspec = pl.BlockSpec((1, N_HEADS, ts, HEAD_DIM), lambda b, i: (b, 0, i, 0))
    out_specs = [tspec(768), tspec(256), tspec(1024),
                 tspec(64), tspec(64), tspec(64), tspec(64), hspec, hspec, tspec(LANES),
                 cspec(LANES), cspec(MIX_W), cspec(HEAD_DIM), cspec(HEAD_DIM), cspec(MIX_W)]
    vec = pl.BlockSpec((1, 1, d), lambda b, i: (b, 0, 0))
    return pl.pallas_call(
        _inproj_body, grid=(bsz, s // ts),
        in_specs=[tspec(d), vec, vec,
                  pl.BlockSpec((d, IN_CAT_W), lambda b, i: (0, 0)),
                  pl.BlockSpec((IN_CAT_T, d), lambda b, i: (0, 0))],
        out_specs=out_specs, out_shape=out_shape,
        compiler_params=_cp(("arbitrary", "arbitrary"), VMEM_LIMIT), name="in_proj",
    )(x, sc1p, shift, wcat, wcat_t)


CUM_T = 512


def _foxcum_body(sm_ref, k_ref, brow_ref, kaug_ref, carry_r):
    @pl.when(pl.program_id(1) == 0)
    def _():
        carry_r[...] = jnp.zeros_like(carry_r)

    t = sm_ref.shape[1]

    def log_sigmoid(v):
        return jnp.minimum(v, 0.0) - jnp.log1p(jnp.exp(-jnp.abs(v)))

    lower = (_iota((t, t), 0) >= _iota((t, t), 1)).astype(BF16)
    cc = _xdot(lower, log_sigmoid(sm_ref[0] + brow_ref[...])) + carry_r[...]
    carry_r[...] = cc[t - 1:t, :]
    parts = _split3(-cc * LOG2E)
    place = (_iota((HEAD_DIM, LANES), 0) == _iota((HEAD_DIM, LANES), 1)).astype(BF16)
    src, dst = _iota((LANES, LANES), 0), _iota((LANES, LANES), 1)
    for hd in range(N_HEADS):
        aug = _dot(k_ref[0, hd], place)
        for j, part in enumerate(parts):
            aug = aug + _dot(part, ((src == SM_F + hd) & (dst == HEAD_DIM + j)).astype(BF16))
        kaug_ref[0, hd] = aug.astype(BF16)


def _fox_cum(small, k, f_bias):
    bsz, s, _ = small.shape
    t = min(CUM_T, s)
    brow = jnp.zeros((1, LANES), F32).at[0, SM_F:SM_F + N_HEADS].set(f_bias)
    return pl.pallas_call(
        _foxcum_body, grid=(bsz, s // t),
        in_specs=[pl.BlockSpec((1, t, LANES), lambda b, i: (b, i, 0)),
                  pl.BlockSpec((1, N_HEADS, t, HEAD_DIM), lambda b, i: (b, 0, i, 0)),
                  pl.BlockSpec((1, LANES), lambda b, i: (0, 0))],
        out_specs=pl.BlockSpec((1, N_HEADS, t, LANES), lambda b, i: (b, 0, i, 0)),
        out_shape=jax.ShapeDtypeStruct((bsz, N_HEADS, s, LANES), BF16),
        scratch_shapes=[pltpu.VMEM((1, LANES), F32)],
        compiler_params=_cp(("arbitrary", "arbitrary")), name="fox_cum",
    )(small, k, brow)


FOX_T = 512


def _lockstep(gens):
    out, live = [None] * len(gens), list(range(len(gens)))
    while live:
        still = []
        for g in live:
            try:
                next(gens[g])
                still.append(g)
            except StopIteration as stop:
                out[g] = stop.value
        live = still
    return out


def _fox_body(q_ref, k_ref, vt_ref, o_ref):
    i = pl.program_id(1)
    t = q_ref.shape[2]
    place = (_iota((HEAD_DIM, LANES), 0) == _iota((HEAD_DIM, LANES), 1)).astype(BF16)
    lane = _iota((1, LANES), 1)
    ones = ((lane >= HEAD_DIM) & (lane < HEAD_DIM + 3)).astype(F32)
    causal = _iota((t, t), 0) <= _iota((t, t), 1)
    q_aug = [(_dot(q_ref[0, hd], place) + ones).astype(BF16) for hd in range(N_HEADS)]

    def head_step(hd, j, carry, masked):
        m, l, acc = carry
        keys = pl.ds(pl.multiple_of(j * t, t), t)
        s = _dot_nt(k_ref[0, hd, keys, :], q_aug[hd])
        yield
        if masked:
            s = jnp.where(causal, s, NEG_BIG)
        m_new = jnp.maximum(m, jnp.max(s, axis=0, keepdims=True))
        yield
        alpha = jnp.exp2(m - m_new)
        p = jnp.exp2(s - m_new)
        yield
        return (m_new, alpha * l + jnp.sum(p, axis=0, keepdims=True),
                alpha * acc + _dot(vt_ref[0, hd * HEAD_DIM:(hd + 1) * HEAD_DIM, keys], p.astype(BF16)))

    def step(j, carries, masked=False):
        return tuple(_lockstep([head_step(hd, j, carries[hd], masked) for hd in range(N_HEADS)]))

    carries = tuple((jnp.full((1, t), NEG_BIG, F32), jnp.zeros((1, t), F32), jnp.zeros((HEAD_DIM, t), F32))
                    for _ in range(N_HEADS))
    carries = lax.fori_loop(0, i, step, carries)
    carries = step(i, carries, masked=True)
    o_ref[0] = jnp.concatenate([acc / l for (_, l, acc) in carries], axis=0).T.astype(o_ref.dtype)


def _fox_attn(q, kaug, vt):
    bsz, nh, s, hd = q.shape
    t = min(FOX_T, s)
    return pl.pallas_call(
        _fox_body, grid=(bsz, s // t),
        in_specs=[pl.BlockSpec((1, nh, t, hd), lambda b, i: (b, 0, i, 0)),
                  pl.BlockSpec((1, nh, s, LANES), lambda b, i: (b, 0, 0, 0)),
                  pl.BlockSpec((1, MIX_W, s), lambda b, i: (b, 0, 0))],
        out_specs=pl.BlockSpec((1, t, MIX_W), lambda b, i: (b, i, 0)),
        out_shape=jax.ShapeDtypeStruct((bsz, s, MIX_W), BF16),
        compiler_params=_cp(("arbitrary", "arbitrary")), name="fox_attn",
    )(q, kaug, vt)


GDN_T = 256
GDN_NB = 2


def _gdn_body(qkv_ref, sm_ref, smt_ref, z_ref, cw_ref, alr_ref, dtr_ref, alc_ref, dtc_ref, nw_ref,
              o_ref, xbuf, state):
    t, c = GDN_T, GDN_CHUNK
    ri, ci = _iota((t, t), 0), _iota((t, t), 1)
    same = _div(ri, c) == _div(ci, c)
    er, ec = _iota((LANES, MIX_W), 0), _div(_iota((LANES, MIX_W), 1), HEAD_DIM)
    head_blk = _head_ones()
    masks = dict(same=same, incl=same & (ci <= ri), strict=same & (ci < ri),
                 same_b=same.astype(BF16), incl_b=(same & (ci <= ri)).astype(BF16),
                 incl_t_b=(same & (ri <= ci)).astype(BF16),
                 exp_a=(er == ec + SM_A).astype(BF16),
                 exp_b=(er == ec + SM_B).astype(BF16),
                 head_blk=head_blk, head_ones=head_blk.astype(BF16),
                 lane_head=_div(_iota((1, MIX_W), 1), HEAD_DIM))
    _lockstep([_gdn_tile(masks, qkv_ref.at[bb], sm_ref.at[bb], smt_ref.at[bb], z_ref.at[bb], cw_ref, alr_ref,
                         dtr_ref, alc_ref, dtc_ref, nw_ref, o_ref.at[bb], xbuf.at[bb], state.at[bb])
               for bb in range(qkv_ref.shape[0])])


def _gdn_tile(masks, qkv_ref, sm_ref, smt_ref, z_ref, cw_ref, alr_ref, dtr_ref, alc_ref, dtc_ref, nw_ref,
              o_ref, xbuf, state):
    t, c = GDN_T, GDN_CHUNK
    incl, strict, head_ones = masks["incl"], masks["strict"], masks["head_ones"]

    @pl.when(pl.program_id(1) == 0)
    def _():
        xbuf[0:8, :] = jnp.zeros((8, 3 * MIX_W), F32)
        state[...] = jnp.zeros_like(state)

    xbuf[8:8 + t, :] = qkv_ref[...]
    conv = cw_ref[0:1, :] * xbuf[pl.ds(8 - CONV_K + 1, t), :]
    for kk in range(1, CONV_K):
        conv = conv + cw_ref[kk:kk + 1, :] * xbuf[pl.ds(8 - CONV_K + 1 + kk, t), :]
    xbuf[0:8, :] = xbuf[t:t + 8, :]
    act = _silu(conv)
    q, k, v = act[:, :MIX_W], act[:, MIX_W:2 * MIX_W], act[:, 2 * MIX_W:]
    q = q * lax.rsqrt(_head_sum(q * q, head_ones) + 1e-6) * HEAD_DIM ** -0.5
    k = k * lax.rsqrt(_head_sum(k * k, head_ones) + 1e-6)

    sm, smt = sm_ref[...], smt_ref[...]
    la_col = -jnp.exp(alr_ref[...]) * _softplus(sm + dtr_ref[...])
    la_row = -jnp.exp(alc_ref[...]) * _softplus(smt + dtc_ref[...])
    beta_col = _sigmoid(sm)
    bc_col = _xdot(masks["incl_b"], la_col)
    last_col = _xdot(masks["same_b"], la_col)
    bc_row = _dot_x(la_row, masks["incl_t_b"])
    exp_a, exp_b = masks["exp_a"], masks["exp_b"]
    bc_x, last_x, beta_x = _dot_x(bc_col, exp_a), _dot_x(last_col, exp_a), _dot_x(beta_col, exp_b)
    eb = jnp.exp(bc_x)
    q_dec = q * eb
    k_dec = k * jnp.exp(last_x - bc_x)
    rhs_v = beta_x * v
    rhs_k = beta_x * eb * k
    rhs = jnp.concatenate([rhs_v, rhs_k], axis=1).astype(BF16)

    kb = k.astype(BF16)
    lane_head = masks["lane_head"]
    heads = range(N_HEADS)
    mh = [lane_head == hd for hd in heads]
    gram = [_dot_nt(jnp.where(mh[hd], k, 0.0).astype(BF16), kb) for hd in heads]
    qk = [_dot_nt(jnp.where(mh[hd], q, 0.0).astype(BF16), kb) for hd in heads]
    yield
    dec = [jnp.exp(jnp.minimum(bc_col[:, SM_A + hd:SM_A + hd + 1] - bc_row[SM_A + hd:SM_A + hd + 1, :], 0.0))
           for hd in heads]
    attn = [jnp.where(incl, qk[hd] * dec[hd], 0.0) for hd in heads]
    p = [jnp.where(strict, -(beta_col[:, SM_B + hd:SM_B + hd + 1] * gram[hd] * dec[hd]), 0.0) for hd in heads]
    tr = list(p)
    for _ in range(5):
        pb = [p[hd].astype(BF16) for hd in heads]
        p = [_dot(pb[hd], pb[hd]) for hd in heads]
        yield
        tr = [tr[hd] + p[hd] + _dot(tr[hd].astype(BF16), p[hd].astype(BF16)) for hd in heads]
    sol = [_dot(tr[hd].astype(BF16), rhs) for hd in heads]
    yield
    u0, w = rhs_v, rhs_k
    for hd in heads:
        u0 = u0 + jnp.where(mh[hd], sol[hd][:, :MIX_W], 0.0)
        w = w + jnp.where(mh[hd], sol[hd][:, MIX_W:], 0.0)

    head_blk = masks["head_blk"]
    k_dec_t = k_dec.T.astype(BF16)
    outs = []
    for n in range(t // c):
        rows = slice(n * c, (n + 1) * c)
        s_old = state[...]
        sb = s_old.astype(BF16)
        u_n = u0[rows] - _dot(w[rows].astype(BF16), sb)
        yield
        parts = ([jnp.zeros((n * c, MIX_W), F32)] if n else []) + [u_n]
        if t - (n + 1) * c:
            parts.append(jnp.zeros((t - (n + 1) * c, MIX_W), F32))
        u_full = jnp.concatenate(parts, axis=0).astype(BF16)
        oa = _dot(jnp.concatenate([a[rows] for a in attn], axis=0).astype(BF16), u_full)
        o_n = _dot(q_dec[rows].astype(BF16), sb)
        for hd in range(N_HEADS):
            o_n = o_n + jnp.where(lane_head == hd, oa[hd * c:(hd + 1) * c], 0.0)
        outs.append(o_n)
        g_row = jnp.exp(last_x[n * c:n * c + 1, :])
        state[...] = g_row * s_old + jnp.where(head_blk, _dot(k_dec_t, u_full), 0.0)
        yield
    o = jnp.concatenate(outs, axis=0)
    o = o * lax.rsqrt(_head_sum(o * o, head_ones) * (1.0 / HEAD_DIM) + 1e-6) * nw_ref[...]
    o_ref[...] = (o * _silu(z_ref[...])).astype(o_ref.dtype)


def _lane_vec(vals, off):
    return jnp.zeros((1, LANES), F32).at[0, off:off + vals.shape[0]].set(vals)


def _gdn(qkv, small, small_t, z, conv_w, a_log, dt_bias, norm_w):
    bsz, s, _ = qkv.shape
    t = GDN_T
    alr, dtr = _lane_vec(a_log, SM_A), _lane_vec(dt_bias, SM_A)
    nw = jnp.tile(norm_w, N_HEADS).reshape(1, MIX_W)
    const = lambda shape: pl.BlockSpec(shape, lambda b, i: (0,) * len(shape))
    nb = GDN_NB if bsz % GDN_NB == 0 else 1
    return pl.pallas_call(
        _gdn_body, grid=(bsz // nb, s // t),
        in_specs=[pl.BlockSpec((nb, t, 3 * MIX_W), lambda b, i: (b, i, 0)),
                  pl.BlockSpec((nb, t, LANES), lambda b, i: (b, i, 0)),
                  pl.BlockSpec((nb, LANES, t), lambda b, i: (b, 0, i)),
                  pl.BlockSpec((nb, t, MIX_W), lambda b, i: (b, i, 0)),
                  const((CONV_K, 3 * MIX_W)), const((1, LANES)), const((1, LANES)),
                  const((LANES, 1)), const((LANES, 1)), const((1, MIX_W))],
        out_specs=pl.BlockSpec((nb, t, MIX_W), lambda b, i: (b, i, 0)),
        out_shape=jax.ShapeDtypeStruct((bsz, s, MIX_W), BF16),
        scratch_shapes=[pltpu.VMEM((nb, t + 8, 3 * MIX_W), F32), pltpu.VMEM((nb, MIX_W, MIX_W), F32)],
        compiler_params=_cp(("arbitrary", "arbitrary")), name="gdn",
    )(qkv, small, small_t, z, conv_w, alr, dtr, alr.reshape(LANES, 1), dtr.reshape(LANES, 1), nw)


RET_T = 512
_RET_LOG_GAMMA = tuple(math.log1p(-(2.0 ** (-5.0 - h))) for h in range(N_HEADS))


def _per_head(lane_head, vals):
    out = jnp.full(lane_head.shape, vals[0], F32)
    for hd in range(1, N_HEADS):
        out = jnp.where(lane_head == hd, vals[hd], out)
    return out


def _ret_body(r_ref, pos_ref, gnw_ref, o_ref, state):
    t, c = r_ref.shape[1], RET_CHUNK

    @pl.when(pl.program_id(1) == 0)
    def _():
        state[...] = jnp.zeros_like(state)

    x = r_ref[0]
    lane = _iota((1, MIX_W), 1)
    lane_head = _div(lane, HEAD_DIM)
    half = HEAD_DIM // 2
    inv_freq = jnp.exp(_mod(_iota((1, LANES), 1), half).astype(F32) * (-math.log(ROPE_BASE) / half))
    ang = pos_ref[0].astype(F32) * inv_freq
    cos, sin = jnp.cos(ang), jnp.sin(ang)
    cos, sin = jnp.concatenate([cos, cos], axis=1), jnp.concatenate([sin, sin], axis=1)
    first = _mod(lane, HEAD_DIM) < half

    def rope(a):
        rot = jnp.where(first, -pltpu.roll(a, MIX_W - half, 1), pltpu.roll(a, half, 1))
        return a * cos + rot * sin

    q, k = rope(x[:, :MIX_W]), rope(x[:, MIX_W:2 * MIX_W])
    v, g = x[:, 2 * MIX_W:3 * MIX_W], x[:, 3 * MIX_W:]

    lg = _per_head(lane_head, _RET_LOG_GAMMA)
    cidx = _iota((c, 1), 0).astype(F32)
    xi = jnp.exp((cidx + 1.0) * lg)
    zeta = jnp.exp((c - 1.0 - cidx) * lg)
    chunk_decay = jnp.exp(float(c) * lg)
    rel = (_iota((c, c), 0) - _iota((c, c), 1)).astype(F32)
    dec = jnp.concatenate([jnp.where(rel >= 0, jnp.exp(jnp.maximum(rel, 0.0) * _RET_LOG_GAMMA[hd]), 0.0)
                           for hd in range(N_HEADS)], axis=0)
    head_blk = _head_ones()
    outs = []
    for n in range(t // c):
        rows = slice(n * c, (n + 1) * c)
        qn, kn, vn = q[rows], k[rows], v[rows].astype(BF16)
        qs = jnp.concatenate([jnp.where(lane_head == hd, qn, 0.0) for hd in range(N_HEADS)], axis=0).astype(BF16)
        sc = _dot_nt(qs, kn.astype(BF16)) * dec
        res = _dot(sc.astype(BF16), vn)
        s_old = state[...]
        o_n = _dot(qn.astype(BF16), s_old.astype(BF16)) * xi
        for hd in range(N_HEADS):
            o_n = o_n + jnp.where(lane_head == hd, res[hd * c:(hd + 1) * c], 0.0)
        outs.append(o_n)
        kv = _dot((kn * zeta).T.astype(BF16), vn)
        state[...] = chunk_decay * s_old + jnp.where(head_blk, kv, 0.0)
    o = jnp.concatenate(outs, axis=0)
    head_ones = head_blk.astype(BF16)
    mu = _head_sum(o, head_ones) * (1.0 / HEAD_DIM)
    xc = o - mu
    var = _head_sum(xc * xc, head_ones) * (1.0 / HEAD_DIM)
    y = xc * lax.rsqrt(var + LN_EPS) * gnw_ref[...]
    o_ref[0] = (_silu(g) * y).astype(o_ref.dtype)


def _retention(r, positions, gn_w):
    bsz, s, _ = r.shape
    t = min(RET_T, s)
    return pl.pallas_call(
        _ret_body, grid=(bsz, s // t),
        in_specs=[pl.BlockSpec((1, t, 4 * MIX_W), lambda b, i: (b, i, 0)),
                  pl.BlockSpec((1, t, 1), lambda b, i: (b, i, 0)),
                  pl.BlockSpec((1, MIX_W), lambda b, i: (0, 0))],
        out_specs=pl.BlockSpec((1, t, MIX_W), lambda b, i: (b, i, 0)),
        out_shape=jax.ShapeDtypeStruct((bsz, s, MIX_W), BF16),
        scratch_shapes=[pltpu.VMEM((MIX_W, MIX_W), F32)],
        compiler_params=_cp(("arbitrary", "arbitrary")), name="retention",
    )(r, positions.reshape(bsz, s, 1), gn_w.reshape(1, MIX_W))


def _nsacmp_body(kc_ref, vc_ref, pe_ref, kw1_ref, kw2_ref, vw1_ref, vw2_ref, ko_ref, vo_ref):
    half = NSA_CMP_STRIDE * HEAD_DIM
    nc = kc_ref.shape[1]
    def hidden(x_ref, w1_ref):
        x = x_ref[0]
        top = _dot_hp(x, w1_ref[0:half, :])
        bot = _dot_hp(x, w1_ref[half:2 * half, :])
        bias = _dot_hp(pe_ref[...], w1_ref[...])[0:1, :]
        return _silu(top + pltpu.roll(bot, nc - 1, 0) + bias)

    ko_ref[0] = _dot_hp(hidden(kc_ref, kw1_ref), kw2_ref[...])
    a1, a2, _ = _split3(vw2_ref[...])
    b1, b2, _ = _split3(hidden(vc_ref, vw1_ref))
    vo_ref[0] = _dot_nt(a1, b1) + (_dot_nt(a1, b2) + _dot_nt(a2, b1))


def _nsa_compress(kc, vc, pe, ck_w1, ck_w2, cv_w1, cv_w2):
    bsz, s, hd = kc.shape
    nc = s // NSA_CMP_STRIDE
    wide = NSA_CMP_STRIDE * hd
    pe8 = jnp.broadcast_to(pe.reshape(1, NSA_CMP_LEN * hd), (8, NSA_CMP_LEN * hd))
    cv_w2t = cv_w2.T
    const = lambda a: pl.BlockSpec(a.shape, lambda b: (0,) * a.ndim)
    xspec = pl.BlockSpec((1, nc, wide), lambda b: (b, 0, 0))
    return pl.pallas_call(
        _nsacmp_body, grid=(bsz,),
        in_specs=[xspec, xspec, const(pe8), const(ck_w1), const(ck_w2), const(cv_w1), const(cv_w2t)],
        out_specs=[pl.BlockSpec((1, nc, hd), lambda b: (b, 0, 0)), pl.BlockSpec((1, hd, nc), lambda b: (b, 0, 0))],
        out_shape=[jax.ShapeDtypeStruct((bsz, nc, hd), F32), jax.ShapeDtypeStruct((bsz, hd, nc), F32)],
        compiler_params=_cp(("arbitrary",)), name="nsa_compress",
    )(kc.reshape(bsz, nc, wide), vc.reshape(bsz, nc, wide), pe8, ck_w1, ck_w2, cv_w1, cv_w2t)


NSA_TQ = 256
NSA_TK = 512
NSA_UNROLL = 2


def _softmax_cols(s, valid):
    s = jnp.where(valid, s, NEG_BIG)
    m = jnp.max(s, axis=0, keepdims=True)
    m = jnp.where(m > 0.5 * NEG_BIG, m, 0.0)
    e = jnp.exp2(s - m)
    den = jnp.sum(e, axis=0, keepdims=True)
    return e, 1.0 / jnp.where(den > 0.0, den, 1.0)


def _nsa_body(qt_ref, kc_ref, vct_ref, ks_ref, vst_ref, kw_ref, vwt_ref, smt_ref, o_ref, kaug):
    i = pl.program_id(1)
    tq, hd = qt_ref.shape[2], HEAD_DIM
    s, nc = ks_ref.shape[1], kc_ref.shape[1]
    nsel = s // NSA_SEL_LEN
    n_top = min(NSA_TOP_N, nsel)
    cols = N_HEADS * tq

    @pl.when(i == 0)
    def _():
        place = (_iota((hd, LANES), 0) == _iota((hd, LANES), 1)).astype(BF16)
        onehot = (_iota((s, LANES), 1) - hd) == _div(_iota((s, LANES), 0), NSA_SEL_LEN)
        kaug[...] = (_dot(ks_ref[0], place) + onehot.astype(F32)).astype(BF16)

    t0 = i * tq
    qt = qt_ref[0]
    q4 = jnp.concatenate([qt[h * hd:(h + 1) * hd, :] for h in range(N_HEADS)], axis=1)
    tpos = t0 + _iota((1, tq), 1)
    tpos4 = t0 + _mod(_iota((1, cols), 1), tq)

    def cmp_branch():
        sc = _dot(kc_ref[0].astype(BF16), q4)
        yield
        cmp_end = _iota((nc, 1), 0) * NSA_CMP_STRIDE + (NSA_CMP_LEN - 1)
        e, inv = _softmax_cols(sc, cmp_end <= tpos4)
        p = e * inv
        yield
        return p, _dot(vct_ref[0].astype(BF16), p.astype(BF16))

    def win_branch():
        wlen = NSA_WINDOW + tq
        start = pl.multiple_of(jnp.maximum(t0 - NSA_WINDOW, 0), tq)
        sc = _dot(kw_ref[0, pl.ds(start, wlen), :], q4)
        yield
        kpos = start + _iota((wlen, 1), 0)
        e, inv = _softmax_cols(sc, (kpos <= tpos4) & (kpos > tpos4 - NSA_WINDOW))
        yield
        return _dot(vwt_ref[0, :, pl.ds(start, wlen)], e.astype(BF16)) * inv

    (p_cmp, o_cmp), o_win = _lockstep([cmp_branch(), win_branch()])

    psum = p_cmp[:, 0:tq] + p_cmp[:, tq:2 * tq] + p_cmp[:, 2 * tq:3 * tq] + p_cmp[:, 3 * tq:4 * tq]
    ss, cs = _iota((hd, nc), 0) * NSA_SEL_LEN, _iota((hd, nc), 1) * NSA_CMP_STRIDE
    overlap = jnp.clip(jnp.minimum(cs + NSA_CMP_LEN, ss + NSA_SEL_LEN) - jnp.maximum(cs, ss), 0, NSA_CMP_LEN)
    importance = _xdot((overlap.astype(F32) * (1.0 / NSA_CMP_LEN)).astype(BF16), psum)
    blk = _iota((hd, 1), 0)
    cur = _div(tpos, NSA_SEL_LEN)
    forced = (blk == 0) | (blk == cur) | (blk == cur - 1)
    score = jnp.where(forced, NSA_FORCE_SCORE, jnp.where(blk * NSA_SEL_LEN <= tpos, importance, -1.0))
    score = jnp.where(blk < nsel, score, -2.0)
    groups = [score[8 * g:8 * g + 8, :] for g in range(hd // 8)]
    gblk = _iota((8, 1), 0)
    cnts = [jnp.zeros((8, tq), jnp.int32) for _ in groups]
    for j in range(nsel):
        row = score[j:j + 1, :]
        for g in range(hd // 8):
            if 8 * g > j:
                ahead = row >= groups[g]
            elif 8 * g + 7 <= j:
                ahead = row > groups[g]
            else:
                ahead = (row > groups[g]) | ((row == groups[g]) & (gblk + 8 * g > j))
            cnts[g] = cnts[g] + ahead.astype(jnp.int32)
    bias = jnp.where(jnp.concatenate(cnts, axis=0) < n_top, 0.0, SEL_MASK_BIAS).astype(BF16)
    q_aug = jnp.concatenate([q4, jnp.concatenate([bias] * N_HEADS, axis=1)], axis=0)

    def sel_step(c, carry, masked=False):
        m, l, acc = carry
        keys = pl.ds(pl.multiple_of(c * NSA_TK, NSA_TK), NSA_TK)
        sc = _dot(kaug[keys, :], q_aug)
        if masked:
            sc = jnp.where(c * NSA_TK + _iota((NSA_TK, 1), 0) <= tpos4, sc, NEG_BIG)
        m_new = jnp.maximum(m, jnp.max(sc, axis=0, keepdims=True))
        alpha = jnp.exp2(m - m_new)
        p = jnp.exp2(sc - m_new)
        return (m_new, alpha * l + jnp.sum(p, axis=0, keepdims=True),
                alpha * acc + _dot(vst_ref[0, :, keys], p.astype(BF16)))

    def sel_group(g, carry):
        for u in range(NSA_UNROLL):
            carry = sel_step(g * NSA_UNROLL + u, carry)
        return carry

    n_full = _div(t0, NSA_TK)
    n_groups = _div(n_full, NSA_UNROLL)
    carry = (jnp.full((1, cols), NEG_BIG, F32), jnp.zeros((1, cols), F32), jnp.zeros((hd, cols), F32))
    carry = lax.fori_loop(0, n_groups, sel_group, carry)
    carry = lax.fori_loop(n_groups * NSA_UNROLL, n_full, sel_step, carry)
    _, l_sel, acc_sel = sel_step(n_full, carry, masked=True)
    o_sel = acc_sel / l_sel

    gates =_sigmoid(smt_ref[0][SM_GATE:SM_GATE + 16, :])
    outs = []
    for h in range(N_HEADS):
        c = slice(h * tq, (h + 1) * tq)
        outs.append(gates[3 * h:3 * h + 1, :] * o_cmp[:, c] + gates[3 * h + 1:3 * h + 2, :] * o_sel[:, c]
                    + gates[3 * h + 2:3 * h + 3, :] * o_win[:, c])
    o_ref[0] = jnp.concatenate(outs, axis=0).T.astype(o_ref.dtype)


def _nsa_attn(qt, k_cmp, v_cmp_t, ks, vst, kw, vwt, small_t):
    bsz, _, s = qt.shape
    tq, hd = NSA_TQ, HEAD_DIM
    assert s % NSA_TK == 0 and NSA_TK % tq == 0 and s >= NSA_WINDOW + tq and s // NSA_SEL_LEN <= hd
    nc = k_cmp.shape[1]
    keys = pl.BlockSpec((1, s, hd), lambda b, i: (b, 0, 0))
    vals = pl.BlockSpec((1, hd, s), lambda b, i: (b, 0, 0))
    return pl.pallas_call(
        _nsa_body, grid=(bsz, s // tq),
        in_specs=[pl.BlockSpec((1, MIX_W, tq), lambda b, i: (b, 0, i)),
                  pl.BlockSpec((1, nc, hd), lambda b, i: (b, 0, 0)),
                  pl.BlockSpec((1, hd, nc), lambda b, i: (b, 0, 0)),
                  keys, vals, keys, vals,
                  pl.BlockSpec((1, LANES, tq), lambda b, i: (b, 0, i))],
        out_specs=pl.BlockSpec((1, tq, MIX_W), lambda b, i: (b, i, 0)),
        out_shape=jax.ShapeDtypeStruct((bsz, s, MIX_W), BF16),
        scratch_shapes=[pltpu.VMEM((s, LANES), BF16)],
        compiler_params=_cp(("arbitrary", "arbitrary")), name="nsa_attn",
    )(qt, k_cmp, v_cmp_t, ks, vst, kw, vwt, small_t)


MERGE_T = 512


def _layer_norm(r, g, b):
    mu = jnp.mean(r, axis=-1, keepdims=True)
    xc = r - mu
    var = jnp.mean(xc * xc, axis=-1, keepdims=True)
    return xc * lax.rsqrt(var + LN_EPS) * g + b


def _merge_body(alpha, x_ref, sc_ref, sh_ref, gt_ref, og_ref, or_ref, on_ref, of_ref, wg_ref, bp_ref, wo_ref,
                lng_ref, lnb_ref, sc2_ref, sh2_ref, rwt_ref, o_ref, hp_ref, lg_ref):
    x = x_ref[0]
    h = (x * sc_ref[0] + sh_ref[0]).astype(BF16)

    merged = None
    for br, o_br in enumerate((og_ref, or_ref, on_ref, of_ref)):
        term = _sigmoid(_dot(h, wg_ref[br])) * _dot(o_br[0], bp_ref[br])
        merged = term if merged is None else merged + term
    y = _dot(merged.astype(BF16), wo_ref[...])
    x_new = _layer_norm(alpha * x + gt_ref[0] * y, lng_ref[...], lnb_ref[...])
    o_ref[0] = x_new
    h2 = x_new * sc2_ref[0] + sh2_ref[0]
    half = h2.shape[1] // 2
    hp_ref[0] = _pack_bf16_pair(h2[:, :half], h2[:, half:])
    a1, a2, _ = _split3(rwt_ref[...])
    b1, b2, _ = _split3(h2)
    lg_ref[0] = _dot_nt(a1, b1) + (_dot_nt(a1, b2) + _dot_nt(a2, b1))


def _merge(x, sc1p, shift, gate, o_gdn, o_ret, o_nsa, o_fox, w_gate, branch_proj, w_out, ln_g, ln_b,
           sc1p_moe, shift_moe, router_w, alpha):
    bsz, s, d = x.shape
    t = min(MERGE_T, s)
    tok = lambda w: pl.BlockSpec((1, t, w), lambda b, i: (b, i, 0))
    vec = pl.BlockSpec((1, 1, d), lambda b, i: (b, 0, 0))
    const = lambda a: pl.BlockSpec(a.shape, lambda b, i: (0,) * a.ndim, pipeline_mode=pl.Buffered(1))
    lng, lnb, rwt = ln_g.reshape(1, d), ln_b.reshape(1, d), router_w.T
    return pl.pallas_call(
        functools.partial(_merge_body, alpha), grid=(bsz, s // t),
        in_specs=[tok(d), vec, vec, vec, tok(MIX_W), tok(MIX_W), tok(MIX_W), tok(MIX_W),
                  const(w_gate), const(branch_proj), const(w_out), const(lng), const(lnb), vec, vec, const(rwt)],
        out_specs=[tok(d), tok(d // 2), pl.BlockSpec((1, N_EXPERTS, t), lambda b, i: (b, 0, i))],
        out_shape=[jax.ShapeDtypeStruct((bsz, s, d), F32), jax.ShapeDtypeStruct((bsz, s, d // 2), jnp.uint32),
                   jax.ShapeDtypeStruct((bsz, N_EXPERTS, s), F32)],
        compiler_params=_cp(("arbitrary", "arbitrary"), VMEM_LIMIT), name="merge",
    )(x, sc1p, shift, gate, o_gdn, o_ret, o_nsa, o_fox, w_gate, branch_proj, w_out, lng, lnb,
      sc1p_moe, shift_moe, rwt)


MOE_TM = 256


def _pack_bf16_pair(a, b):
    hi = lax.bitcast_convert_type(a.astype(BF16).astype(F32), jnp.uint32)
    lo = lax.bitcast_convert_type(b.astype(BF16).astype(F32), jnp.uint32)
    return hi | lax.shift_right_logical(lo, jnp.uint32(16))


def _unpack_bf16_pair(w):
    a = lax.bitcast_convert_type(w & jnp.uint32(0xFFFF0000), F32)
    b = lax.bitcast_convert_type(lax.shift_left(w, jnp.uint32(16)), F32)
    return a, b


def _router_body(tm, lg_ref, rb_ref, pos_ref, wcol_ref, tile_ref):
    s = lg_ref.shape[2]
    scores = _sigmoid(lg_ref[0])
    biased = scores + rb_ref[...]
    b = [biased[e:e + 1, :] for e in range(N_EXPERTS)]
    sc = [scores[e:e + 1, :] for e in range(N_EXPERTS)]
    gs = []
    for g in range(N_GROUPS):
        m = [b[EXPERTS_PER_GROUP * g + j] for j in range(EXPERTS_PER_GROUP)]
        best = m[0] + m[1]
        for u in range(EXPERTS_PER_GROUP):
            for v in range(u + 1, EXPERTS_PER_GROUP):
                if (u, v) != (0, 1):
                    best = jnp.maximum(best, m[u] + m[v])
        gs.append(best)
    gsel, best = jnp.zeros((1, s), jnp.int32), gs[0]
    for g in range(1, N_GROUPS):
        take = gs[g] > best
        gsel = jnp.where(take, g, gsel)
        best = jnp.where(take, gs[g], best)
    first, second = [], []
    for e in range(N_EXPERTS):
        g = e // EXPERTS_PER_GROUP
        cnt = jnp.zeros((1, s), jnp.int32)
        for j in range(EXPERTS_PER_GROUP * g, EXPERTS_PER_GROUP * (g + 1)):
            if j != e:
                ahead = (b[j] >= b[e]) if j < e else (b[j] > b[e])
                cnt = cnt + ahead.astype(jnp.int32)
        first.append((gsel == g) & (cnt == 0))
        second.append((gsel == g) & (cnt == 1))
    s0 = sum(jnp.where(first[e], sc[e], 0.0) for e in range(N_EXPERTS))
    s1 = sum(jnp.where(second[e], sc[e], 0.0) for e in range(N_EXPERTS))
    den = s0 + s1
    w0, w1 = s0 / den, s1 / den
    blk = min(512, s)
    eye = (_iota((blk, blk), 0) == _iota((blk, blk), 1)).astype(BF16)
    rowid = _iota((LANES, blk), 0)
    for j in range(s // blk):
        cols = slice(j * blk, (j + 1) * blk)
        wb = jnp.where(rowid == 0, w0[:, cols], jnp.where(rowid == 1, w1[:, cols], 0.0))
        wcol_ref[0, cols, :] = sum(_dot_nt(eye, part) for part in _split3(wb))
    onehot = jnp.concatenate([(first[e] | second[e]).astype(F32) for e in range(N_EXPERTS)], axis=0)
    before = (_iota((blk, blk), 0) < _iota((blk, blk), 1)).astype(BF16)
    carry, ranks = jnp.zeros((N_EXPERTS, 1), F32), []
    for j in range(s // blk):
        ob = onehot[:, j * blk:(j + 1) * blk]
        ranks.append(_dot(ob.astype(BF16), before) + carry)
        carry = carry + jnp.sum(ob, axis=1, keepdims=True)
    rank = jnp.concatenate(ranks, axis=1)
    ntile = jnp.floor((carry + (tm - 1.0)) * (1.0 / tm))
    lower = (_iota((N_EXPERTS, N_EXPERTS), 1) < _iota((N_EXPERTS, N_EXPERTS), 0)).astype(BF16)
    toff = _dot(lower, jnp.broadcast_to(ntile, (N_EXPERTS, LANES)).astype(BF16))[:, 0:1]
    slot = toff * float(tm) + rank
    pos0 = sum(jnp.where(first[e], slot[e:e + 1, :], 0.0) for e in range(N_EXPERTS))
    pos1 = sum(jnp.where(second[e], slot[e:e + 1, :], 0.0) for e in range(N_EXPERTS))
    pos_ref[0] = jnp.concatenate([pos0, pos1], axis=1).astype(jnp.int32)
    tend = toff + ntile
    tid = _iota((1, LANES), 1).astype(F32)
    texp = jnp.sum((tend <= tid).astype(F32), axis=0, keepdims=True)
    ntot = jnp.broadcast_to(jnp.sum(ntile, axis=0, keepdims=True), (1, LANES))
    diag = _iota((N_EXPERTS, LANES), 0) == _iota((N_EXPERTS, LANES), 1)
    to_lanes = lambda col: jnp.sum(jnp.where(diag, col, 0.0), axis=0, keepdims=True)
    tile_ref[0] = jnp.concatenate([jnp.minimum(texp, N_EXPERTS - 1.0), ntot, to_lanes(carry), to_lanes(toff)],
                                  axis=0).astype(jnp.int32)


def _router(logits, router_b, tm):
    bsz, _, s = logits.shape
    return pl.pallas_call(
        functools.partial(_router_body, tm), grid=(bsz,),
        in_specs=[pl.BlockSpec((1, N_EXPERTS, s), lambda b: (b, 0, 0)),
                  pl.BlockSpec((N_EXPERTS, 1), lambda b: (0, 0))],
        out_specs=[pl.BlockSpec((1, 1, 2 * s), lambda b: (b, 0, 0)),
                   pl.BlockSpec((1, s, LANES), lambda b: (b, 0, 0)),
                   pl.BlockSpec((1, 4, LANES), lambda b: (b, 0, 0))],
        out_shape=[jax.ShapeDtypeStruct((bsz, 1, 2 * s), jnp.int32),
                   jax.ShapeDtypeStruct((bsz, s, LANES), F32),
                   jax.ShapeDtypeStruct((bsz, 4, LANES), jnp.int32)],
        compiler_params=_cp(("arbitrary",)), name="router",
    )(logits, router_b.reshape(N_EXPERTS, 1))


LN_ROWS = 512


def _moe_body(alpha, tm, nt, tiles_ref, hp_ref, x_ref, wcol_ref, gt_ref, pos_ref,
              w1_ref, w3_ref, w2_ref, lng_ref, lnb_ref, o_ref, src, xs, ysall, st0, st1):
    b, i = pl.program_id(0), pl.program_id(1)
    s = hp_ref.shape[1]
    tb = b * (4 * LANES)

    @pl.when(i == 0)
    def _():
        for e in range(N_EXPERTS):
            cnt, first = tiles_ref[tb + 2 * LANES + e], tiles_ref[tb + 3 * LANES + e]
            lo = first * tm + cnt
            hi = (first + lax.shift_right_logical(cnt + (tm - 1), int(math.log2(tm)))) * tm

            def pad(p, c):
                src[p] = 0
                return c
            lax.fori_loop(lo, hi, pad, 0)

        def fill(t, c):
            for k in range(2):
                src[pos_ref[0, 0, k * s + t]] = t
            return c
        lax.fori_loop(0, s, fill, 0, unroll=8)

    @pl.when(i < tiles_ref[tb + LANES])
    def _():
        base = pl.multiple_of(i * tm, tm)

        def gather(r, c):
            xs[pl.ds(r, 1), :] = hp_ref[0, pl.ds(src[base + r], 1), :]
            return c
        lax.fori_loop(0, tm, gather, 0, unroll=True)
        xa, xb = _unpack_bf16_pair(xs[...])
        x_t = jnp.concatenate([xa, xb], axis=1).astype(BF16)
        hid = _silu(_dot(x_t, w1_ref[0])) * _dot(x_t, w3_ref[0])
        y = _dot(hid.astype(BF16), w2_ref[0])
        half = y.shape[1] // 2
        ysall[pl.ds(base, tm), :] = _pack_bf16_pair(y[:, :half], y[:, half:])

    @pl.when(i >= nt)
    def _():
        t0 = (i - nt) * LN_ROWS

        def fetch(r, c):
            st0[pl.ds(r, 1), :] = ysall[pl.ds(pos_ref[0, 0, t0 + r], 1), :]
            st1[pl.ds(r, 1), :] = ysall[pl.ds(pos_ref[0, 0, s + t0 + r], 1), :]
            return c
        lax.fori_loop(0, LN_ROWS, fetch, 0, unroll=True)
        a0, b0 = _unpack_bf16_pair(st0[...])
        a1, b1 = _unpack_bf16_pair(st1[...])
        w0, w1 = wcol_ref[0][:, 0:1], wcol_ref[0][:, 1:2]
        y = jnp.concatenate([w0 * a0 + w1 * a1, w0 * b0 + w1 * b1], axis=1)
        o_ref[0] = _layer_norm(alpha * x_ref[0] + gt_ref[0] * y, lng_ref[...], lnb_ref[...])


def _moe(x, gate, hp, pos, wcol, tiles, w1, w3, w2, ln_g, ln_b, alpha, tm):
    bsz, s, d = x.shape
    f = w1.shape[2]
    nt = 2 * s // tm + N_EXPERTS
    assert nt <= LANES and tm & (tm - 1) == 0 and s % LN_ROWS == 0
    blk = lambda i: jnp.maximum(i - nt, 0)
    vec = pl.BlockSpec((1, 1, d), lambda b, i, tl: (b, 0, 0))
    tok = lambda w: pl.BlockSpec((1, LN_ROWS, w), lambda b, i, tl: (b, blk(i), 0))
    row = pl.BlockSpec((1, d), lambda b, i, tl: (0, 0))
    expert = lambda shape: pl.BlockSpec(shape, lambda b, i, tl: (tl[b * (4 * LANES) + jnp.minimum(i, nt - 1)], 0, 0))
    grid_spec = pltpu.PrefetchScalarGridSpec(
        num_scalar_prefetch=1, grid=(bsz, nt + s // LN_ROWS),
        in_specs=[pl.BlockSpec((1, s, d // 2), lambda b, i, tl: (b, 0, 0), pipeline_mode=pl.Buffered(1)),
                  tok(d), tok(LANES), vec,
                  pl.BlockSpec((1, 1, 2 * s), lambda b, i, tl: (b, 0, 0), memory_space=pltpu.SMEM),
                  expert((1, d, f)), expert((1, d, f)), expert((1, f, d)), row, row],
        out_specs=tok(d),
        scratch_shapes=[pltpu.SMEM((nt * tm,), jnp.int32), pltpu.VMEM((tm, d // 2), jnp.uint32),
                        pltpu.VMEM((nt * tm, d // 2), jnp.uint32),
                        pltpu.VMEM((LN_ROWS, d // 2), jnp.uint32), pltpu.VMEM((LN_ROWS, d // 2), jnp.uint32)])
    return pl.pallas_call(
        functools.partial(_moe_body, alpha, tm, nt), grid_spec=grid_spec,
        out_shape=jax.ShapeDtypeStruct((bsz, s, d), F32),
        compiler_params=_cp(("arbitrary", "arbitrary"), VMEM_LIMIT), name="moe_experts",
    )(tiles.reshape(-1), hp, x, wcol, gate, pos, w1, w3, w2, ln_g.reshape(1, d), ln_b.reshape(1, d))


def kernel(x, c, positions, ada_w, ada_b, w_in, gdn_conv_w, gdn_a_log, gdn_dt_bias, gdn_norm_w, ret_gn_w,
           nsa_cmp_pe, nsa_ck_w1, nsa_ck_w2, nsa_cv_w1, nsa_cv_w2, fox_f_bias, branch_proj, w_gate, w_out,
           ln_g, ln_b, router_w, router_b, exp_w1, exp_w3, exp_w2):
    depth, d = w_in.shape[0], x.shape[-1]
    alpha = (2.0 * depth) ** 0.25
    mod = _ada_mod(c, ada_w, ada_b)
    for l in range(depth):
        shift, sc1p, gate = mod[l, 0, :, :, :d], mod[l, 0, :, :, d:2 * d], mod[l, 0, :, :, 2 * d:]
        wcat, wcat_t = _cat_in_weights(w_in[l])
        (gqkv, gz, r, kc, vc, ks, kw, fq, fk, sm,
         smt, nqt, vst, vwt, fvt) = _in_proj(x, sc1p, shift, wcat, wcat_t)
        o_gdn = _gdn(gqkv, sm, smt, gz, gdn_conv_w[l], gdn_a_log[l], gdn_dt_bias[l], gdn_norm_w[l])
        o_ret = _retention(r, positions, ret_gn_w[l])
        k_cmp, v_cmp_t = _nsa_compress(kc, vc, nsa_cmp_pe[l], nsa_ck_w1[l], nsa_ck_w2[l], nsa_cv_w1[l], nsa_cv_w2[l])
        o_nsa = _nsa_attn(nqt, k_cmp, v_cmp_t, ks, vst, kw, vwt, smt)
        o_fox = _fox_attn(fq, _fox_cum(sm, fk, fox_f_bias[l]), fvt)
        shift2, sc1p2, gate2 = mod[l, 1, :, :, :d], mod[l, 1, :, :, d:2 * d], mod[l, 1, :, :, 2 * d:]
        x, hp, logits = _merge(x, sc1p, shift, gate, o_gdn, o_ret, o_nsa, o_fox, w_gate[l].astype(BF16),
                               branch_proj[l].astype(BF16), w_out[l].astype(BF16), ln_g[l, 0], ln_b[l, 0],
                               sc1p2, shift2, router_w, alpha)
        pos, wcol, tiles = _router(logits, router_b, MOE_TM)
        x = _moe(x, gate2, hp, pos, wcol, tiles, exp_w1[l].astype(BF16), exp_w3[l].astype(BF16),
                 exp_w2[l].astype(BF16), ln_g[l, 1], ln_b[l, 1], alpha, MOE_TM)
    return x
```

```python
import functools
import math

import jax
import jax.numpy as jnp
import numpy as np
from jax import lax
from jax.experimental import pallas as pl
from jax.experimental.pallas import tpu as pltpu

F32 = jnp.float32
BF16 = jnp.bfloat16

N_HEADS = 4
HEAD_DIM = 64
MIX_W = N_HEADS * HEAD_DIM
GDN_CHUNK = 64
CONV_K = 4
RET_CHUNK = 128
ROPE_BASE = 10000.0
NSA_CMP_LEN = 32
NSA_CMP_STRIDE = 16
NSA_SEL_LEN = 64
NSA_TOP_N = 16
NSA_WINDOW = 512
NSA_FORCE_SCORE = 1.0e4
N_EXPERTS = 16
N_GROUPS = 4
EXPERTS_PER_GROUP = N_EXPERTS // N_GROUPS
LN_EPS = 1e-5
LANES = 128
MXU_N = 256
NEG_BIG = -1.0e30
SEL_MASK_BIAS = -30000.0
LOG2E = math.log2(math.e)

IN_WIDTHS = (
    MIX_W, MIX_W, MIX_W, N_HEADS, N_HEADS, MIX_W,
    MIX_W, MIX_W, MIX_W, MIX_W,
    MIX_W, HEAD_DIM, HEAD_DIM, HEAD_DIM, HEAD_DIM, HEAD_DIM, HEAD_DIM, 3 * N_HEADS,
    MIX_W, MIX_W, MIX_W, N_HEADS,
)
_IN_OFF = np.concatenate([[0], np.cumsum(IN_WIDTHS)]).astype(int)
(_GQ, _GK, _GV, _GA, _GB, _GZ, _RQ, _RK, _RV, _RG, _NQ, _NKC, _NVC, _NKS, _NVS, _NKW, _NVW, _NGATE,
 _FQ, _FK, _FV, _FF) = range(22)
SM_A, SM_B, SM_GATE, SM_F = 0, 4, 8, 20

VMEM_LIMIT = 56 * 1024 * 1024


def _cp(sem, vmem=None):
    return pltpu.CompilerParams(dimension_semantics=sem, vmem_limit_bytes=vmem)


def _sigmoid(x):
    return 1.0 / (1.0 + jnp.exp(-x))


def _silu(x):
    return x * _sigmoid(x)


def _softplus(x):
    return jnp.maximum(x, 0.0) + jnp.log1p(jnp.exp(-jnp.abs(x)))


def _dot(a, b):
    return jnp.dot(a, b, preferred_element_type=F32)


def _dot_nt(a, b):
    return lax.dot_general(a, b, (((1,), (1,)), ((), ())), preferred_element_type=F32)


def _split3(a):
    a1 = a.astype(BF16)
    r = a - a1.astype(F32)
    a2 = r.astype(BF16)
    a3 = (r - a2.astype(F32)).astype(BF16)
    return a1, a2, a3


def _dot_x(a, m):
    a1, a2, a3 = _split3(a)
    return _dot(a1, m) + _dot(a2, m) + _dot(a3, m)


def _xdot(m, a):
    a1, a2, a3 = _split3(a)
    return _dot(m, a1) + _dot(m, a2) + _dot(m, a3)


def _dot_hp(a, b):
    a1, a2, _ = _split3(a)
    b1, b2, _ = _split3(b)
    return _dot(a1, b1) + (_dot(a1, b2) + _dot(a2, b1))


def _iota(shape, dim):
    return lax.broadcasted_iota(jnp.int32, shape, dim)


def _div(x, n):
    return lax.shift_right_logical(x, int(math.log2(n)))


def _mod(x, n):
    return x & (n - 1)


def _head_ones():
    return _div(_iota((MIX_W, MIX_W), 0), HEAD_DIM) == _div(_iota((MIX_W, MIX_W), 1), HEAD_DIM)


def _head_sum(x, ones):
    return _dot_x(x, ones)


def _ada_body(c_ref, w_ref, b_ref, o_ref):
    mod = _dot_hp(_silu(c_ref[...]), w_ref[0]) + b_ref[0]
    o_ref[0] = jnp.where(pl.program_id(1) == 1, 1.0 + mod, mod)


def _ada_mod(c, ada_w, ada_b):
    depth = ada_w.shape[0]
    bsz, d = c.shape
    n = depth * 2
    w = ada_w.reshape(n, d, 3 * d)
    b = ada_b.reshape(n, 1, 3 * d)
    out = pl.pallas_call(
        _ada_body, grid=(n, 3),
        in_specs=[pl.BlockSpec((bsz, d), lambda i, j: (0, 0)),
                  pl.BlockSpec((1, d, d), lambda i, j: (i, 0, j)),
                  pl.BlockSpec((1, 1, d), lambda i, j: (i, 0, j))],
        out_specs=pl.BlockSpec((1, bsz, d), lambda i, j: (i, 0, j)),
        out_shape=jax.ShapeDtypeStruct((n, bsz, 3 * d), F32),
        compiler_params=_cp(("arbitrary", "arbitrary")), name="ada_mod",
    )(c, w, b)
    return out.reshape(depth, 2, bsz, 1, 3 * d)


IN_TS = 512
_W_GROUPS = (768, 256, 1024, 256, 512, 128)
_W_OFF = np.concatenate([[0], np.cumsum(_W_GROUPS)]).astype(int)
IN_CAT_W = int(_W_OFF[-1])
_WT_GROUPS = (LANES, MIX_W, HEAD_DIM, HEAD_DIM, MIX_W)
_WT_OFF = np.concatenate([[0], np.cumsum(_WT_GROUPS)]).astype(int)
IN_CAT_T = int(_WT_OFF[-1])


def _cat_in_weights(w_in):
    def col(i):
        return w_in[:, _IN_OFF[i]:_IN_OFF[i + 1]]
    small = jnp.concatenate([col(_GA), col(_GB), col(_NGATE), col(_FF)], axis=1)
    small = jnp.pad(small, ((0, 0), (0, LANES - small.shape[1])))
    scale = HEAD_DIM ** -0.5
    cat = jnp.concatenate([
        col(_GQ), col(_GK), col(_GV), col(_GZ),
        col(_RQ), col(_RK) * scale, col(_RV), col(_RG),
        col(_NKC), col(_NVC), col(_NKS), col(_NKW),
        col(_FQ) * (scale * LOG2E), col(_FK),
        small], axis=1)
    cat_t = jnp.concatenate([small, col(_NQ) * (scale * LOG2E), col(_NVS), col(_NVW), col(_FV)], axis=1).T
    return cat.astype(BF16), cat_t.astype(BF16)


def _inproj_body(x_ref, sc_ref, sh_ref, w_ref, wt_ref,
                 gqkv_ref, gz_ref, r_ref, kc_ref, vc_ref, ks_ref, kw_ref, fq_ref, fk_ref, sm_ref,
                 smt_ref, nqt_ref, vst_ref, vwt_ref, fvt_ref):
    h = (x_ref[0] * sc_ref[0] + sh_ref[0]).astype(BF16)

    def proj(g):
        return _dot(h, w_ref[:, _W_OFF[g]:_W_OFF[g + 1]])

    gqkv_ref[0] = proj(0)
    gz_ref[0] = proj(1)
    r_ref[0] = proj(2)
    nkv = proj(3)
    for i, ref in enumerate((kc_ref, vc_ref, ks_ref, kw_ref)):
        ref[0] = nkv[:, i * HEAD_DIM:(i + 1) * HEAD_DIM].astype(ref.dtype)
    f = proj(4)
    for j, ref in enumerate((fq_ref, fk_ref)):
        for hd in range(N_HEADS):
            lo = j * MIX_W + hd * HEAD_DIM
            ref[0, hd] = f[:, lo:lo + HEAD_DIM].astype(BF16)
    sm_ref[0] = proj(5)
    for g, ref in enumerate((smt_ref, nqt_ref, vst_ref, vwt_ref, fvt_ref)):
        ref[0] = _dot_nt(wt_ref[_WT_OFF[g]:_WT_OFF[g + 1], :], h).astype(ref.dtype)


def _in_proj(x, sc1p, shift, wcat, wcat_t):
    bsz, s, d = x.shape
    ts = min(IN_TS, s)
    tok = lambda w, dt: jax.ShapeDtypeStruct((bsz, s, w), dt)
    chan = lambda w, dt: jax.ShapeDtypeStruct((bsz, w, s), dt)
    hm = jax.ShapeDtypeStruct((bsz, N_HEADS, s, HEAD_DIM), BF16)
    out_shape = [tok(768, F32), tok(256, F32), tok(1024, F32),
                 tok(64, F32), tok(64, F32), tok(64, BF16), tok(64, BF16), hm, hm, tok(LANES, F32),
                 chan(LANES, F32), chan(MIX_W, BF16), chan(HEAD_DIM, BF16), chan(HEAD_DIM, BF16), chan(MIX_W, BF16)]
    tspec = lambda w: pl.BlockSpec((1, ts, w), lambda b, i: (b, i, 0))
    cspec = lambda w: pl.BlockSpec((1, w, ts), lambda b, i: (b, 0, i))
    hspec = pl.BlockSpec((1, N_HEADS, ts, HEAD_DIM), lambda b, i: (b, 0, i, 0))
    out_specs = [tspec(768), tspec(256), tspec(1024),
                 tspec(64), tspec(64), tspec(64), tspec(64), hspec, hspec, tspec(LANES),
                 cspec(LANES), cspec(MIX_W), cspec(HEAD_DIM), cspec(HEAD_DIM), cspec(MIX_W)]
    vec = pl.BlockSpec((1, 1, d), lambda b, i: (b, 0, 0))
    return pl.pallas_call(
        _inproj_body, grid=(bsz, s // ts),
        in_specs=[tspec(d), vec, vec,
                  pl.BlockSpec((d, IN_CAT_W), lambda b, i: (0, 0)),
                  pl.BlockSpec((IN_CAT_T, d), lambda b, i: (0, 0))],
        out_specs=out_specs, out_shape=out_shape,
        compiler_params=_cp(("arbitrary", "arbitrary"), VMEM_LIMIT), name="in_proj",
    )(x, sc1p, shift, wcat, wcat_t)


CUM_T = 512


def _foxcum_body(sm_ref, k_ref, brow_ref, kaug_ref, carry_r):
    @pl.when(pl.program_id(1) == 0)
    def _():
        carry_r[...] = jnp.zeros_like(carry_r)

    t = sm_ref.shape[1]

    def log_sigmoid(v):
        return jnp.minimum(v, 0.0) - jnp.log1p(jnp.exp(-jnp.abs(v)))

    lower = (_iota((t, t), 0) >= _iota((t, t), 1)).astype(BF16)
    cc = _xdot(lower, log_sigmoid(sm_ref[0] + brow_ref[...])) + carry_r[...]
    carry_r[...] = cc[t - 1:t, :]
    parts = _split3(-cc * LOG2E)
    place = (_iota((HEAD_DIM, LANES), 0) == _iota((HEAD_DIM, LANES), 1)).astype(BF16)
    src, dst = _iota((LANES, LANES), 0), _iota((LANES, LANES), 1)
    for hd in range(N_HEADS):
        aug = _dot(k_ref[0, hd], place)
        for j, part in enumerate(parts):
            aug = aug + _dot(part, ((src == SM_F + hd) & (dst == HEAD_DIM + j)).astype(BF16))
        kaug_ref[0, hd] = aug.astype(BF16)


def _fox_cum(small, k, f_bias):
    bsz, s, _ = small.shape
    t = min(CUM_T, s)
    brow = jnp.zeros((1, LANES), F32).at[0, SM_F:SM_F + N_HEADS].set(f_bias)
    return pl.pallas_call(
        _foxcum_body, grid=(bsz, s // t),
        in_specs=[pl.BlockSpec((1, t, LANES), lambda b, i: (b, i, 0)),
                  pl.BlockSpec((1, N_HEADS, t, HEAD_DIM), lambda b, i: (b, 0, i, 0)),
                  pl.BlockSpec((1, LANES), lambda b, i: (0, 0))],
        out_specs=pl.BlockSpec((1, N_HEADS, t, LANES), lambda b, i: (b, 0, i, 0)),
        out_shape=jax.ShapeDtypeStruct((bsz, N_HEADS, s, LANES), BF16),
        scratch_shapes=[pltpu.VMEM((1, LANES), F32)],
        compiler_params=_cp(("arbitrary", "arbitrary")), name="fox_cum",
    )(small, k, brow)


FOX_T = 512


def _lockstep(gens):
    out, live = [None] * len(gens), list(range(len(gens)))
    while live:
        still = []
        for g in live:
            try:
                next(gens[g])
                still.append(g)
            except StopIteration as stop:
                out[g] = stop.value
        live = still
    return out


def _fox_body(q_ref, k_ref, vt_ref, o_ref):
    i = pl.program_id(1)
    t = q_ref.shape[2]
    place = (_iota((HEAD_DIM, LANES), 0) == _iota((HEAD_DIM, LANES), 1)).astype(BF16)
    lane = _iota((1, LANES), 1)
    ones = ((lane >= HEAD_DIM) & (lane < HEAD_DIM + 3)).astype(F32)
    causal = _iota((t, t), 0) <= _iota((t, t), 1)
    q_aug = [(_dot(q_ref[0, hd], place) + ones).astype(BF16) for hd in range(N_HEADS)]

    def head_step(hd, j, carry, masked):
        m, l, acc = carry
        keys = pl.ds(pl.multiple_of(j * t, t), t)
        s = _dot_nt(k_ref[0, hd, keys, :], q_aug[hd])
        yield
        if masked:
            s = jnp.where(causal, s, NEG_BIG)
        m_new = jnp.maximum(m, jnp.max(s, axis=0, keepdims=True))
        yield
        alpha = jnp.exp2(m - m_new)
        p = jnp.exp2(s - m_new)
        yield
        return (m_new, alpha * l + jnp.sum(p, axis=0, keepdims=True),
                alpha * acc + _dot(vt_ref[0, hd * HEAD_DIM:(hd + 1) * HEAD_DIM, keys], p.astype(BF16)))

    def step(j, carries, masked=False):
        return tuple(_lockstep([head_step(hd, j, carries[hd], masked) for hd in range(N_HEADS)]))

    carries = tuple((jnp.full((1, t), NEG_BIG, F32), jnp.zeros((1, t), F32), jnp.zeros((HEAD_DIM, t), F32))
                    for _ in range(N_HEADS))
    carries = lax.fori_loop(0, i, step, carries)
    carries = step(i, carries, masked=True)
    o_ref[0] = jnp.concatenate([acc / l for (_, l, acc) in carries], axis=0).T.astype(o_ref.dtype)


def _fox_attn(q, kaug, vt):
    bsz, nh, s, hd = q.shape
    t = min(FOX_T, s)
    return pl.pallas_call(
        _fox_body, grid=(bsz, s // t),
        in_specs=[pl.BlockSpec((1, nh, t, hd), lambda b, i: (b, 0, i, 0)),
                  pl.BlockSpec((1, nh, s, LANES), lambda b, i: (b, 0, 0, 0)),
                  pl.BlockSpec((1, MIX_W, s), lambda b, i: (b, 0, 0))],
        out_specs=pl.BlockSpec((1, t, MIX_W), lambda b, i: (b, i, 0)),
        out_shape=jax.ShapeDtypeStruct((bsz, s, MIX_W), BF16),
        compiler_params=_cp(("arbitrary", "arbitrary")), name="fox_attn",
    )(q, kaug, vt)


GDN_T = 256
GDN_NB = 4


def _gdn_body(qkv_ref, sm_ref, smt_ref, z_ref, cw_ref, alr_ref, dtr_ref, alc_ref, dtc_ref, nw_ref,
              o_ref, xbuf, state):
    t, c = GDN_T, GDN_CHUNK
    ri, ci = _iota((t, t), 0), _iota((t, t), 1)
    same = _div(ri, c) == _div(ci, c)
    er, ec = _iota((LANES, MIX_W), 0), _div(_iota((LANES, MIX_W), 1), HEAD_DIM)
    head_blk = _head_ones()
    masks = dict(same=same, incl=same & (ci <= ri), strict=same & (ci < ri),
                 same_b=same.astype(BF16), incl_b=(same & (ci <= ri)).astype(BF16),
                 incl_t_b=(same & (ri <= ci)).astype(BF16),
                 exp_a=(er == ec + SM_A).astype(BF16),
                 exp_b=(er == ec + SM_B).astype(BF16),
                 head_blk=head_blk, head_ones=head_blk.astype(BF16),
                 lane_head=_div(_iota((1, MIX_W), 1), HEAD_DIM))
    _lockstep([_gdn_tile(masks, qkv_ref.at[bb], sm_ref.at[bb], smt_ref.at[bb], z_ref.at[bb], cw_ref, alr_ref,
                         dtr_ref, alc_ref, dtc_ref, nw_ref, o_ref.at[bb], xbuf.at[bb], state.at[bb])
               for bb in range(qkv_ref.shape[0])])


def _gdn_tile(masks, qkv_ref, sm_ref, smt_ref, z_ref, cw_ref, alr_ref, dtr_ref, alc_ref, dtc_ref, nw_ref,
              o_ref, xbuf, state):
    t, c = GDN_T, GDN_CHUNK
    incl, strict, head_ones = masks["incl"], masks["strict"], masks["head_ones"]

    @pl.when(pl.program_id(1) == 0)
    def _():
        xbuf[0:8, :] = jnp.zeros((8, 3 * MIX_W), F32)
        state[...] = jnp.zeros_like(state)

    xbuf[8:8 + t, :] = qkv_ref[...]
    conv = cw_ref[0:1, :] * xbuf[pl.ds(8 - CONV_K + 1, t), :]
    for kk in range(1, CONV_K):
        conv = conv + cw_ref[kk:kk + 1, :] * xbuf[pl.ds(8 - CONV_K + 1 + kk, t), :]
    xbuf[0:8, :] = xbuf[t:t + 8, :]
    act = _silu(conv)
    q, k, v = act[:, :MIX_W], act[:, MIX_W:2 * MIX_W], act[:, 2 * MIX_W:]
    q = q * lax.rsqrt(_head_sum(q * q, head_ones) + 1e-6) * HEAD_DIM ** -0.5
    k = k * lax.rsqrt(_head_sum(k * k, head_ones) + 1e-6)

    sm, smt = sm_ref[...], smt_ref[...]
    la_col = -jnp.exp(alr_ref[...]) * _softplus(sm + dtr_ref[...])
    la_row = -jnp.exp(alc_ref[...]) * _softplus(smt + dtc_ref[...])
    beta_col = _sigmoid(sm)
    bc_col = _xdot(masks["incl_b"], la_col)
    last_col = _xdot(masks["same_b"], la_col)
    bc_row = _dot_x(la_row, masks["incl_t_b"])
    exp_a, exp_b = masks["exp_a"], masks["exp_b"]
    bc_x, last_x, beta_x = _dot_x(bc_col, exp_a), _dot_x(last_col, exp_a), _dot_x(beta_col, exp_b)
    eb = jnp.exp(bc_x)
    q_dec = q * eb
    k_dec = k * jnp.exp(last_x - bc_x)
    rhs_v = beta_x * v
    rhs_k = beta_x * eb * k
    rhs = jnp.concatenate([rhs_v, rhs_k], axis=1).astype(BF16)

    kb = k.astype(BF16)
    lane_head = masks["lane_head"]
    heads = range(N_HEADS)
    mh = [lane_head == hd for hd in heads]
    gram = [_dot_nt(jnp.where(mh[hd], k, 0.0).astype(BF16), kb) for hd in heads]
    qk = [_dot_nt(jnp.where(mh[hd], q, 0.0).astype(BF16), kb) for hd in heads]
    yield
    dec = [jnp.exp(jnp.minimum(bc_col[:, SM_A + hd:SM_A + hd + 1] - bc_row[SM_A + hd:SM_A + hd + 1, :], 0.0))
           for hd in heads]
    attn = [jnp.where(incl, qk[hd] * dec[hd], 0.0) for hd in heads]
    p = [jnp.where(strict, -(beta_col[:, SM_B + hd:SM_B + hd + 1] * gram[hd] * dec[hd]), 0.0) for hd in heads]
    tr = list(p)
    for _ in range(5):
        pb = [p[hd].astype(BF16) for hd in heads]
        p = [_dot(pb[hd], pb[hd]) for hd in heads]
        yield
        tr = [tr[hd] + p[hd] + _dot(tr[hd].astype(BF16), p[hd].astype(BF16)) for hd in heads]
    sol = [_dot(tr[hd].astype(BF16), rhs) for hd in heads]
    yield
    u0, w = rhs_v, rhs_k
    for hd in heads:
        u0 = u0 + jnp.where(mh[hd], sol[hd][:, :MIX_W], 0.0)
        w = w + jnp.where(mh[hd], sol[hd][:, MIX_W:], 0.0)

    head_blk = masks["head_blk"]
    k_dec_t = k_dec.T.astype(BF16)
    outs = []
    for n in range(t // c):
        rows = slice(n * c, (n + 1) * c)
        s_old = state[...]
        sb = s_old.astype(BF16)
        u_n = u0[rows] - _dot(w[rows].astype(BF16), sb)
        yield
        parts = ([jnp.zeros((n * c, MIX_W), F32)] if n else []) + [u_n]
        if t - (n + 1) * c:
            parts.append(jnp.zeros((t - (n + 1) * c, MIX_W), F32))
        u_full = jnp.concatenate(parts, axis=0).astype(BF16)
        oa = _dot(jnp.concatenate([a[rows] for a in attn], axis=0).astype(BF16), u_full)
        o_n = _dot(q_dec[rows].astype(BF16), sb)
        for hd in range(N_HEADS):
            o_n = o_n + jnp.where(lane_head == hd, oa[hd * c:(hd + 1) * c], 0.0)
        outs.append(o_n)
        g_row = jnp.exp(last_x[n * c:n * c + 1, :])
        state[...] = g_row * s_old + jnp.where(head_blk, _dot(k_dec_t, u_full), 0.0)
        yield
    o = jnp.concatenate(outs, axis=0)
    o = o * lax.rsqrt(_head_sum(o * o, head_ones) * (1.0 / HEAD_DIM) + 1e-6) * nw_ref[...]
    o_ref[...] = (o * _silu(z_ref[...])).astype(o_ref.dtype)


def _lane_vec(vals, off):
    return jnp.zeros((1, LANES), F32).at[0, off:off + vals.shape[0]].set(vals)


def _gdn(qkv, small, small_t, z, conv_w, a_log, dt_bias, norm_w):
    bsz, s, _ = qkv.shape
    t = GDN_T
    alr, dtr = _lane_vec(a_log, SM_A), _lane_vec(dt_bias, SM_A)
    nw = jnp.tile(norm_w, N_HEADS).reshape(1, MIX_W)
    const = lambda shape: pl.BlockSpec(shape, lambda b, i: (0,) * len(shape))
    nb = GDN_NB if bsz % GDN_NB == 0 else 1
    return pl.pallas_call(
        _gdn_body, grid=(bsz // nb, s // t),
        in_specs=[pl.BlockSpec((nb, t, 3 * MIX_W), lambda b, i: (b, i, 0)),
                  pl.BlockSpec((nb, t, LANES), lambda b, i: (b, i, 0)),
                  pl.BlockSpec((nb, LANES, t), lambda b, i: (b, 0, i)),
                  pl.BlockSpec((nb, t, MIX_W), lambda b, i: (b, i, 0)),
                  const((CONV_K, 3 * MIX_W)), const((1, LANES)), const((1, LANES)),
                  const((LANES, 1)), const((LANES, 1)), const((1, MIX_W))],
        out_specs=pl.BlockSpec((nb, t, MIX_W), lambda b, i: (b, i, 0)),
        out_shape=jax.ShapeDtypeStruct((bsz, s, MIX_W), BF16),
        scratch_shapes=[pltpu.VMEM((nb, t + 8, 3 * MIX_W), F32), pltpu.VMEM((nb, MIX_W, MIX_W), F32)],
        compiler_params=_cp(("arbitrary", "arbitrary")), name="gdn",
    )(qkv, small, small_t, z, conv_w, alr, dtr, alr.reshape(LANES, 1), dtr.reshape(LANES, 1), nw)


RET_T = 1024
_RET_LOG_GAMMA = tuple(math.log1p(-(2.0 ** (-5.0 - h))) for h in range(N_HEADS))


def _per_head(lane_head, vals):
    out = jnp.full(lane_head.shape, vals[0], F32)
    for hd in range(1, N_HEADS):
        out = jnp.where(lane_head == hd, vals[hd], out)
    return out


def _ret_body(r_ref, pos_ref, gnw_ref, o_ref, state):
    t, c = r_ref.shape[1], RET_CHUNK

    @pl.when(pl.program_id(1) == 0)
    def _():
        state[...] = jnp.zeros_like(state)

    x = r_ref[0]
    lane = _iota((1, MIX_W), 1)
    lane_head = _div(lane, HEAD_DIM)
    half = HEAD_DIM // 2
    inv_freq = jnp.exp(_mod(_iota((1, LANES), 1), half).astype(F32) * (-math.log(ROPE_BASE) / half))
    ang = pos_ref[0].astype(F32) * inv_freq
    cos, sin = jnp.cos(ang), jnp.sin(ang)
    cos, sin = jnp.concatenate([cos, cos], axis=1), jnp.concatenate([sin, sin], axis=1)
    first = _mod(lane, HEAD_DIM) < half

    def rope(a):
        rot = jnp.where(first, -pltpu.roll(a, MIX_W - half, 1), pltpu.roll(a, half, 1))
        return a * cos + rot * sin

    q, k = rope(x[:, :MIX_W]), rope(x[:, MIX_W:2 * MIX_W])
    v, g = x[:, 2 * MIX_W:3 * MIX_W], x[:, 3 * MIX_W:]

    lg = _per_head(lane_head, _RET_LOG_GAMMA)
    cidx = _iota((c, 1), 0).astype(F32)
    xi = jnp.exp((cidx + 1.0) * lg)
    zeta = jnp.exp((c - 1.0 - cidx) * lg)
    chunk_decay = jnp.exp(float(c) * lg)
    rel = (_iota((c, c), 0) - _iota((c, c), 1)).astype(F32)
    dec = jnp.concatenate([jnp.where(rel >= 0, jnp.exp(jnp.maximum(rel, 0.0) * _RET_LOG_GAMMA[hd]), 0.0)
                           for hd in range(N_HEADS)], axis=0)
    head_blk = _head_ones()
    outs = []
    for n in range(t // c):
        rows = slice(n * c, (n + 1) * c)
        qn, kn, vn = q[rows], k[rows], v[rows].astype(BF16)
        qs = jnp.concatenate([jnp.where(lane_head == hd, qn, 0.0) for hd in range(N_HEADS)], axis=0).astype(BF16)
        sc = _dot_nt(qs, kn.astype(BF16)) * dec
        res = _dot(sc.astype(BF16), vn)
        s_old = state[...]
        o_n = _dot(qn.astype(BF16), s_old.astype(BF16)) * xi
        for hd in range(N_HEADS):
            o_n = o_n + jnp.where(lane_head == hd, res[hd * c:(hd + 1) * c], 0.0)
        outs.append(o_n)
        kv = _dot((kn * zeta).T.astype(BF16), vn)
        state[...] = chunk_decay * s_old + jnp.where(head_blk, kv, 0.0)
    o = jnp.concatenate(outs, axis=0)
    head_ones = head_blk.astype(BF16)
    mu = _head_sum(o, head_ones) * (1.0 / HEAD_DIM)
    xc = o - mu
    var = _head_sum(xc * xc, head_ones) * (1.0 / HEAD_DIM)
    y = xc * lax.rsqrt(var + LN_EPS) * gnw_ref[...]
    o_ref[0] = (_silu(g) * y).astype(o_ref.dtype)


def _retention(r, positions, gn_w):
    bsz, s, _ = r.shape
    t = min(RET_T, s)
    return pl.pallas_call(
        _ret_body, grid=(bsz, s // t),
        in_specs=[pl.BlockSpec((1, t, 4 * MIX_W), lambda b, i: (b, i, 0)),
                  pl.BlockSpec((1, t, 1), lambda b, i: (b, i, 0)),
                  pl.BlockSpec((1, MIX_W), lambda b, i: (0, 0))],
        out_specs=pl.BlockSpec((1, t, MIX_W), lambda b, i: (b, i, 0)),
        out_shape=jax.ShapeDtypeStruct((bsz, s, MIX_W), BF16),
        scratch_shapes=[pltpu.VMEM((MIX_W, MIX_W), F32)],
        compiler_params=_cp(("arbitrary", "arbitrary")), name="retention",
    )(r, positions.reshape(bsz, s, 1), gn_w.reshape(1, MIX_W))


def _nsacmp_body(kc_ref, vc_ref, pe_ref, kw1_ref, kw2_ref, vw1_ref, vw2_ref, ko_ref, vo_ref):
    half = NSA_CMP_STRIDE * HEAD_DIM
    nc = kc_ref.shape[1]
    def hidden(x_ref, w1_ref):
        x = x_ref[0]
        top = _dot_hp(x, w1_ref[0:half, :])
        bot = _dot_hp(x, w1_ref[half:2 * half, :])
        bias = _dot_hp(pe_ref[...], w1_ref[...])[0:1, :]
        return _silu(top + pltpu.roll(bot, nc - 1, 0) + bias)

    ko_ref[0] = _dot_hp(hidden(kc_ref, kw1_ref), kw2_ref[...])
    a1, a2, _ = _split3(vw2_ref[...])
    b1, b2, _ = _split3(hidden(vc_ref, vw1_ref))
    vo_ref[0] = _dot_nt(a1, b1) + (_dot_nt(a1, b2) + _dot_nt(a2, b1))


def _nsa_compress(kc, vc, pe, ck_w1, ck_w2, cv_w1, cv_w2):
    bsz, s, hd = kc.shape
    nc = s // NSA_CMP_STRIDE
    wide = NSA_CMP_STRIDE * hd
    pe8 = jnp.broadcast_to(pe.reshape(1, NSA_CMP_LEN * hd), (8, NSA_CMP_LEN * hd))
    cv_w2t = cv_w2.T
    const = lambda a: pl.BlockSpec(a.shape, lambda b: (0,) * a.ndim)
    xspec = pl.BlockSpec((1, nc, wide), lambda b: (b, 0, 0))
    return pl.pallas_call(
        _nsacmp_body, grid=(bsz,),
        in_specs=[xspec, xspec, const(pe8), const(ck_w1), const(ck_w2), const(cv_w1), const(cv_w2t)],
        out_specs=[pl.BlockSpec((1, nc, hd), lambda b: (b, 0, 0)), pl.BlockSpec((1, hd, nc), lambda b: (b, 0, 0))],
        out_shape=[jax.ShapeDtypeStruct((bsz, nc, hd), F32), jax.ShapeDtypeStruct((bsz, hd, nc), F32)],
        compiler_params=_cp(("arbitrary",)), name="nsa_compress",
    )(kc.reshape(bsz, nc, wide), vc.reshape(bsz, nc, wide), pe8, ck_w1, ck_w2, cv_w1, cv_w2t)


NSA_TQ = 256
NSA_TK = 512
NSA_UNROLL = 2


def _softmax_cols(s, valid):
    s = jnp.where(valid, s, NEG_BIG)
    m = jnp.max(s, axis=0, keepdims=True)
    m = jnp.where(m > 0.5 * NEG_BIG, m, 0.0)
    e = jnp.exp2(s - m)
    den = jnp.sum(e, axis=0, keepdims=True)
    return e, 1.0 / jnp.where(den > 0.0, den, 1.0)


def _nsa_body(qt_ref, kc_ref, vct_ref, ks_ref, vst_ref, kw_ref, vwt_ref, smt_ref, o_ref, kaug):
    i = pl.program_id(1)
    tq, hd = qt_ref.shape[2], HEAD_DIM
    s, nc = ks_ref.shape[1], kc_ref.shape[1]
    nsel = s // NSA_SEL_LEN
    n_top = min(NSA_TOP_N, nsel)
    cols = N_HEADS * tq

    @pl.when(i == 0)
    def _():
        place = (_iota((hd, LANES), 0) == _iota((hd, LANES), 1)).astype(BF16)
        onehot = (_iota((s, LANES), 1) - hd) == _div(_iota((s, LANES), 0), NSA_SEL_LEN)
        kaug[...] = (_dot(ks_ref[0], place) + onehot.astype(F32)).astype(BF16)

    t0 = i * tq
    qt = qt_ref[0]
    q4 = jnp.concatenate([qt[h * hd:(h + 1) * hd, :] for h in range(N_HEADS)], axis=1)
    tpos = t0 + _iota((1, tq), 1)
    tpos4 = t0 + _mod(_iota((1, cols), 1), tq)

    def cmp_branch():
        sc = _dot(kc_ref[0].astype(BF16), q4)
        yield
        cmp_end = _iota((nc, 1), 0) * NSA_CMP_STRIDE + (NSA_CMP_LEN - 1)
        e, inv = _softmax_cols(sc, cmp_end <= tpos4)
        p = e * inv
        yield
        return p, _dot(vct_ref[0].astype(BF16), p.astype(BF16))

    def win_branch():
        wlen = NSA_WINDOW + tq
        start = pl.multiple_of(jnp.maximum(t0 - NSA_WINDOW, 0), tq)
        sc = _dot(kw_ref[0, pl.ds(start, wlen), :], q4)
        yield
        kpos = start + _iota((wlen, 1), 0)
        e, inv = _softmax_cols(sc, (kpos <= tpos4) & (kpos > tpos4 - NSA_WINDOW))
        yield
        return _dot(vwt_ref[0, :, pl.ds(start, wlen)], e.astype(BF16)) * inv

    (p_cmp, o_cmp), o_win = _lockstep([cmp_branch(), win_branch()])

    psum = p_cmp[:, 0:tq] + p_cmp[:, tq:2 * tq] + p_cmp[:, 2 * tq:3 * tq] + p_cmp[:, 3 * tq:4 * tq]
    ss, cs = _iota((hd, nc), 0) * NSA_SEL_LEN, _iota((hd, nc), 1) * NSA_CMP_STRIDE
    overlap = jnp.clip(jnp.minimum(cs + NSA_CMP_LEN, ss + NSA_SEL_LEN) - jnp.maximum(cs, ss), 0, NSA_CMP_LEN)
    importance = _xdot((overlap.astype(F32) * (1.0 / NSA_CMP_LEN)).astype(BF16), psum)
    blk = _iota((hd, 1), 0)
    cur = _div(tpos, NSA_SEL_LEN)
    forced = (blk == 0) | (blk == cur) | (blk == cur - 1)
    score = jnp.where(forced, NSA_FORCE_SCORE, jnp.where(blk * NSA_SEL_LEN <= tpos, importance, -1.0))
    score = jnp.where(blk < nsel, score, -2.0)
    groups = [score[8 * g:8 * g + 8, :] for g in range(hd // 8)]
    gblk = _iota((8, 1), 0)
    cnts = [jnp.zeros((8, tq), jnp.int32) for _ in groups]
    for j in range(nsel):
        row = score[j:j + 1, :]
        for g in range(hd // 8):
            if 8 * g > j:
                ahead = row >= groups[g]
            elif 8 * g + 7 <= j:
                ahead = row > groups[g]
            else:
                ahead = (row > groups[g]) | ((row == groups[g]) & (gblk + 8 * g > j))
            cnts[g] = cnts[g] + ahead.astype(jnp.int32)
    bias = jnp.where(jnp.concatenate(cnts, axis=0) < n_top, 0.0, SEL_MASK_BIAS).astype(BF16)
    q_aug = jnp.concatenate([q4, jnp.concatenate([bias] * N_HEADS, axis=1)], axis=0)

    def sel_step(c, carry, masked=False):
        m, l, acc = carry
        keys = pl.ds(pl.multiple_of(c * NSA_TK, NSA_TK), NSA_TK)
        sc = _dot(kaug[keys, :], q_aug)
        if masked:
            sc = jnp.where(c * NSA_TK + _iota((NSA_TK, 1), 0) <= tpos4, sc, NEG_BIG)
        m_new = jnp.maximum(m, jnp.max(sc, axis=0, keepdims=True))
        alpha = jnp.exp2(m - m_new)
        p = jnp.exp2(sc - m_new)
        return (m_new, alpha * l + jnp.sum(p, axis=0, keepdims=True),
                alpha * acc + _dot(vst_ref[0, :, keys], p.astype(BF16)))

    def sel_group(g, carry):
        for u in range(NSA_UNROLL):
            carry = sel_step(g * NSA_UNROLL + u, carry)
        return carry

    n_full = _div(t0, NSA_TK)
    n_groups = _div(n_full, NSA_UNROLL)
    carry = (jnp.full((1, cols), NEG_BIG, F32), jnp.zeros((1, cols), F32), jnp.zeros((hd, cols), F32))
    carry = lax.fori_loop(0, n_groups, sel_group, carry)
    carry = lax.fori_loop(n_groups * NSA_UNROLL, n_full, sel_step, carry)
    _, l_sel, acc_sel = sel_step(n_full, carry, masked=True)
    o_sel = acc_sel / l_sel

    gates =_sigmoid(smt_ref[0][SM_GATE:SM_GATE + 16, :])
    outs = []
    for h in range(N_HEADS):
        c = slice(h * tq, (h + 1) * tq)
        outs.append(gates[3 * h:3 * h + 1, :] * o_cmp[:, c] + gates[3 * h + 1:3 * h + 2, :] * o_sel[:, c]
                    + gates[3 * h + 2:3 * h + 3, :] * o_win[:, c])
    o_ref[0] = jnp.concatenate(outs, axis=0).T.astype(o_ref.dtype)


def _nsa_attn(qt, k_cmp, v_cmp_t, ks, vst, kw, vwt, small_t):
    bsz, _, s = qt.shape
    tq, hd = NSA_TQ, HEAD_DIM
    assert s % NSA_TK == 0 and NSA_TK % tq == 0 and s >= NSA_WINDOW + tq and s // NSA_SEL_LEN <= hd
    nc = k_cmp.shape[1]
    keys = pl.BlockSpec((1, s, hd), lambda b, i: (b, 0, 0))
    vals = pl.BlockSpec((1, hd, s), lambda b, i: (b, 0, 0))
    return pl.pallas_call(
        _nsa_body, grid=(bsz, s // tq),
        in_specs=[pl.BlockSpec((1, MIX_W, tq), lambda b, i: (b, 0, i)),
                  pl.BlockSpec((1, nc, hd), lambda b, i: (b, 0, 0)),
                  pl.BlockSpec((1, hd, nc), lambda b, i: (b, 0, 0)),
                  keys, vals, keys, vals,
                  pl.BlockSpec((1, LANES, tq), lambda b, i: (b, 0, i))],
        out_specs=pl.BlockSpec((1, tq, MIX_W), lambda b, i: (b, i, 0)),
        out_shape=jax.ShapeDtypeStruct((bsz, s, MIX_W), BF16),
        scratch_shapes=[pltpu.VMEM((s, LANES), BF16)],
        compiler_params=_cp(("arbitrary", "arbitrary")), name="nsa_attn",
    )(qt, k_cmp, v_cmp_t, ks, vst, kw, vwt, small_t)


MERGE_T = 1024


def _layer_norm(r, g, b):
    mu = jnp.mean(r, axis=-1, keepdims=True)
    xc = r - mu
    var = jnp.mean(xc * xc, axis=-1, keepdims=True)
    return xc * lax.rsqrt(var + LN_EPS) * g + b


def _merge_body(alpha, x_ref, sc_ref, sh_ref, gt_ref, og_ref, or_ref, on_ref, of_ref, wg_ref, bp_ref, wo_ref,
                lng_ref, lnb_ref, sc2_ref, sh2_ref, rwt_ref, o_ref, hp_ref, lg_ref):
    x = x_ref[0]
    h = (x * sc_ref[0] + sh_ref[0]).astype(BF16)

    merged = None
    for br, o_br in enumerate((og_ref, or_ref, on_ref, of_ref)):
        term = _sigmoid(_dot(h, wg_ref[br])) * _dot(o_br[0], bp_ref[br])
        merged = term if merged is None else merged + term
    y = _dot(merged.astype(BF16), wo_ref[...])
    x_new = _layer_norm(alpha * x + gt_ref[0] * y, lng_ref[...], lnb_ref[...])
    o_ref[0] = x_new
    h2 = x_new * sc2_ref[0] + sh2_ref[0]
    half = h2.shape[1] // 2
    hp_ref[0] = _pack_bf16_pair(h2[:, :half], h2[:, half:])
    a1, a2, _ = _split3(rwt_ref[...])
    b1, b2, _ = _split3(h2)
    lg_ref[0] = _dot_nt(a1, b1) + (_dot_nt(a1, b2) + _dot_nt(a2, b1))


def _merge(x, sc1p, shift, gate, o_gdn, o_ret, o_nsa, o_fox, w_gate, branch_proj, w_out, ln_g, ln_b,
           sc1p_moe, shift_moe, router_w, alpha):
    bsz, s, d = x.shape
    t = min(MERGE_T, s)
    tok = lambda w: pl.BlockSpec((1, t, w), lambda b, i: (b, i, 0))
    vec = pl.BlockSpec((1, 1, d), lambda b, i: (b, 0, 0))
    const = lambda a: pl.BlockSpec(a.shape, lambda b, i: (0,) * a.ndim, pipeline_mode=pl.Buffered(1))
    lng, lnb, rwt = ln_g.reshape(1, d), ln_b.reshape(1, d), router_w.T
    return pl.pallas_call(
        functools.partial(_merge_body, alpha), grid=(bsz, s // t),
        in_specs=[tok(d), vec, vec, vec, tok(MIX_W), tok(MIX_W), tok(MIX_W), tok(MIX_W),
                  const(w_gate), const(branch_proj), const(w_out), const(lng), const(lnb), vec, vec, const(rwt)],
        out_specs=[tok(d), tok(d // 2), pl.BlockSpec((1, N_EXPERTS, t), lambda b, i: (b, 0, i))],
        out_shape=[jax.ShapeDtypeStruct((bsz, s, d), F32), jax.ShapeDtypeStruct((bsz, s, d // 2), jnp.uint32),
                   jax.ShapeDtypeStruct((bsz, N_EXPERTS, s), F32)],
        compiler_params=_cp(("arbitrary", "arbitrary"), VMEM_LIMIT), name="merge",
    )(x, sc1p, shift, gate, o_gdn, o_ret, o_nsa, o_fox, w_gate, branch_proj, w_out, lng, lnb,
      sc1p_moe, shift_moe, rwt)


MOE_TM = 256


def _pack_bf16_pair(a, b):
    hi = lax.bitcast_convert_type(a.astype(BF16).astype(F32), jnp.uint32)
    lo = lax.bitcast_convert_type(b.astype(BF16).astype(F32), jnp.uint32)
    return hi | lax.shift_right_logical(lo, jnp.uint32(16))


def _unpack_bf16_pair(w):
    a = lax.bitcast_convert_type(w & jnp.uint32(0xFFFF0000), F32)
    b = lax.bitcast_convert_type(lax.shift_left(w, jnp.uint32(16)), F32)
    return a, b


def _router_body(tm, lg_ref, rb_ref, pos_ref, wcol_ref, tile_ref):
    s = lg_ref.shape[2]
    scores = _sigmoid(lg_ref[0])
    biased = scores + rb_ref[...]
    b = [biased[e:e + 1, :] for e in range(N_EXPERTS)]
    sc = [scores[e:e + 1, :] for e in range(N_EXPERTS)]
    gs = []
    for g in range(N_GROUPS):
        m = [b[EXPERTS_PER_GROUP * g + j] for j in range(EXPERTS_PER_GROUP)]
        best = m[0] + m[1]
        for u in range(EXPERTS_PER_GROUP):
            for v in range(u + 1, EXPERTS_PER_GROUP):
                if (u, v) != (0, 1):
                    best = jnp.maximum(best, m[u] + m[v])
        gs.append(best)
    gsel, best = jnp.zeros((1, s), jnp.int32), gs[0]
    for g in range(1, N_GROUPS):
        take = gs[g] > best
        gsel = jnp.where(take, g, gsel)
        best = jnp.where(take, gs[g], best)
    first, second = [], []
    for e in range(N_EXPERTS):
        g = e // EXPERTS_PER_GROUP
        cnt = jnp.zeros((1, s), jnp.int32)
        for j in range(EXPERTS_PER_GROUP * g, EXPERTS_PER_GROUP * (g + 1)):
            if j != e:
                ahead = (b[j] >= b[e]) if j < e else (b[j] > b[e])
                cnt = cnt + ahead.astype(jnp.int32)
        first.append((gsel == g) & (cnt == 0))
        second.append((gsel == g) & (cnt == 1))
    s0 = sum(jnp.where(first[e], sc[e], 0.0) for e in range(N_EXPERTS))
    s1 = sum(jnp.where(second[e], sc[e], 0.0) for e in range(N_EXPERTS))
    den = s0 + s1
    w0, w1 = s0 / den, s1 / den
    blk = min(512, s)
    eye = (_iota((blk, blk), 0) == _iota((blk, blk), 1)).astype(BF16)
    rowid = _iota((LANES, blk), 0)
    for j in range(s // blk):
        cols = slice(j * blk, (j + 1) * blk)
        wb = jnp.where(rowid == 0, w0[:, cols], jnp.where(rowid == 1, w1[:, cols], 0.0))
        wcol_ref[0, cols, :] = sum(_dot_nt(eye, part) for part in _split3(wb))
    onehot = jnp.concatenate([(first[e] | second[e]).astype(F32) for e in range(N_EXPERTS)], axis=0)
    before = (_iota((blk, blk), 0) < _iota((blk, blk), 1)).astype(BF16)
    carry, ranks = jnp.zeros((N_EXPERTS, 1), F32), []
    for j in range(s // blk):
        ob = onehot[:, j * blk:(j + 1) * blk]
        ranks.append(_dot(ob.astype(BF16), before) + carry)
        carry = carry + jnp.sum(ob, axis=1, keepdims=True)
    rank = jnp.concatenate(ranks, axis=1)
    ntile = jnp.floor((carry + (tm - 1.0)) * (1.0 / tm))
    lower = (_iota((N_EXPERTS, N_EXPERTS), 1) < _iota((N_EXPERTS, N_EXPERTS), 0)).astype(BF16)
    toff = _dot(lower, jnp.broadcast_to(ntile, (N_EXPERTS, LANES)).astype(BF16))[:, 0:1]
    slot = toff * float(tm) + rank
    pos0 = sum(jnp.where(first[e], slot[e:e + 1, :], 0.0) for e in range(N_EXPERTS))
    pos1 = sum(jnp.where(second[e], slot[e:e + 1, :], 0.0) for e in range(N_EXPERTS))
    pos_ref[0] = jnp.concatenate([pos0, pos1], axis=1).astype(jnp.int32)
    tend = toff + ntile
    tid = _iota((1, LANES), 1).astype(F32)
    texp = jnp.sum((tend <= tid).astype(F32), axis=0, keepdims=True)
    ntot = jnp.broadcast_to(jnp.sum(ntile, axis=0, keepdims=True), (1, LANES))
    diag = _iota((N_EXPERTS, LANES), 0) == _iota((N_EXPERTS, LANES), 1)
    to_lanes = lambda col: jnp.sum(jnp.where(diag, col, 0.0), axis=0, keepdims=True)
    tile_ref[0] = jnp.concatenate([jnp.minimum(texp, N_EXPERTS - 1.0), ntot, to_lanes(carry), to_lanes(toff)],
                                  axis=0).astype(jnp.int32)


def _router(logits, router_b, tm):
    bsz, _, s = logits.shape
    return pl.pallas_call(
        functools.partial(_router_body, tm), grid=(bsz,),
        in_specs=[pl.BlockSpec((1, N_EXPERTS, s), lambda b: (b, 0, 0)),
                  pl.BlockSpec((N_EXPERTS, 1), lambda b: (0, 0))],
        out_specs=[pl.BlockSpec((1, 1, 2 * s), lambda b: (b, 0, 0)),
                   pl.BlockSpec((1, s, LANES), lambda b: (b, 0, 0)),
                   pl.BlockSpec((1, 4, LANES), lambda b: (b, 0, 0))],
        out_shape=[jax.ShapeDtypeStruct((bsz, 1, 2 * s), jnp.int32),
                   jax.ShapeDtypeStruct((bsz, s, LANES), F32),
                   jax.ShapeDtypeStruct((bsz, 4, LANES), jnp.int32)],
        compiler_params=_cp(("arbitrary",)), name="router",
    )(logits, router_b.reshape(N_EXPERTS, 1))


LN_ROWS = 512


def _moe_body(alpha, tm, nt, tiles_ref, hp_ref, x_ref, wcol_ref, gt_ref, pos_ref,
              w1_ref, w3_ref, w2_ref, lng_ref, lnb_ref, o_ref, src, xs, ysall, st0, st1):
    b, i = pl.program_id(0), pl.program_id(1)
    s = hp_ref.shape[1]
    tb = b * (4 * LANES)

    def pad(p, c):
        src[p] = 0
        return c

    def gather(buf, base, lo, hi):
        for r in range(lo, hi):
            xs[buf, pl.ds(r, 1), :] = hp_ref[0, pl.ds(src[base + r], 1), :]

    @pl.when(i == 0)
    def _():
        for e in range(N_EXPERTS):
            cnt, first = tiles_ref[tb + 2 * LANES + e], tiles_ref[tb + 3 * LANES + e]
            lo = first * tm + cnt
            hi = (first + lax.shift_right_logical(cnt + (tm - 1), int(math.log2(tm)))) * tm
            lax.fori_loop(lo, hi, pad, 0)

        def fill(t, c):
            for k in range(2):
                src[pos_ref[0, 0, k * s + t]] = t
            return c
        lax.fori_loop(0, s, fill, 0, unroll=8)
        last = tiles_ref[tb + LANES] * tm
        lax.fori_loop(last, last + tm, pad, 0)
        gather(0, 0, 0, tm)

    @pl.when(i < tiles_ref[tb + LANES])
    def _():
        base = pl.multiple_of(i * tm, tm)
        cur = lax.rem(i, 2)
        f, d = w1_ref.shape[2], w2_ref.shape[2]
        n_piece = 2 * (f // MXU_N) + d // MXU_N
        rows = tm // n_piece
        piece = iter(range(n_piece))

        def gather_ahead():
            p = next(piece)
            gather(1 - cur, base + tm, p * rows, (p + 1) * rows)

        xa, xb = _unpack_bf16_pair(xs[cur])
        x_t = jnp.concatenate([xa, xb], axis=1).astype(BF16)
        up = []
        for w_ref in (w1_ref, w3_ref):
            cols = []
            for j in range(f // MXU_N):
                cols.append(_dot(x_t, w_ref[0, :, j * MXU_N:(j + 1) * MXU_N]))
                gather_ahead()
            up.append(jnp.concatenate(cols, axis=1))
        hid = (_silu(up[0]) * up[1]).astype(BF16)
        ys = []
        for j in range(d // MXU_N):
            ys.append(_dot(hid, w2_ref[0, :, j * MXU_N:(j + 1) * MXU_N]))
            gather_ahead()
        half = d // (2 * MXU_N)
        ysall[pl.ds(base, tm), :] = _pack_bf16_pair(jnp.concatenate(ys[:half], axis=1),
                                                    jnp.concatenate(ys[half:], axis=1))

    @pl.when(i >= nt)
    def _():
        t0 = (i - nt) * LN_ROWS

        def fetch(r, c):
            st0[pl.ds(r, 1), :] = ysall[pl.ds(pos_ref[0, 0, t0 + r], 1), :]
            st1[pl.ds(r, 1), :] = ysall[pl.ds(pos_ref[0, 0, s + t0 + r], 1), :]
            return c
        lax.fori_loop(0, LN_ROWS, fetch, 0, unroll=True)
        a0, b0 = _unpack_bf16_pair(st0[...])
        a1, b1 = _unpack_bf16_pair(st1[...])
        w0, w1 = wcol_ref[0][:, 0:1], wcol_ref[0][:, 1:2]
        y = jnp.concatenate([w0 * a0 + w1 * a1, w0 * b0 + w1 * b1], axis=1)
        o_ref[0] = _layer_norm(alpha * x_ref[0] + gt_ref[0] * y, lng_ref[...], lnb_ref[...])


def _moe(x, gate, hp, pos, wcol, tiles, w1, w3, w2, ln_g, ln_b, alpha, tm):
    bsz, s, d = x.shape
    f = w1.shape[2]
    nt = 2 * s // tm + N_EXPERTS
    assert nt <= LANES and tm & (tm - 1) == 0 and s % LN_ROWS == 0
    blk = lambda i: jnp.maximum(i - nt, 0)
    vec = pl.BlockSpec((1, 1, d), lambda b, i, tl: (b, 0, 0))
    tok = lambda w: pl.BlockSpec((1, LN_ROWS, w), lambda b, i, tl: (b, blk(i), 0))
    row = pl.BlockSpec((1, d), lambda b, i, tl: (0, 0))
    expert = lambda shape: pl.BlockSpec(shape, lambda b, i, tl: (tl[b * (4 * LANES) + jnp.minimum(i, nt - 1)], 0, 0))
    grid_spec = pltpu.PrefetchScalarGridSpec(
        num_scalar_prefetch=1, grid=(bsz, nt + s // LN_ROWS),
        in_specs=[pl.BlockSpec((1, s, d // 2), lambda b, i, tl: (b, 0, 0), pipeline_mode=pl.Buffered(1)),
                  tok(d), tok(LANES), vec,
                  pl.BlockSpec((1, 1, 2 * s), lambda b, i, tl: (b, 0, 0), memory_space=pltpu.SMEM),
                  expert((1, d, f)), expert((1, d, f)), expert((1, f, d)), row, row],
        out_specs=tok(d),
        scratch_shapes=[pltpu.SMEM(((nt + 1) * tm,), jnp.int32), pltpu.VMEM((2, tm, d // 2), jnp.uint32),
                        pltpu.VMEM((nt * tm, d // 2), jnp.uint32),
                        pltpu.VMEM((LN_ROWS, d // 2), jnp.uint32), pltpu.VMEM((LN_ROWS, d // 2), jnp.uint32)])
    return pl.pallas_call(
        functools.partial(_moe_body, alpha, tm, nt), grid_spec=grid_spec,
        out_shape=jax.ShapeDtypeStruct((bsz, s, d), F32),
        compiler_params=_cp(("arbitrary", "arbitrary"), VMEM_LIMIT), name="moe_experts",
    )(tiles.reshape(-1), hp, x, wcol, gate, pos, w1, w3, w2, ln_g.reshape(1, d), ln_b.reshape(1, d))


def kernel(x, c, positions, ada_w, ada_b, w_in, gdn_conv_w, gdn_a_log, gdn_dt_bias, gdn_norm_w, ret_gn_w,
           nsa_cmp_pe, nsa_ck_w1, nsa_ck_w2, nsa_cv_w1, nsa_cv_w2, fox_f_bias, branch_proj, w_gate, w_out,
           ln_g, ln_b, router_w, router_b, exp_w1, exp_w3, exp_w2):
    depth, d = w_in.shape[0], x.shape[-1]
    alpha = (2.0 * depth) ** 0.25
    mod = _ada_mod(c, ada_w, ada_b)
    for l in range(depth):
        shift, sc1p, gate = mod[l, 0, :, :, :d], mod[l, 0, :, :, d:2 * d], mod[l, 0, :, :, 2 * d:]
        wcat, wcat_t = _cat_in_weights(w_in[l])
        (gqkv, gz, r, kc, vc, ks, kw, fq, fk, sm,
         smt, nqt, vst, vwt, fvt) = _in_proj(x, sc1p, shift, wcat, wcat_t)
        o_gdn = _gdn(gqkv, sm, smt, gz, gdn_conv_w[l], gdn_a_log[l], gdn_dt_bias[l], gdn_norm_w[l])
        o_ret = _retention(r, positions, ret_gn_w[l])
        k_cmp, v_cmp_t = _nsa_compress(kc, vc, nsa_cmp_pe[l], nsa_ck_w1[l], nsa_ck_w2[l], nsa_cv_w1[l], nsa_cv_w2[l])
        o_nsa = _nsa_attn(nqt, k_cmp, v_cmp_t, ks, vst, kw, vwt, smt)
        o_fox = _fox_attn(fq, _fox_cum(sm, fk, fox_f_bias[l]), fvt)
        shift2, sc1p2, gate2 = mod[l, 1, :, :, :d], mod[l, 1, :, :, d:2 * d], mod[l, 1, :, :, 2 * d:]
        x, hp, logits = _merge(x, sc1p, shift, gate, o_gdn, o_ret, o_nsa, o_fox, w_gate[l].astype(BF16),
                               branch_proj[l].astype(BF16), w_out[l].astype(BF16), ln_g[l, 0], ln_b[l, 0],
                               sc1p2, shift2, router_w, alpha)
        pos, wcol, tiles = _router(logits, router_b, MOE_TM)
        x = _moe(x, gate2, hp, pos, wcol, tiles, exp_w1[l].astype(BF16), exp_w3[l].astype(BF16),
                 exp_w2[l].astype(BF16), ln_g[l, 1], ln_b[l, 1], alpha, MOE_TM)
    return x
```

```python
import functools
import math

import jax
import jax.numpy as jnp
import numpy as np
from jax import lax
from jax.experimental import pallas as pl
from jax.experimental.pallas import tpu as pltpu

F32 = jnp.float32
BF16 = jnp.bfloat16

N_HEADS = 4
HEAD_DIM = 64
MIX_W = N_HEADS * HEAD_DIM
GDN_CHUNK = 64
CONV_K = 4
RET_CHUNK = 128
ROPE_BASE = 10000.0
NSA_CMP_LEN = 32
NSA_CMP_STRIDE = 16
NSA_SEL_LEN = 64
NSA_TOP_N = 16
NSA_WINDOW = 512
NSA_FORCE_SCORE = 1.0e4
N_EXPERTS = 16
N_GROUPS = 4
EXPERTS_PER_GROUP = N_EXPERTS // N_GROUPS
LN_EPS = 1e-5
LANES = 128
MXU_N = 256
NEG_BIG = -1.0e30
SEL_MASK_BIAS = NEG_BIG
LOG2E = math.log2(math.e)

IN_WIDTHS = (
    MIX_W, MIX_W, MIX_W, N_HEADS, N_HEADS, MIX_W,
    MIX_W, MIX_W, MIX_W, MIX_W,
    MIX_W, HEAD_DIM, HEAD_DIM, HEAD_DIM, HEAD_DIM, HEAD_DIM, HEAD_DIM, 3 * N_HEADS,
    MIX_W, MIX_W, MIX_W, N_HEADS,
)
_IN_OFF = np.concatenate([[0], np.cumsum(IN_WIDTHS)]).astype(int)
(_GQ, _GK, _GV, _GA, _GB, _GZ, _RQ, _RK, _RV, _RG, _NQ, _NKC, _NVC, _NKS, _NVS, _NKW, _NVW, _NGATE,
 _FQ, _FK, _FV, _FF) = range(22)
SM_A, SM_B, SM_GATE, SM_F = 0, 4, 8, 20

VMEM_LIMIT = 56 * 1024 * 1024


def _cp(sem, vmem=None):
    return pltpu.CompilerParams(dimension_semantics=sem, vmem_limit_bytes=vmem)


def _sigmoid(x):
    return 1.0 / (1.0 + jnp.exp(-x))


def _silu(x):
    return x * _sigmoid(x)


def _softplus(x):
    return jnp.maximum(x, 0.0) + jnp.log1p(jnp.exp(-jnp.abs(x)))


def _dot(a, b):
    return jnp.dot(a, b, preferred_element_type=F32)


def _dot_nt(a, b):
    return lax.dot_general(a, b, (((1,), (1,)), ((), ())), preferred_element_type=F32)


def _split3(a):
    a1 = a.astype(BF16)
    r = a - a1.astype(F32)
    a2 = r.astype(BF16)
    a3 = (r - a2.astype(F32)).astype(BF16)
    return a1, a2, a3


def _dot_x(a, m):
    a1, a2, a3 = _split3(a)
    return _dot(a1, m) + _dot(a2, m) + _dot(a3, m)


def _xdot(m, a):
    a1, a2, a3 = _split3(a)
    return _dot(m, a1) + _dot(m, a2) + _dot(m, a3)


def _dot_hp(a, b):
    a1, a2, _ = _split3(a)
    b1, b2, _ = _split3(b)
    return _dot(a1, b1) + (_dot(a1, b2) + _dot(a2, b1))


def _iota(shape, dim):
    return lax.broadcasted_iota(jnp.int32, shape, dim)


def _div(x, n):
    return lax.shift_right_logical(x, int(math.log2(n)))


def _mod(x, n):
    return x & (n - 1)


def _head_ones():
    return _div(_iota((MIX_W, MIX_W), 0), HEAD_DIM) == _div(_iota((MIX_W, MIX_W), 1), HEAD_DIM)


def _head_sum(x, ones):
    return _dot_x(x, ones)


def _ada_body(c_ref, w_ref, b_ref, o_ref):
    mod = _dot_hp(_silu(c_ref[...]), w_ref[0]) + b_ref[0]
    o_ref[0] = jnp.where(pl.program_id(1) == 1, 1.0 + mod, mod)


def _ada_mod(c, ada_w, ada_b):
    depth = ada_w.shape[0]
    bsz, d = c.shape
    n = depth * 2
    w = ada_w.reshape(n, d, 3 * d)
    b = ada_b.reshape(n, 1, 3 * d)
    out = pl.pallas_call(
        _ada_body, grid=(n, 3),
        in_specs=[pl.BlockSpec((bsz, d), lambda i, j: (0, 0)),
                  pl.BlockSpec((1, d, d), lambda i, j: (i, 0, j)),
                  pl.BlockSpec((1, 1, d), lambda i, j: (i, 0, j))],
        out_specs=pl.BlockSpec((1, bsz, d), lambda i, j: (i, 0, j)),
        out_shape=jax.ShapeDtypeStruct((n, bsz, 3 * d), F32),
        compiler_params=_cp(("arbitrary", "arbitrary")), name="ada_mod",
    )(c, w, b)
    return out.reshape(depth, 2, bsz, 1, 3 * d)


IN_TS = 512
_W_GROUPS = (768, 256, 1024, 256, 512, 128)
_W_OFF = np.concatenate([[0], np.cumsum(_W_GROUPS)]).astype(int)
IN_CAT_W = int(_W_OFF[-1])
_WT_GROUPS = (LANES, MIX_W, HEAD_DIM, HEAD_DIM, MIX_W)
_WT_OFF = np.concatenate([[0], np.cumsum(_WT_GROUPS)]).astype(int)
IN_CAT_T = int(_WT_OFF[-1])


def _cat_in_weights(w_in):
    def col(i):
        return w_in[:, _IN_OFF[i]:_IN_OFF[i + 1]]
    small = jnp.concatenate([col(_GA), col(_GB), col(_NGATE), col(_FF)], axis=1)
    small = jnp.pad(small, ((0, 0), (0, LANES - small.shape[1])))
    scale = HEAD_DIM ** -0.5
    cat = jnp.concatenate([
        col(_GQ), col(_GK), col(_GV), col(_GZ),
        col(_RQ), col(_RK) * scale, col(_RV), col(_RG),
        col(_NKC), col(_NVC), col(_NKS), col(_NKW),
        col(_FQ) * (scale * LOG2E), col(_FK),
        small], axis=1)
    cat_t = jnp.concatenate([small, col(_NQ) * (scale * LOG2E), col(_NVS), col(_NVW), col(_FV)], axis=1).T
    return cat.astype(BF16), cat_t.astype(BF16)


def _inproj_body(x_ref, sc_ref, sh_ref, w_ref, wt_ref,
                 gqkv_ref, gz_ref, r_ref, kc_ref, vc_ref, ks_ref, kw_ref, fq_ref, fk_ref, sm_ref,
                 smt_ref, nqt_ref, vst_ref, vwt_ref, fvt_ref):
    h = (x_ref[0] * sc_ref[0] + sh_ref[0]).astype(BF16)

    def proj(g):
        return _dot(h, w_ref[:, _W_OFF[g]:_W_OFF[g + 1]])

    gqkv_ref[0] = proj(0)
    gz_ref[0] = proj(1)
    r_ref[0] = proj(2)
    nkv = proj(3)
    for i, ref in enumerate((kc_ref, vc_ref, ks_ref, kw_ref)):
        ref[0] = nkv[:, i * HEAD_DIM:(i + 1) * HEAD_DIM].astype(ref.dtype)
    f = proj(4)
    for j, ref in enumerate((fq_ref, fk_ref)):
        for hd in range(N_HEADS):
            lo = j * MIX_W + hd * HEAD_DIM
            ref[0, hd] = f[:, lo:lo + HEAD_DIM].astype(BF16)
    sm_ref[0] = proj(5)
    for g, ref in enumerate((smt_ref, nqt_ref, vst_ref, vwt_ref, fvt_ref)):
        ref[0] = _dot_nt(wt_ref[_WT_OFF[g]:_WT_OFF[g + 1], :], h).astype(ref.dtype)


def _in_proj(x, sc1p, shift, wcat, wcat_t):
    bsz, s, d = x.shape
    ts = min(IN_TS, s)
    tok = lambda w, dt: jax.ShapeDtypeStruct((bsz, s, w), dt)
    chan = lambda w, dt: jax.ShapeDtypeStruct((bsz, w, s), dt)
    hm = jax.ShapeDtypeStruct((bsz, N_HEADS, s, HEAD_DIM), BF16)
    out_shape = [tok(768, F32), tok(256, F32), tok(1024, F32),
                 tok(64, F32), tok(64, F32), tok(64, BF16), tok(64, BF16), hm, hm, tok(LANES, F32),
                 chan(LANES, F32), chan(MIX_W, BF16), chan(HEAD_DIM, BF16), chan(HEAD_DIM, BF16), chan(MIX_W, BF16)]
    tspec = lambda w: pl.BlockSpec((1, ts, w), lambda b, i: (b, i, 0))
    cspec = lambda w: pl.BlockSpec((1, w, ts), lambda b, i: (b, 0, i))
    hspec = pl.BlockSpec((1, N_HEADS, ts, HEAD_DIM), lambda b, i: (b, 0, i, 0))
    out_specs = [tspec(768), tspec(256), tspec(1024),
                 tspec(64), tspec(64), tspec(64), tspec(64), hspec, hspec, tspec(LANES),
                 cspec(LANES), cspec(MIX_W), cspec(HEAD_DIM), cspec(HEAD_DIM), cspec(MIX_W)]
    vec = pl.BlockSpec((1, 1, d), lambda b, i: (b, 0, 0))
    return pl.pallas_call(
        _inproj_body, grid=(bsz, s // ts),
        in_specs=[tspec(d), vec, vec,
                  pl.BlockSpec((d, IN_CAT_W), lambda b, i: (0, 0)),
                  pl.BlockSpec((IN_CAT_T, d), lambda b, i: (0, 0))],
        out_specs=out_specs, out_shape=out_shape,
        compiler_params=_cp(("arbitrary", "arbitrary"), VMEM_LIMIT), name="in_proj",
    )(x, sc1p, shift, wcat, wcat_t)


CUM_T = 512


def _foxcum_body(sm_ref, k_ref, brow_ref, kaug_ref, carry_r):
    @pl.when(pl.program_id(1) == 0)
    def _():
        carry_r[...] = jnp.zeros_like(carry_r)

    t = sm_ref.shape[1]

    def log_sigmoid(v):
        return jnp.minimum(v, 0.0) - jnp.log1p(jnp.exp(-jnp.abs(v)))

    lower = (_iota((t, t), 0) >= _iota((t, t), 1)).astype(BF16)
    cc = _xdot(lower, log_sigmoid(sm_ref[0] + brow_ref[...])) + carry_r[...]
    carry_r[...] = cc[t - 1:t, :]
    parts = _split3(-cc * LOG2E)
    place = (_iota((HEAD_DIM, LANES), 0) == _iota((HEAD_DIM, LANES), 1)).astype(BF16)
    src, dst = _iota((LANES, LANES), 0), _iota((LANES, LANES), 1)
    for hd in range(N_HEADS):
        aug = _dot(k_ref[0, hd], place)
        for j, part in enumerate(parts):
            aug = aug + _dot(part, ((src == SM_F + hd) & (dst == HEAD_DIM + j)).astype(BF16))
        kaug_ref[0, hd] = aug.astype(BF16)


def _fox_cum(small, k, f_bias):
    bsz, s, _ = small.shape
    t = min(CUM_T, s)
    brow = jnp.zeros((1, LANES), F32).at[0, SM_F:SM_F + N_HEADS].set(f_bias)
    return pl.pallas_call(
        _foxcum_body, grid=(bsz, s // t),
        in_specs=[pl.BlockSpec((1, t, LANES), lambda b, i: (b, i, 0)),
                  pl.BlockSpec((1, N_HEADS, t, HEAD_DIM), lambda b, i: (b, 0, i, 0)),
                  pl.BlockSpec((1, LANES), lambda b, i: (0, 0))],
        out_specs=pl.BlockSpec((1, N_HEADS, t, LANES), lambda b, i: (b, 0, i, 0)),
        out_shape=jax.ShapeDtypeStruct((bsz, N_HEADS, s, LANES), BF16),
        scratch_shapes=[pltpu.VMEM((1, LANES), F32)],
        compiler_params=_cp(("arbitrary", "arbitrary")), name="fox_cum",
    )(small, k, brow)


FOX_T = 512
FOX_UNROLL = 2


def _lockstep(gens):
    out, live = [None] * len(gens), list(range(len(gens)))
    while live:
        still = []
        for g in live:
            try:
                next(gens[g])
                still.append(g)
            except StopIteration as stop:
                out[g] = stop.value
        live = still
    return out


def _fox_body(q_ref, k_ref, vt_ref, o_ref):
    i = pl.program_id(1)
    t = q_ref.shape[2]
    place = (_iota((HEAD_DIM, LANES), 0) == _iota((HEAD_DIM, LANES), 1)).astype(BF16)
    lane = _iota((1, LANES), 1)
    ones = ((lane >= HEAD_DIM) & (lane < HEAD_DIM + 3)).astype(F32)
    causal = _iota((t, t), 0) <= _iota((t, t), 1)
    q_aug = [(_dot(q_ref[0, hd], place) + ones).astype(BF16) for hd in range(N_HEADS)]

    def head_steps(hd, chunks, carry, masked):
        m, l, acc = carry
        keys = [pl.ds(pl.multiple_of(j * t, t), t) for j in chunks]
        ahead = _dot_nt(k_ref[0, hd, keys[0], :], q_aug[hd])
        yield
        for u in range(len(chunks)):
            s = ahead
            if u + 1 < len(chunks):
                ahead = _dot_nt(k_ref[0, hd, keys[u + 1], :], q_aug[hd])
            if masked:
                s = jnp.where(causal, s, NEG_BIG)
            m_new = jnp.maximum(m, jnp.max(s, axis=0, keepdims=True))
            yield
            alpha = jnp.exp2(m - m_new)
            p = jnp.exp2(s - m_new)
            yield
            m, l = m_new, alpha * l + jnp.sum(p, axis=0, keepdims=True)
            acc = alpha * acc + _dot(vt_ref[0, hd * HEAD_DIM:(hd + 1) * HEAD_DIM, keys[u]], p.astype(BF16))
            yield
        return m, l, acc

    def steps(chunks, carries, masked=False):
        return tuple(_lockstep([head_steps(hd, chunks, carries[hd], masked) for hd in range(N_HEADS)]))

    carries = tuple((jnp.full((1, t), NEG_BIG, F32), jnp.zeros((1, t), F32), jnp.zeros((HEAD_DIM, t), F32))
                    for _ in range(N_HEADS))
    n_groups = _div(i, FOX_UNROLL)
    carries = lax.fori_loop(0, n_groups, lambda g, c: steps([g * FOX_UNROLL + u for u in range(FOX_UNROLL)], c),
                            carries)
    carries = lax.fori_loop(n_groups * FOX_UNROLL, i, lambda j, c: steps([j], c), carries)
    carries = steps([i], carries, masked=True)
    o_ref[0] = jnp.concatenate([acc / l for (_, l, acc) in carries], axis=0).T.astype(o_ref.dtype)


def _fox_attn(q, kaug, vt):
    bsz, nh, s, hd = q.shape
    t = min(FOX_T, s)
    return pl.pallas_call(
        _fox_body, grid=(bsz, s // t),
        in_specs=[pl.BlockSpec((1, nh, t, hd), lambda b, i: (b, 0, i, 0)),
                  pl.BlockSpec((1, nh, s, LANES), lambda b, i: (b, 0, 0, 0)),
                  pl.BlockSpec((1, MIX_W, s), lambda b, i: (b, 0, 0))],
        out_specs=pl.BlockSpec((1, t, MIX_W), lambda b, i: (b, i, 0)),
        out_shape=jax.ShapeDtypeStruct((bsz, s, MIX_W), BF16),
        compiler_params=_cp(("arbitrary", "arbitrary")), name="fox_attn",
    )(q, kaug, vt)


GDN_T = 256
GDN_NB = 4


def _gdn_body(qkv_ref, sm_ref, smt_ref, z_ref, cw_ref, alr_ref, dtr_ref, alc_ref, dtc_ref, nw_ref,
              o_ref, xbuf, state):
    t, c = GDN_T, GDN_CHUNK
    ri, ci = _iota((t, t), 0), _iota((t, t), 1)
    same = _div(ri, c) == _div(ci, c)
    er, ec = _iota((LANES, MIX_W), 0), _div(_iota((LANES, MIX_W), 1), HEAD_DIM)
    head_blk = _head_ones()
    masks = dict(same=same, incl=same & (ci <= ri), strict=same & (ci < ri),
                 same_b=same.astype(BF16), incl_b=(same & (ci <= ri)).astype(BF16),
                 incl_t_b=(same & (ri <= ci)).astype(BF16),
                 exp_a=(er == ec + SM_A).astype(BF16),
                 exp_b=(er == ec + SM_B).astype(BF16),
                 head_blk=head_blk, head_ones=head_blk.astype(BF16),
                 lane_head=_div(_iota((1, MIX_W), 1), HEAD_DIM))
    _lockstep([_gdn_tile(masks, qkv_ref.at[bb], sm_ref.at[bb], smt_ref.at[bb], z_ref.at[bb], cw_ref, alr_ref,
                         dtr_ref, alc_ref, dtc_ref, nw_ref, o_ref.at[bb], xbuf.at[bb], state.at[bb])
               for bb in range(qkv_ref.shape[0])])


def _gdn_tile(masks, qkv_ref, sm_ref, smt_ref, z_ref, cw_ref, alr_ref, dtr_ref, alc_ref, dtc_ref, nw_ref,
              o_ref, xbuf, state):
    t, c = GDN_T, GDN_CHUNK
    incl, strict, head_ones = masks["incl"], masks["strict"], masks["head_ones"]

    @pl.when(pl.program_id(1) == 0)
    def _():
        xbuf[0:8, :] = jnp.zeros((8, 3 * MIX_W), F32)
        state[...] = jnp.zeros_like(state)

    xbuf[8:8 + t, :] = qkv_ref[...]
    conv = cw_ref[0:1, :] * xbuf[pl.ds(8 - CONV_K + 1, t), :]
    for kk in range(1, CONV_K):
        conv = conv + cw_ref[kk:kk + 1, :] * xbuf[pl.ds(8 - CONV_K + 1 + kk, t), :]
    xbuf[0:8, :] = xbuf[t:t + 8, :]
    act = _silu(conv)
    q, k, v = act[:, :MIX_W], act[:, MIX_W:2 * MIX_W], act[:, 2 * MIX_W:]
    q = q * lax.rsqrt(_head_sum(q * q, head_ones) + 1e-6) * HEAD_DIM ** -0.5
    k = k * lax.rsqrt(_head_sum(k * k, head_ones) + 1e-6)

    sm, smt = sm_ref[...], smt_ref[...]
    la_col = -jnp.exp(alr_ref[...]) * _softplus(sm + dtr_ref[...])
    la_row = -jnp.exp(alc_ref[...]) * _softplus(smt + dtc_ref[...])
    beta_col = _sigmoid(sm)
    bc_col = _xdot(masks["incl_b"], la_col)
    last_col = _xdot(masks["same_b"], la_col)
    bc_row = _dot_x(la_row, masks["incl_t_b"])
    exp_a, exp_b = masks["exp_a"], masks["exp_b"]
    bc_x, last_x, beta_x = _dot_x(bc_col, exp_a), _dot_x(last_col, exp_a), _dot_x(beta_col, exp_b)
    eb = jnp.exp(bc_x)
    q_dec = q * eb
    k_dec = k * jnp.exp(last_x - bc_x)
    rhs_v = beta_x * v
    rhs_k = beta_x * eb * k
    rhs = jnp.concatenate([rhs_v, rhs_k], axis=1).astype(BF16)

    kb = k.astype(BF16)
    lane_head = masks["lane_head"]
    heads = range(N_HEADS)
    mh = [lane_head == hd for hd in heads]
    gram = [_dot_nt(jnp.where(mh[hd], k, 0.0).astype(BF16), kb) for hd in heads]
    qk = [_dot_nt(jnp.where(mh[hd], q, 0.0).astype(BF16), kb) for hd in heads]
    yield
    dec = [jnp.exp(jnp.minimum(bc_col[:, SM_A + hd:SM_A + hd + 1] - bc_row[SM_A + hd:SM_A + hd + 1, :], 0.0))
           for hd in heads]
    attn = [jnp.where(incl, qk[hd] * dec[hd], 0.0) for hd in heads]
    p = [jnp.where(strict, -(beta_col[:, SM_B + hd:SM_B + hd + 1] * gram[hd] * dec[hd]), 0.0) for hd in heads]
    tr = list(p)
    for _ in range(5):
        pb = [p[hd].astype(BF16) for hd in heads]
        p = [_dot(pb[hd], pb[hd]) for hd in heads]
        yield
        tr = [tr[hd] + p[hd] + _dot(tr[hd].astype(BF16), p[hd].astype(BF16)) for hd in heads]
    sol = [_dot(tr[hd].astype(BF16), rhs) for hd in heads]
    yield
    u0, w = rhs_v, rhs_k
    for hd in heads:
        u0 = u0 + jnp.where(mh[hd], sol[hd][:, :MIX_W], 0.0)
        w = w + jnp.where(mh[hd], sol[hd][:, MIX_W:], 0.0)

    head_blk = masks["head_blk"]
    k_dec_t = k_dec.T.astype(BF16)
    outs = []
    for n in range(t // c):
        rows = slice(n * c, (n + 1) * c)
        s_old = state[...]
        sb = s_old.astype(BF16)
        u_n = u0[rows] - _dot(w[rows].astype(BF16), sb)
        yield
        parts = ([jnp.zeros((n * c, MIX_W), F32)] if n else []) + [u_n]
        if t - (n + 1) * c:
            parts.append(jnp.zeros((t - (n + 1) * c, MIX_W), F32))
        u_full = jnp.concatenate(parts, axis=0).astype(BF16)
        oa = _dot(jnp.concatenate([a[rows] for a in attn], axis=0).astype(BF16), u_full)
        o_n = _dot(q_dec[rows].astype(BF16), sb)
        for hd in range(N_HEADS):
            o_n = o_n + jnp.where(lane_head == hd, oa[hd * c:(hd + 1) * c], 0.0)
        outs.append(o_n)
        g_row = jnp.exp(last_x[n * c:n * c + 1, :])
        state[...] = g_row * s_old + jnp.where(head_blk, _dot(k_dec_t, u_full), 0.0)
        yield
    o = jnp.concatenate(outs, axis=0)
    o = o * lax.rsqrt(_head_sum(o * o, head_ones) * (1.0 / HEAD_DIM) + 1e-6) * nw_ref[...]
    o_ref[...] = (o * _silu(z_ref[...])).astype(o_ref.dtype)


def _lane_vec(vals, off):
    return jnp.zeros((1, LANES), F32).at[0, off:off + vals.shape[0]].set(vals)


def _gdn(qkv, small, small_t, z, conv_w, a_log, dt_bias, norm_w):
    bsz, s, _ = qkv.shape
    t = GDN_T
    alr, dtr = _lane_vec(a_log, SM_A), _lane_vec(dt_bias, SM_A)
    nw = jnp.tile(norm_w, N_HEADS).reshape(1, MIX_W)
    const = lambda shape: pl.BlockSpec(shape, lambda b, i: (0,) * len(shape))
    nb = GDN_NB if bsz % GDN_NB == 0 else 1
    return pl.pallas_call(
        _gdn_body, grid=(bsz // nb, s // t),
        in_specs=[pl.BlockSpec((nb, t, 3 * MIX_W), lambda b, i: (b, i, 0)),
                  pl.BlockSpec((nb, t, LANES), lambda b, i: (b, i, 0)),
                  pl.BlockSpec((nb, LANES, t), lambda b, i: (b, 0, i)),
                  pl.BlockSpec((nb, t, MIX_W), lambda b, i: (b, i, 0)),
                  const((CONV_K, 3 * MIX_W)), const((1, LANES)), const((1, LANES)),
                  const((LANES, 1)), const((LANES, 1)), const((1, MIX_W))],
        out_specs=pl.BlockSpec((nb, t, MIX_W), lambda b, i: (b, i, 0)),
        out_shape=jax.ShapeDtypeStruct((bsz, s, MIX_W), BF16),
        scratch_shapes=[pltpu.VMEM((nb, t + 8, 3 * MIX_W), F32), pltpu.VMEM((nb, MIX_W, MIX_W), F32)],
        compiler_params=_cp(("arbitrary", "arbitrary")), name="gdn",
    )(qkv, small, small_t, z, conv_w, alr, dtr, alr.reshape(LANES, 1), dtr.reshape(LANES, 1), nw)


RET_T = 1024
_RET_LOG_GAMMA = tuple(math.log1p(-(2.0 ** (-5.0 - h))) for h in range(N_HEADS))


def _per_head(lane_head, vals):
    out = jnp.full(lane_head.shape, vals[0], F32)
    for hd in range(1, N_HEADS):
        out = jnp.where(lane_head == hd, vals[hd], out)
    return out


def _ret_body(r_ref, pos_ref, gnw_ref, o_ref, state):
    t, c = r_ref.shape[1], RET_CHUNK

    @pl.when(pl.program_id(1) == 0)
    def _():
        state[...] = jnp.zeros_like(state)

    x = r_ref[0]
    lane = _iota((1, MIX_W), 1)
    lane_head = _div(lane, HEAD_DIM)
    half = HEAD_DIM // 2
    inv_freq = jnp.exp(_mod(_iota((1, LANES), 1), half).astype(F32) * (-math.log(ROPE_BASE) / half))
    ang = pos_ref[0].astype(F32) * inv_freq
    cos, sin = jnp.cos(ang), jnp.sin(ang)
    cos, sin = jnp.concatenate([cos, cos], axis=1), jnp.concatenate([sin, sin], axis=1)
    first = _mod(lane, HEAD_DIM) < half

    def rope(a):
        rot = jnp.where(first, -pltpu.roll(a, MIX_W - half, 1), pltpu.roll(a, half, 1))
        return a * cos + rot * sin

    q, k = rope(x[:, :MIX_W]), rope(x[:, MIX_W:2 * MIX_W])
    v, g = x[:, 2 * MIX_W:3 * MIX_W], x[:, 3 * MIX_W:]

    lg = _per_head(lane_head, _RET_LOG_GAMMA)
    cidx = _iota((c, 1), 0).astype(F32)
    xi = jnp.exp((cidx + 1.0) * lg)
    zeta = jnp.exp((c - 1.0 - cidx) * lg)
    chunk_decay = jnp.exp(float(c) * lg)
    rel = (_iota((c, c), 0) - _iota((c, c), 1)).astype(F32)
    dec = jnp.concatenate([jnp.where(rel >= 0, jnp.exp(jnp.maximum(rel, 0.0) * _RET_LOG_GAMMA[hd]), 0.0)
                           for hd in range(N_HEADS)], axis=0)
    head_blk = _head_ones()
    outs = []
    for n in range(t // c):
        rows = slice(n * c, (n + 1) * c)
        qn, kn, vn = q[rows], k[rows], v[rows].astype(BF16)
        qs = jnp.concatenate([jnp.where(lane_head == hd, qn, 0.0) for hd in range(N_HEADS)], axis=0).astype(BF16)
        sc = _dot_nt(qs, kn.astype(BF16)) * dec
        res = _dot(sc.astype(BF16), vn)
        s_old = state[...]
        o_n = _dot(qn.astype(BF16), s_old.astype(BF16)) * xi
        for hd in range(N_HEADS):
            o_n = o_n + jnp.where(lane_head == hd, res[hd * c:(hd + 1) * c], 0.0)
        outs.append(o_n)
        kv = _dot((kn * zeta).T.astype(BF16), vn)
        state[...] = chunk_decay * s_old + jnp.where(head_blk, kv, 0.0)
    o = jnp.concatenate(outs, axis=0)
    head_ones = head_blk.astype(BF16)
    mu = _head_sum(o, head_ones) * (1.0 / HEAD_DIM)
    xc = o - mu
    var = _head_sum(xc * xc, head_ones) * (1.0 / HEAD_DIM)
    y = xc * lax.rsqrt(var + LN_EPS) * gnw_ref[...]
    o_ref[0] = (_silu(g) * y).astype(o_ref.dtype)


def _retention(r, positions, gn_w):
    bsz, s, _ = r.shape
    t = min(RET_T, s)
    return pl.pallas_call(
        _ret_body, grid=(bsz, s // t),
        in_specs=[pl.BlockSpec((1, t, 4 * MIX_W), lambda b, i: (b, i, 0)),
                  pl.BlockSpec((1, t, 1), lambda b, i: (b, i, 0)),
                  pl.BlockSpec((1, MIX_W), lambda b, i: (0, 0))],
        out_specs=pl.BlockSpec((1, t, MIX_W), lambda b, i: (b, i, 0)),
        out_shape=jax.ShapeDtypeStruct((bsz, s, MIX_W), BF16),
        scratch_shapes=[pltpu.VMEM((MIX_W, MIX_W), F32)],
        compiler_params=_cp(("arbitrary", "arbitrary")), name="retention",
    )(r, positions.reshape(bsz, s, 1), gn_w.reshape(1, MIX_W))


def _nsacmp_body(kc_ref, vc_ref, pe_ref, kw1_ref, kw2_ref, vw1_ref, vw2_ref, ko_ref, vo_ref):
    half = NSA_CMP_STRIDE * HEAD_DIM
    nc = kc_ref.shape[1]
    def hidden(x_ref, w1_ref):
        x = x_ref[0]
        top = _dot_hp(x, w1_ref[0:half, :])
        bot = _dot_hp(x, w1_ref[half:2 * half, :])
        bias = _dot_hp(pe_ref[...], w1_ref[...])[0:1, :]
        return _silu(top + pltpu.roll(bot, nc - 1, 0) + bias)

    ko_ref[0] = _dot_hp(hidden(kc_ref, kw1_ref), kw2_ref[...])
    a1, a2, _ = _split3(vw2_ref[...])
    b1, b2, _ = _split3(hidden(vc_ref, vw1_ref))
    vo_ref[0] = _dot_nt(a1, b1) + (_dot_nt(a1, b2) + _dot_nt(a2, b1))


def _nsa_compress(kc, vc, pe, ck_w1, ck_w2, cv_w1, cv_w2):
    bsz, s, hd = kc.shape
    nc = s // NSA_CMP_STRIDE
    wide = NSA_CMP_STRIDE * hd
    pe8 = jnp.broadcast_to(pe.reshape(1, NSA_CMP_LEN * hd), (8, NSA_CMP_LEN * hd))
    cv_w2t = cv_w2.T
    const = lambda a: pl.BlockSpec(a.shape, lambda b: (0,) * a.ndim)
    xspec = pl.BlockSpec((1, nc, wide), lambda b: (b, 0, 0))
    return pl.pallas_call(
        _nsacmp_body, grid=(bsz,),
        in_specs=[xspec, xspec, const(pe8), const(ck_w1), const(ck_w2), const(cv_w1), const(cv_w2t)],
        out_specs=[pl.BlockSpec((1, nc, hd), lambda b: (b, 0, 0)), pl.BlockSpec((1, hd, nc), lambda b: (b, 0, 0))],
        out_shape=[jax.ShapeDtypeStruct((bsz, nc, hd), F32), jax.ShapeDtypeStruct((bsz, hd, nc), F32)],
        compiler_params=_cp(("arbitrary",)), name="nsa_compress",
    )(kc.reshape(bsz, nc, wide), vc.reshape(bsz, nc, wide), pe8, ck_w1, ck_w2, cv_w1, cv_w2t)


NSA_TQ = 256
NSA_TK = 512
NSA_UNROLL = 2


def _softmax_cols(s, valid):
    s = jnp.where(valid, s, NEG_BIG)
    m = jnp.max(s, axis=0, keepdims=True)
    m = jnp.where(m > 0.5 * NEG_BIG, m, 0.0)
    e = jnp.exp2(s - m)
    den = jnp.sum(e, axis=0, keepdims=True)
    return e, 1.0 / jnp.where(den > 0.0, den, 1.0)


def _nsa_body(qt_ref, kc_ref, vct_ref, ks_ref, vst_ref, kw_ref, vwt_ref, smt_ref, o_ref, kaug):
    i = pl.program_id(1)
    tq, hd = qt_ref.shape[2], HEAD_DIM
    s, nc = ks_ref.shape[1], kc_ref.shape[1]
    nsel = s // NSA_SEL_LEN
    n_top = min(NSA_TOP_N, nsel)
    cols = N_HEADS * tq

    @pl.when(i == 0)
    def _():
        place = (_iota((hd, LANES), 0) == _iota((hd, LANES), 1)).astype(BF16)
        onehot = (_iota((s, LANES), 1) - hd) == _div(_iota((s, LANES), 0), NSA_SEL_LEN)
        kaug[...] = (_dot(ks_ref[0], place) + onehot.astype(F32)).astype(BF16)

    t0 = i * tq
    qt = qt_ref[0]
    q4 = jnp.concatenate([qt[h * hd:(h + 1) * hd, :] for h in range(N_HEADS)], axis=1)
    tpos = t0 + _iota((1, tq), 1)
    tpos4 = t0 + _mod(_iota((1, cols), 1), tq)

    def cmp_branch():
        sc = _dot(kc_ref[0].astype(BF16), q4)
        yield
        cmp_end = _iota((nc, 1), 0) * NSA_CMP_STRIDE + (NSA_CMP_LEN - 1)
        e, inv = _softmax_cols(sc, cmp_end <= tpos4)
        p = e * inv
        yield
        return p, _dot(vct_ref[0].astype(BF16), p.astype(BF16))

    def win_branch():
        wlen = NSA_WINDOW + tq
        start = pl.multiple_of(jnp.maximum(t0 - NSA_WINDOW, 0), tq)
        sc = _dot(kw_ref[0, pl.ds(start, wlen), :], q4)
        yield
        kpos = start + _iota((wlen, 1), 0)
        e, inv = _softmax_cols(sc, (kpos <= tpos4) & (kpos > tpos4 - NSA_WINDOW))
        yield
        return _dot(vwt_ref[0, :, pl.ds(start, wlen)], e.astype(BF16)) * inv

    (p_cmp, o_cmp), o_win = _lockstep([cmp_branch(), win_branch()])

    psum = p_cmp[:, 0:tq] + p_cmp[:, tq:2 * tq] + p_cmp[:, 2 * tq:3 * tq] + p_cmp[:, 3 * tq:4 * tq]
    ss, cs = _iota((hd, nc), 0) * NSA_SEL_LEN, _iota((hd, nc), 1) * NSA_CMP_STRIDE
    overlap = jnp.clip(jnp.minimum(cs + NSA_CMP_LEN, ss + NSA_SEL_LEN) - jnp.maximum(cs, ss), 0, NSA_CMP_LEN)
    importance = _xdot((overlap.astype(F32) * (1.0 / NSA_CMP_LEN)).astype(BF16), psum)
    blk = _iota((hd, 1), 0)
    cur = _div(tpos, NSA_SEL_LEN)
    forced = (blk == 0) | (blk == cur) | (blk == cur - 1)
    score = jnp.where(forced, NSA_FORCE_SCORE, jnp.where(blk * NSA_SEL_LEN <= tpos, importance, -1.0))
    score = jnp.where(blk < nsel, score, -2.0)
    groups = [score[8 * g:8 * g + 8, :] for g in range(hd // 8)]
    gblk = _iota((8, 1), 0)
    cnts = [jnp.zeros((8, tq), jnp.int32) for _ in groups]
    for j in range(nsel):
        row = score[j:j + 1, :]
        for g in range(hd // 8):
            if 8 * g > j:
                ahead = row >= groups[g]
            elif 8 * g + 7 <= j:
                ahead = row > groups[g]
            else:
                ahead = (row > groups[g]) | ((row == groups[g]) & (gblk + 8 * g > j))
            cnts[g] = cnts[g] + ahead.astype(jnp.int32)
    bias = jnp.where(jnp.concatenate(cnts, axis=0) < n_top, 0.0, SEL_MASK_BIAS).astype(BF16)
    q_aug = jnp.concatenate([q4, jnp.concatenate([bias] * N_HEADS, axis=1)], axis=0)

    def sel_keys(c):
        return pl.ds(pl.multiple_of(c * NSA_TK, NSA_TK), NSA_TK)

    def sel_scores(c):
        return _dot(kaug[sel_keys(c), :], q_aug)

    def sel_absorb(c, sc, carry):
        m, l, acc = carry
        m_new = jnp.maximum(m, jnp.max(sc, axis=0, keepdims=True))
        alpha = jnp.exp2(m - m_new)
        p = jnp.exp2(sc - m_new)
        return (m_new, alpha * l + jnp.sum(p, axis=0, keepdims=True),
                alpha * acc + _dot(vst_ref[0, :, sel_keys(c)], p.astype(BF16)))

    def sel_step(c, carry, masked=False):
        sc = sel_scores(c)
        if masked:
            sc = jnp.where(c * NSA_TK + _iota((NSA_TK, 1), 0) <= tpos4, sc, NEG_BIG)
        return sel_absorb(c, sc, carry)

    def sel_group(g, carry):
        chunks = [g * NSA_UNROLL + u for u in range(NSA_UNROLL)]
        ahead = sel_scores(chunks[0])
        for u, c in enumerate(chunks):
            sc, ahead = ahead, (sel_scores(chunks[u + 1]) if u + 1 < NSA_UNROLL else None)
            carry = sel_absorb(c, sc, carry)
        return carry

    n_full = _div(t0, NSA_TK)
    n_groups = _div(n_full, NSA_UNROLL)
    carry = (jnp.full((1, cols), NEG_BIG, F32), jnp.zeros((1, cols), F32), jnp.zeros((hd, cols), F32))
    carry = lax.fori_loop(0, n_groups, sel_group, carry)
    carry = lax.fori_loop(n_groups * NSA_UNROLL, n_full, sel_step, carry)
    _, l_sel, acc_sel = sel_step(n_full, carry, masked=True)
    o_sel = acc_sel / l_sel

    gates =_sigmoid(smt_ref[0][SM_GATE:SM_GATE + 16, :])
    outs = []
    for h in range(N_HEADS):
        c = slice(h * tq, (h + 1) * tq)
        outs.append(gates[3 * h:3 * h + 1, :] * o_cmp[:, c] + gates[3 * h + 1:3 * h + 2, :] * o_sel[:, c]
                    + gates[3 * h + 2:3 * h + 3, :] * o_win[:, c])
    o_ref[0] = jnp.concatenate(outs, axis=0).T.astype(o_ref.dtype)


def _nsa_attn(qt, k_cmp, v_cmp_t, ks, vst, kw, vwt, small_t):
    bsz, _, s = qt.shape
    tq, hd = NSA_TQ, HEAD_DIM
    assert s % NSA_TK == 0 and NSA_TK % tq == 0 and s >= NSA_WINDOW + tq and s // NSA_SEL_LEN <= hd
    nc = k_cmp.shape[1]
    keys = pl.BlockSpec((1, s, hd), lambda b, i: (b, 0, 0))
    vals = pl.BlockSpec((1, hd, s), lambda b, i: (b, 0, 0))
    return pl.pallas_call(
        _nsa_body, grid=(bsz, s // tq),
        in_specs=[pl.BlockSpec((1, MIX_W, tq), lambda b, i: (b, 0, i)),
                  pl.BlockSpec((1, nc, hd), lambda b, i: (b, 0, 0)),
                  pl.BlockSpec((1, hd, nc), lambda b, i: (b, 0, 0)),
                  keys, vals, keys, vals,
                  pl.BlockSpec((1, LANES, tq), lambda b, i: (b, 0, i))],
        out_specs=pl.BlockSpec((1, tq, MIX_W), lambda b, i: (b, i, 0)),
        out_shape=jax.ShapeDtypeStruct((bsz, s, MIX_W), BF16),
        scratch_shapes=[pltpu.VMEM((s, LANES), BF16)],
        compiler_params=_cp(("arbitrary", "arbitrary")), name="nsa_attn",
    )(qt, k_cmp, v_cmp_t, ks, vst, kw, vwt, small_t)


MERGE_T = 1024


def _layer_norm(r, g, b):
    mu = jnp.mean(r, axis=-1, keepdims=True)
    xc = r - mu
    var = jnp.mean(xc * xc, axis=-1, keepdims=True)
    return xc * lax.rsqrt(var + LN_EPS) * g + b


def _merge_body(alpha, x_ref, sc_ref, sh_ref, gt_ref, og_ref, or_ref, on_ref, of_ref, wg_ref, bp_ref, wo_ref,
                lng_ref, lnb_ref, sc2_ref, sh2_ref, rwt_ref, o_ref, hp_ref, lg_ref):
    x = x_ref[0]
    h = (x * sc_ref[0] + sh_ref[0]).astype(BF16)

    merged = None
    for br, o_br in enumerate((og_ref, or_ref, on_ref, of_ref)):
        term = _sigmoid(_dot(h, wg_ref[br])) * _dot(o_br[0], bp_ref[br])
        merged = term if merged is None else merged + term
    y = _dot(merged.astype(BF16), wo_ref[...])
    x_new = _layer_norm(alpha * x + gt_ref[0] * y, lng_ref[...], lnb_ref[...])
    o_ref[0] = x_new
    h2 = x_new * sc2_ref[0] + sh2_ref[0]
    half = h2.shape[1] // 2
    hp_ref[0] = _pack_bf16_pair(h2[:, :half], h2[:, half:])
    a1, a2, _ = _split3(rwt_ref[...])
    b1, b2, _ = _split3(h2)
    lg_ref[0] = _dot_nt(a1, b1) + (_dot_nt(a1, b2) + _dot_nt(a2, b1))


def _merge(x, sc1p, shift, gate, o_gdn, o_ret, o_nsa, o_fox, w_gate, branch_proj, w_out, ln_g, ln_b,
           sc1p_moe, shift_moe, router_w, alpha):
    bsz, s, d = x.shape
    t = min(MERGE_T, s)
    tok = lambda w: pl.BlockSpec((1, t, w), lambda b, i: (b, i, 0))
    vec = pl.BlockSpec((1, 1, d), lambda b, i: (b, 0, 0))
    const = lambda a: pl.BlockSpec(a.shape, lambda b, i: (0,) * a.ndim, pipeline_mode=pl.Buffered(1))
    lng, lnb, rwt = ln_g.reshape(1, d), ln_b.reshape(1, d), router_w.T
    return pl.pallas_call(
        functools.partial(_merge_body, alpha), grid=(bsz, s // t),
        in_specs=[tok(d), vec, vec, vec, tok(MIX_W), tok(MIX_W), tok(MIX_W), tok(MIX_W),
                  const(w_gate), const(branch_proj), const(w_out), const(lng), const(lnb), vec, vec, const(rwt)],
        out_specs=[tok(d), tok(d // 2), pl.BlockSpec((1, N_EXPERTS, t), lambda b, i: (b, 0, i))],
        out_shape=[jax.ShapeDtypeStruct((bsz, s, d), F32), jax.ShapeDtypeStruct((bsz, s, d // 2), jnp.uint32),
                   jax.ShapeDtypeStruct((bsz, N_EXPERTS, s), F32)],
        compiler_params=_cp(("arbitrary", "arbitrary"), VMEM_LIMIT), name="merge",
    )(x, sc1p, shift, gate, o_gdn, o_ret, o_nsa, o_fox, w_gate, branch_proj, w_out, lng, lnb,
      sc1p_moe, shift_moe, rwt)


MOE_TM = 256


def _pack_bf16_pair(a, b):
    hi = lax.bitcast_convert_type(a.astype(BF16).astype(F32), jnp.uint32)
    lo = lax.bitcast_convert_type(b.astype(BF16).astype(F32), jnp.uint32)
    return hi | lax.shift_right_logical(lo, jnp.uint32(16))


def _unpack_bf16_pair(w):
    a = lax.bitcast_convert_type(w & jnp.uint32(0xFFFF0000), F32)
    b = lax.bitcast_convert_type(lax.shift_left(w, jnp.uint32(16)), F32)
    return a, b


def _router_body(tm, lg_ref, rb_ref, pos_ref, wcol_ref, tile_ref):
    s = lg_ref.shape[2]
    scores = _sigmoid(lg_ref[0])
    biased = scores + rb_ref[...]
    b = [biased[e:e + 1, :] for e in range(N_EXPERTS)]
    sc = [scores[e:e + 1, :] for e in range(N_EXPERTS)]
    gs = []
    for g in range(N_GROUPS):
        m = [b[EXPERTS_PER_GROUP * g + j] for j in range(EXPERTS_PER_GROUP)]
        best = m[0] + m[1]
        for u in range(EXPERTS_PER_GROUP):
            for v in range(u + 1, EXPERTS_PER_GROUP):
                if (u, v) != (0, 1):
                    best = jnp.maximum(best, m[u] + m[v])
        gs.append(best)
    gsel, best = jnp.zeros((1, s), jnp.int32), gs[0]
    for g in range(1, N_GROUPS):
        take = gs[g] > best
        gsel = jnp.where(take, g, gsel)
        best = jnp.where(take, gs[g], best)
    first, second = [], []
    for e in range(N_EXPERTS):
        g = e // EXPERTS_PER_GROUP
        cnt = jnp.zeros((1, s), jnp.int32)
        for j in range(EXPERTS_PER_GROUP * g, EXPERTS_PER_GROUP * (g + 1)):
            if j != e:
                ahead = (b[j] >= b[e]) if j < e else (b[j] > b[e])
                cnt = cnt + ahead.astype(jnp.int32)
        first.append((gsel == g) & (cnt == 0))
        second.append((gsel == g) & (cnt == 1))
    s0 = sum(jnp.where(first[e], sc[e], 0.0) for e in range(N_EXPERTS))
    s1 = sum(jnp.where(second[e], sc[e], 0.0) for e in range(N_EXPERTS))
    den = s0 + s1
    w0, w1 = s0 / den, s1 / den
    blk = min(512, s)
    eye = (_iota((blk, blk), 0) == _iota((blk, blk), 1)).astype(BF16)
    rowid = _iota((LANES, blk), 0)
    for j in range(s // blk):
        cols = slice(j * blk, (j + 1) * blk)
        wb = jnp.where(rowid == 0, w0[:, cols], jnp.where(rowid == 1, w1[:, cols], 0.0))
        wcol_ref[0, cols, :] = sum(_dot_nt(eye, part) for part in _split3(wb))
    onehot = jnp.concatenate([(first[e] | second[e]).astype(F32) for e in range(N_EXPERTS)], axis=0)
    before = (_iota((blk, blk), 0) < _iota((blk, blk), 1)).astype(BF16)
    carry, ranks = jnp.zeros((N_EXPERTS, 1), F32), []
    for j in range(s // blk):
        ob = onehot[:, j * blk:(j + 1) * blk]
        ranks.append(_dot(ob.astype(BF16), before) + carry)
        carry = carry + jnp.sum(ob, axis=1, keepdims=True)
    rank = jnp.concatenate(ranks, axis=1)
    ntile = jnp.floor((carry + (tm - 1.0)) * (1.0 / tm))
    lower = (_iota((N_EXPERTS, N_EXPERTS), 1) < _iota((N_EXPERTS, N_EXPERTS), 0)).astype(BF16)
    toff = _dot(lower, jnp.broadcast_to(ntile, (N_EXPERTS, LANES)).astype(BF16))[:, 0:1]
    slot = toff * float(tm) + rank
    pos0 = sum(jnp.where(first[e], slot[e:e + 1, :], 0.0) for e in range(N_EXPERTS))
    pos1 = sum(jnp.where(second[e], slot[e:e + 1, :], 0.0) for e in range(N_EXPERTS))
    pos_ref[0] = jnp.concatenate([pos0, pos1], axis=1).astype(jnp.int32)
    tend = toff + ntile
    tid = _iota((1, LANES), 1).astype(F32)
    texp = jnp.sum((tend <= tid).astype(F32), axis=0, keepdims=True)
    ntot = jnp.broadcast_to(jnp.sum(ntile, axis=0, keepdims=True), (1, LANES))
    diag = _iota((N_EXPERTS, LANES), 0) == _iota((N_EXPERTS, LANES), 1)
    to_lanes = lambda col: jnp.sum(jnp.where(diag, col, 0.0), axis=0, keepdims=True)
    tile_ref[0] = jnp.concatenate([jnp.minimum(texp, N_EXPERTS - 1.0), ntot, to_lanes(carry), to_lanes(toff)],
                                  axis=0).astype(jnp.int32)


def _router(logits, router_b, tm):
    bsz, _, s = logits.shape
    return pl.pallas_call(
        functools.partial(_router_body, tm), grid=(bsz,),
        in_specs=[pl.BlockSpec((1, N_EXPERTS, s), lambda b: (b, 0, 0)),
                  pl.BlockSpec((N_EXPERTS, 1), lambda b: (0, 0))],
        out_specs=[pl.BlockSpec((1, 1, 2 * s), lambda b: (b, 0, 0)),
                   pl.BlockSpec((1, s, LANES), lambda b: (b, 0, 0)),
                   pl.BlockSpec((1, 4, LANES), lambda b: (b, 0, 0))],
        out_shape=[jax.ShapeDtypeStruct((bsz, 1, 2 * s), jnp.int32),
                   jax.ShapeDtypeStruct((bsz, s, LANES), F32),
                   jax.ShapeDtypeStruct((bsz, 4, LANES), jnp.int32)],
        compiler_params=_cp(("arbitrary",)), name="router",
    )(logits, router_b.reshape(N_EXPERTS, 1))


LN_ROWS = 512


def _moe_body(alpha, tm, nt, tiles_ref, hp_ref, x_ref, wcol_ref, gt_ref, pos_ref,
              w1_ref, w3_ref, w2_ref, lng_ref, lnb_ref, o_ref, src, xs, ysall, st0, st1):
    b, i = pl.program_id(0), pl.program_id(1)
    s = hp_ref.shape[1]
    tb = b * (4 * LANES)

    def pad(p, c):
        src[p] = 0
        return c

    def gather(buf, base, lo, hi):
        for r in range(lo, hi):
            xs[buf, pl.ds(r, 1), :] = hp_ref[0, pl.ds(src[base + r], 1), :]

    @pl.when(i == 0)
    def _():
        for e in range(N_EXPERTS):
            cnt, first = tiles_ref[tb + 2 * LANES + e], tiles_ref[tb + 3 * LANES + e]
            lo = first * tm + cnt
            hi = (first + lax.shift_right_logical(cnt + (tm - 1), int(math.log2(tm)))) * tm
            lax.fori_loop(lo, hi, pad, 0)

        def fill(t, c):
            for k in range(2):
                src[pos_ref[0, 0, k * s + t]] = t
            return c
        lax.fori_loop(0, s, fill, 0, unroll=8)
        last = tiles_ref[tb + LANES] * tm
        lax.fori_loop(last, last + tm, pad, 0)
        gather(0, 0, 0, tm)

    @pl.when(i < tiles_ref[tb + LANES])
    def _():
        base = pl.multiple_of(i * tm, tm)
        cur = lax.rem(i, 2)
        f, d = w1_ref.shape[2], w2_ref.shape[2]
        n_piece = 2 * (f // MXU_N) + d // MXU_N
        rows = tm // n_piece
        piece = iter(range(n_piece))

        def gather_ahead():
            p = next(piece)
            gather(1 - cur, base + tm, p * rows, (p + 1) * rows)

        xa, xb = _unpack_bf16_pair(xs[cur])
        x_t = jnp.concatenate([xa, xb], axis=1).astype(BF16)
        up = []
        for w_ref in (w1_ref, w3_ref):
            cols = []
            for j in range(f // MXU_N):
                cols.append(_dot(x_t, w_ref[0, :, j * MXU_N:(j + 1) * MXU_N]))
                gather_ahead()
            up.append(jnp.concatenate(cols, axis=1))
        hid = (_silu(up[0]) * up[1]).astype(BF16)
        ys = []
        for j in range(d // MXU_N):
            ys.append(_dot(hid, w2_ref[0, :, j * MXU_N:(j + 1) * MXU_N]))
            gather_ahead()
        half = d // (2 * MXU_N)
        ysall[pl.ds(base, tm), :] = _pack_bf16_pair(jnp.concatenate(ys[:half], axis=1),
                                                    jnp.concatenate(ys[half:], axis=1))

    @pl.when(i >= nt)
    def _():
        t0 = (i - nt) * LN_ROWS

        def fetch(r, c):
            st0[pl.ds(r, 1), :] = ysall[pl.ds(pos_ref[0, 0, t0 + r], 1), :]
            st1[pl.ds(r, 1), :] = ysall[pl.ds(pos_ref[0, 0, s + t0 + r], 1), :]
            return c
        lax.fori_loop(0, LN_ROWS, fetch, 0, unroll=True)
        a0, b0 = _unpack_bf16_pair(st0[...])
        a1, b1 = _unpack_bf16_pair(st1[...])
        w0, w1 = wcol_ref[0][:, 0:1], wcol_ref[0][:, 1:2]
        y = jnp.concatenate([w0 * a0 + w1 * a1, w0 * b0 + w1 * b1], axis=1)
        o_ref[0] = _layer_norm(alpha * x_ref[0] + gt_ref[0] * y, lng_ref[...], lnb_ref[...])


def _moe(x, gate, hp, pos, wcol, tiles, w1, w3, w2, ln_g, ln_b, alpha, tm):
    bsz, s, d = x.shape
    f = w1.shape[2]
    nt = 2 * s // tm + N_EXPERTS
    assert nt <= LANES and tm & (tm - 1) == 0 and s % LN_ROWS == 0
    blk = lambda i: jnp.maximum(i - nt, 0)
    vec = pl.BlockSpec((1, 1, d), lambda b, i, tl: (b, 0, 0))
    tok = lambda w: pl.BlockSpec((1, LN_ROWS, w), lambda b, i, tl: (b, blk(i), 0))
    row = pl.BlockSpec((1, d), lambda b, i, tl: (0, 0))
    expert = lambda shape: pl.BlockSpec(shape, lambda b, i, tl: (tl[b * (4 * LANES) + jnp.minimum(i, nt - 1)], 0, 0))
    grid_spec = pltpu.PrefetchScalarGridSpec(
        num_scalar_prefetch=1, grid=(bsz, nt + s // LN_ROWS),
        in_specs=[pl.BlockSpec((1, s, d // 2), lambda b, i, tl: (b, 0, 0), pipeline_mode=pl.Buffered(1)),
                  tok(d), tok(LANES), vec,
                  pl.BlockSpec((1, 1, 2 * s), lambda b, i, tl: (b, 0, 0), memory_space=pltpu.SMEM),
                  expert((1, d, f)), expert((1, d, f)), expert((1, f, d)), row, row],
        out_specs=tok(d),
        scratch_shapes=[pltpu.SMEM(((nt + 1) * tm,), jnp.int32), pltpu.VMEM((2, tm, d // 2), jnp.uint32),
                        pltpu.VMEM((nt * tm, d // 2), jnp.uint32),
                        pltpu.VMEM((LN_ROWS, d // 2), jnp.uint32), pltpu.VMEM((LN_ROWS, d // 2), jnp.uint32)])
    return pl.pallas_call(
        functools.partial(_moe_body, alpha, tm, nt), grid_spec=grid_spec,
        out_shape=jax.ShapeDtypeStruct((bsz, s, d), F32),
        compiler_params=_cp(("arbitrary", "arbitrary"), VMEM_LIMIT), name="moe_experts",
    )(tiles.reshape(-1), hp, x, wcol, gate, pos, w1, w3, w2, ln_g.reshape(1, d), ln_b.reshape(1, d))


def kernel(x, c, positions, ada_w, ada_b, w_in, gdn_conv_w, gdn_a_log, gdn_dt_bias, gdn_norm_w, ret_gn_w,
           nsa_cmp_pe, nsa_ck_w1, nsa_ck_w2, nsa_cv_w1, nsa_cv_w2, fox_f_bias, branch_proj, w_gate, w_out,
           ln_g, ln_b, router_w, router_b, exp_w1, exp_w3, exp_w2):
    depth, d = w_in.shape[0], x.shape[-1]
    alpha = (2.0 * depth) ** 0.25
    mod = _ada_mod(c, ada_w, ada_b)
    for l in range(depth):
        shift, sc1p, gate = mod[l, 0, :, :, :d], mod[l, 0, :, :, d:2 * d], mod[l, 0, :, :, 2 * d:]
        wcat, wcat_t = _cat_in_weights(w_in[l])
        (gqkv, gz, r, kc, vc, ks, kw, fq, fk, sm,
         smt, nqt, vst, vwt, fvt) = _in_proj(x, sc1p, shift, wcat, wcat_t)
        o_gdn = _gdn(gqkv, sm, smt, gz, gdn_conv_w[l], gdn_a_log[l], gdn_dt_bias[l], gdn_norm_w[l])
        o_ret = _retention(r, positions, ret_gn_w[l])
        k_cmp, v_cmp_t = _nsa_compress(kc, vc, nsa_cmp_pe[l], nsa_ck_w1[l], nsa_ck_w2[l], nsa_cv_w1[l], nsa_cv_w2[l])
        o_nsa = _nsa_attn(nqt, k_cmp, v_cmp_t, ks, vst, kw, vwt, smt)
        o_fox = _fox_attn(fq, _fox_cum(sm, fk, fox_f_bias[l]), fvt)
        shift2, sc1p2, gate2 = mod[l, 1, :, :, :d], mod[l, 1, :, :, d:2 * d], mod[l, 1, :, :, 2 * d:]
        x, hp, logits = _merge(x, sc1p, shift, gate, o_gdn, o_ret, o_nsa, o_fox, w_gate[l].astype(BF16),
                               branch_proj[l].astype(BF16), w_out[l].astype(BF16), ln_g[l, 0], ln_b[l, 0],
                               sc1p2, shift2, router_w, alpha)
        pos, wcol, tiles = _router(logits, router_b, MOE_TM)
        x = _moe(x, gate2, hp, pos, wcol, tiles, exp_w1[l].astype(BF16), exp_w3[l].astype(BF16),
                 exp_w2[l].astype(BF16), ln_g[l, 1], ln_b[l, 1], alpha, MOE_TM)
    return x
```

```python
import functools
import math

import jax
import jax.numpy as jnp
import numpy as np
from jax import lax
from jax.experimental import pallas as pl
from jax.experimental.pallas import tpu as pltpu

F32 = jnp.float32
BF16 = jnp.bfloat16

N_HEADS = 4
HEAD_DIM = 64
MIX_W = N_HEADS * HEAD_DIM
GDN_CHUNK = 64
CONV_K = 4
RET_CHUNK = 128
ROPE_BASE = 10000.0
NSA_CMP_LEN = 32
NSA_CMP_STRIDE = 16
NSA_SEL_LEN = 64
NSA_TOP_N = 16
NSA_WINDOW = 512
NSA_FORCE_SCORE = 1.0e4
N_EXPERTS = 16
N_GROUPS = 4
EXPERTS_PER_GROUP = N_EXPERTS // N_GROUPS
LN_EPS = 1e-5
LANES = 128
MXU_N = 256
SUM_ROWS = 16
NEG_BIG = -1.0e30
SEL_MASK_BIAS = NEG_BIG
LOG2E = math.log2(math.e)

IN_WIDTHS = (
    MIX_W, MIX_W, MIX_W, N_HEADS, N_HEADS, MIX_W,
    MIX_W, MIX_W, MIX_W, MIX_W,
    MIX_W, HEAD_DIM, HEAD_DIM, HEAD_DIM, HEAD_DIM, HEAD_DIM, HEAD_DIM, 3 * N_HEADS,
    MIX_W, MIX_W, MIX_W, N_HEADS,
)
_IN_OFF = np.concatenate([[0], np.cumsum(IN_WIDTHS)]).astype(int)
(_GQ, _GK, _GV, _GA, _GB, _GZ, _RQ, _RK, _RV, _RG, _NQ, _NKC, _NVC, _NKS, _NVS, _NKW, _NVW, _NGATE,
 _FQ, _FK, _FV, _FF) = range(22)
SM_A, SM_B, SM_GATE, SM_F = 0, 4, 8, 20

VMEM_LIMIT = 56 * 1024 * 1024


def _cp(sem, vmem=None):
    return pltpu.CompilerParams(dimension_semantics=sem, vmem_limit_bytes=vmem)


def _sigmoid(x):
    return 1.0 / (1.0 + jnp.exp(-x))


def _silu(x):
    return x * _sigmoid(x)


def _softplus(x):
    return jnp.maximum(x, 0.0) + jnp.log1p(jnp.exp(-jnp.abs(x)))


def _dot(a, b):
    return jnp.dot(a, b, preferred_element_type=F32)


def _dot_nt(a, b):
    return lax.dot_general(a, b, (((1,), (1,)), ((), ())), preferred_element_type=F32)


def _split3(a):
    a1 = a.astype(BF16)
    r = a - a1.astype(F32)
    a2 = r.astype(BF16)
    a3 = (r - a2.astype(F32)).astype(BF16)
    return a1, a2, a3


def _dot_x(a, m):
    a1, a2, a3 = _split3(a)
    return _dot(a1, m) + _dot(a2, m) + _dot(a3, m)


def _xdot(m, a):
    a1, a2, a3 = _split3(a)
    return _dot(m, a1) + _dot(m, a2) + _dot(m, a3)


def _dot_hp(a, b):
    a1, a2, _ = _split3(a)
    b1, b2, _ = _split3(b)
    return _dot(a1, b1) + (_dot(a1, b2) + _dot(a2, b1))


def _iota(shape, dim):
    return lax.broadcasted_iota(jnp.int32, shape, dim)


def _div(x, n):
    return lax.shift_right_logical(x, int(math.log2(n)))


def _mod(x, n):
    return x & (n - 1)


def _head_ones():
    return _div(_iota((MIX_W, MIX_W), 0), HEAD_DIM) == _div(_iota((MIX_W, MIX_W), 1), HEAD_DIM)


def _head_sum(x, ones):
    a1, a2, _ = _split3(x)
    return _dot(a1, ones) + _dot(a2, ones)


def _ada_body(c_ref, w_ref, b_ref, o_ref):
    mod = _dot_hp(_silu(c_ref[...]), w_ref[0]) + b_ref[0]
    o_ref[0] = jnp.where(pl.program_id(1) == 1, 1.0 + mod, mod)


def _ada_mod(c, ada_w, ada_b):
    depth = ada_w.shape[0]
    bsz, d = c.shape
    n = depth * 2
    w = ada_w.reshape(n, d, 3 * d)
    b = ada_b.reshape(n, 1, 3 * d)
    out = pl.pallas_call(
        _ada_body, grid=(n, 3),
        in_specs=[pl.BlockSpec((bsz, d), lambda i, j: (0, 0)),
                  pl.BlockSpec((1, d, d), lambda i, j: (i, 0, j)),
                  pl.BlockSpec((1, 1, d), lambda i, j: (i, 0, j))],
        out_specs=pl.BlockSpec((1, bsz, d), lambda i, j: (i, 0, j)),
        out_shape=jax.ShapeDtypeStruct((n, bsz, 3 * d), F32),
        compiler_params=_cp(("arbitrary", "arbitrary")), name="ada_mod",
    )(c, w, b)
    return out.reshape(depth, 2, bsz, 1, 3 * d)


IN_TS = 512
_W_GROUPS = (768, 256, 1024, 256, 512, 128)
_W_OFF = np.concatenate([[0], np.cumsum(_W_GROUPS)]).astype(int)
IN_CAT_W = int(_W_OFF[-1])
_WT_GROUPS = (LANES, MIX_W, HEAD_DIM, HEAD_DIM, MIX_W)
_WT_OFF = np.concatenate([[0], np.cumsum(_WT_GROUPS)]).astype(int)
IN_CAT_T = int(_WT_OFF[-1])


def _cat_in_weights(w_in):
    def col(i):
        return w_in[:, _IN_OFF[i]:_IN_OFF[i + 1]]
    small = jnp.concatenate([col(_GA), col(_GB), col(_NGATE), col(_FF)], axis=1)
    small = jnp.pad(small, ((0, 0), (0, LANES - small.shape[1])))
    scale = HEAD_DIM ** -0.5
    cat = jnp.concatenate([
        col(_GQ), col(_GK), col(_GV), col(_GZ),
        col(_RQ), col(_RK) * scale, col(_RV), col(_RG),
        col(_NKC), col(_NVC), col(_NKS), col(_NKW),
        col(_FQ) * (scale * LOG2E), col(_FK),
        small], axis=1)
    cat_t = jnp.concatenate([small, col(_NQ) * (scale * LOG2E), col(_NVS), col(_NVW), col(_FV)], axis=1).T
    return cat.astype(BF16), cat_t.astype(BF16)


def _inproj_body(x_ref, sc_ref, sh_ref, w_ref, wt_ref,
                 gqkv_ref, gz_ref, r_ref, kc_ref, vc_ref, ks_ref, kw_ref, fq_ref, fk_ref, sm_ref,
                 smt_ref, nqt_ref, vst_ref, vwt_ref, fvt_ref):
    h = (x_ref[0] * sc_ref[0] + sh_ref[0]).astype(BF16)

    def proj(g):
        return _dot(h, w_ref[:, _W_OFF[g]:_W_OFF[g + 1]])

    gqkv_ref[0] = proj(0)
    gz_ref[0] = proj(1)
    r_ref[0] = proj(2)
    nkv = proj(3)
    for i, ref in enumerate((kc_ref, vc_ref, ks_ref, kw_ref)):
        ref[0] = nkv[:, i * HEAD_DIM:(i + 1) * HEAD_DIM].astype(ref.dtype)
    f = proj(4)
    for j, ref in enumerate((fq_ref, fk_ref)):
        for hd in range(N_HEADS):
            lo = j * MIX_W + hd * HEAD_DIM
            ref[0, hd] = f[:, lo:lo + HEAD_DIM].astype(BF16)
    sm_ref[0] = proj(5)
    for g, ref in enumerate((smt_ref, nqt_ref, vst_ref, vwt_ref, fvt_ref)):
        ref[0] = _dot_nt(wt_ref[_WT_OFF[g]:_WT_OFF[g + 1], :], h).astype(ref.dtype)


def _in_proj(x, sc1p, shift, wcat, wcat_t):
    bsz, s, d = x.shape
    ts = min(IN_TS, s)
    tok = lambda w, dt: jax.ShapeDtypeStruct((bsz, s, w), dt)
    chan = lambda w, dt: jax.ShapeDtypeStruct((bsz, w, s), dt)
    hm = jax.ShapeDtypeStruct((bsz, N_HEADS, s, HEAD_DIM), BF16)
    out_shape = [tok(768, F32), tok(256, F32), tok(1024, F32),
                 tok(64, F32), tok(64, F32), tok(64, BF16), tok(64, BF16), hm, hm, tok(LANES, F32),
                 chan(LANES, F32), chan(MIX_W, BF16), chan(HEAD_DIM, BF16), chan(HEAD_DIM, BF16), chan(MIX_W, BF16)]
    tspec = lambda w: pl.BlockSpec((1, ts, w), lambda b, i: (b, i, 0))
    cspec = lambda w: pl.BlockSpec((1, w, ts), lambda b, i: (b, 0, i))
    hspec = pl.BlockSpec((1, N_HEADS, ts, HEAD_DIM), lambda b, i: (b, 0, i, 0))
    out_specs = [tspec(768), tspec(256), tspec(1024),
                 tspec(64), tspec(64), tspec(64), tspec(64), hspec, hspec, tspec(LANES),
                 cspec(LANES), cspec(MIX_W), cspec(HEAD_DIM), cspec(HEAD_DIM), cspec(MIX_W)]
    vec = pl.BlockSpec((1, 1, d), lambda b, i: (b, 0, 0))
    return pl.pallas_call(
        _inproj_body, grid=(bsz, s // ts),
        in_specs=[tspec(d), vec, vec,
                  pl.BlockSpec((d, IN_CAT_W), lambda b, i: (0, 0)),
                  pl.BlockSpec((IN_CAT_T, d), lambda b, i: (0, 0))],
        out_specs=out_specs, out_shape=out_shape,
        compiler_params=_cp(("arbitrary", "arbitrary"), VMEM_LIMIT), name="in_proj",
    )(x, sc1p, shift, wcat, wcat_t)


CUM_T = 512


def _foxcum_body(sm_ref, k_ref, brow_ref, kaug_ref, carry_r):
    @pl.when(pl.program_id(1) == 0)
    def _():
        carry_r[...] = jnp.zeros_like(carry_r)

    t = sm_ref.shape[1]

    def log_sigmoid(v):
        return jnp.minimum(v, 0.0) - jnp.log1p(jnp.exp(-jnp.abs(v)))

    lower = (_iota((t, t), 0) >= _iota((t, t), 1)).astype(BF16)
    cc = _xdot(lower, log_sigmoid(sm_ref[0] + brow_ref[...])) + carry_r[...]
    carry_r[...] = cc[t - 1:t, :]
    parts = _split3(-cc * LOG2E)
    place = (_iota((HEAD_DIM, LANES), 0) == _iota((HEAD_DIM, LANES), 1)).astype(BF16)
    src, dst = _iota((LANES, LANES), 0), _iota((LANES, LANES), 1)
    for hd in range(N_HEADS):
        aug = _dot(k_ref[0, hd], place)
        for j, part in enumerate(parts):
            aug = aug + _dot(part, ((src == SM_F + hd) & (dst == HEAD_DIM + j)).astype(BF16))
        kaug_ref[0, hd] = aug.astype(BF16)


def _fox_cum(small, k, f_bias):
    bsz, s, _ = small.shape
    t = min(CUM_T, s)
    brow = jnp.zeros((1, LANES), F32).at[0, SM_F:SM_F + N_HEADS].set(f_bias)
    return pl.pallas_call(
        _foxcum_body, grid=(bsz, s // t),
        in_specs=[pl.BlockSpec((1, t, LANES), lambda b, i: (b, i, 0)),
                  pl.BlockSpec((1, N_HEADS, t, HEAD_DIM), lambda b, i: (b, 0, i, 0)),
                  pl.BlockSpec((1, LANES), lambda b, i: (0, 0))],
        out_specs=pl.BlockSpec((1, N_HEADS, t, LANES), lambda b, i: (b, 0, i, 0)),
        out_shape=jax.ShapeDtypeStruct((bsz, N_HEADS, s, LANES), BF16),
        scratch_shapes=[pltpu.VMEM((1, LANES), F32)],
        compiler_params=_cp(("arbitrary", "arbitrary")), name="fox_cum",
    )(small, k, brow)


FOX_T = 512
FOX_UNROLL = 2


def _lockstep(gens):
    out, live = [None] * len(gens), list(range(len(gens)))
    while live:
        still = []
        for g in live:
            try:
                next(gens[g])
                still.append(g)
            except StopIteration as stop:
                out[g] = stop.value
        live = still
    return out


def _fox_body(q_ref, k_ref, vt_ref, o_ref):
    i = pl.program_id(1)
    t = q_ref.shape[2]
    place = (_iota((HEAD_DIM, LANES), 0) == _iota((HEAD_DIM, LANES), 1)).astype(BF16)
    lane = _iota((1, LANES), 1)
    ones = ((lane >= HEAD_DIM) & (lane < HEAD_DIM + 3)).astype(F32)
    causal = _iota((t, t), 0) <= _iota((t, t), 1)
    q_aug = [(_dot(q_ref[0, hd], place) + ones).astype(BF16) for hd in range(N_HEADS)]

    ones_rows = jnp.ones((SUM_ROWS, t), BF16)

    def head_steps(hd, chunks, carry, masked):
        m, acc = carry
        keys = [pl.ds(pl.multiple_of(j * t, t), t) for j in chunks]
        ahead = _dot_nt(k_ref[0, hd, keys[0], :], q_aug[hd])
        yield
        for u in range(len(chunks)):
            s = ahead
            if u + 1 < len(chunks):
                ahead = _dot_nt(k_ref[0, hd, keys[u + 1], :], q_aug[hd])
            if masked:
                s = jnp.where(causal, s, NEG_BIG)
            m_new = jnp.maximum(m, jnp.max(s, axis=0, keepdims=True))
            yield
            alpha = jnp.exp2(m - m_new)
            p = jnp.exp2(s - m_new)
            yield
            vt_aug = jnp.concatenate([vt_ref[0, hd * HEAD_DIM:(hd + 1) * HEAD_DIM, keys[u]], ones_rows], axis=0)
            m, acc = m_new, alpha * acc + _dot(vt_aug, p.astype(BF16))
            yield
        return m, acc

    def steps(chunks, carries, masked=False):
        return tuple(_lockstep([head_steps(hd, chunks, carries[hd], masked) for hd in range(N_HEADS)]))

    carries = tuple((jnp.full((1, t), NEG_BIG, F32), jnp.zeros((HEAD_DIM + SUM_ROWS, t), F32))
                    for _ in range(N_HEADS))
    n_groups = _div(i, FOX_UNROLL)
    carries = lax.fori_loop(0, n_groups, lambda g, c: steps([g * FOX_UNROLL + u for u in range(FOX_UNROLL)], c),
                            carries)
    carries = lax.fori_loop(n_groups * FOX_UNROLL, i, lambda j, c: steps([j], c), carries)
    carries = steps([i], carries, masked=True)
    o_ref[0] = jnp.concatenate([acc[:HEAD_DIM] / acc[HEAD_DIM:HEAD_DIM + 1] for (_, acc) in carries],
                               axis=0).T.astype(o_ref.dtype)


def _fox_attn(q, kaug, vt):
    bsz, nh, s, hd = q.shape
    t = min(FOX_T, s)
    return pl.pallas_call(
        _fox_body, grid=(bsz, s // t),
        in_specs=[pl.BlockSpec((1, nh, t, hd), lambda b, i: (b, 0, i, 0)),
                  pl.BlockSpec((1, nh, s, LANES), lambda b, i: (b, 0, 0, 0)),
                  pl.BlockSpec((1, MIX_W, s), lambda b, i: (b, 0, 0))],
        out_specs=pl.BlockSpec((1, t, MIX_W), lambda b, i: (b, i, 0)),
        out_shape=jax.ShapeDtypeStruct((bsz, s, MIX_W), BF16),
        compiler_params=_cp(("arbitrary", "arbitrary")), name="fox_attn",
    )(q, kaug, vt)


GDN_T = 256
GDN_NB = 4


def _gdn_body(qkv_ref, sm_ref, smt_ref, z_ref, cw_ref, alr_ref, dtr_ref, alc_ref, dtc_ref, nw_ref,
              o_ref, xbuf, state):
    t, c = GDN_T, GDN_CHUNK
    ri, ci = _iota((t, t), 0), _iota((t, t), 1)
    same = _div(ri, c) == _div(ci, c)
    er, ec = _iota((LANES, MIX_W), 0), _div(_iota((LANES, MIX_W), 1), HEAD_DIM)
    head_blk = _head_ones()
    masks = dict(same=same, incl=same & (ci <= ri), strict=same & (ci < ri),
                 same_b=same.astype(BF16), incl_b=(same & (ci <= ri)).astype(BF16),
                 incl_t_b=(same & (ri <= ci)).astype(BF16),
                 exp_a=(er == ec + SM_A).astype(BF16),
                 exp_b=(er == ec + SM_B).astype(BF16),
                 head_blk=head_blk, head_ones=head_blk.astype(BF16),
                 lane_head=_div(_iota((1, MIX_W), 1), HEAD_DIM))
    _lockstep([_gdn_tile(masks, qkv_ref.at[bb], sm_ref.at[bb], smt_ref.at[bb], z_ref.at[bb], cw_ref, alr_ref,
                         dtr_ref, alc_ref, dtc_ref, nw_ref, o_ref.at[bb], xbuf.at[bb], state.at[bb])
               for bb in range(qkv_ref.shape[0])])


def _gdn_tile(masks, qkv_ref, sm_ref, smt_ref, z_ref, cw_ref, alr_ref, dtr_ref, alc_ref, dtc_ref, nw_ref,
              o_ref, xbuf, state):
    t, c = GDN_T, GDN_CHUNK
    incl, strict, head_ones = masks["incl"], masks["strict"], masks["head_ones"]

    @pl.when(pl.program_id(1) == 0)
    def _():
        xbuf[0:8, :] = jnp.zeros((8, 3 * MIX_W), F32)
        state[...] = jnp.zeros_like(state)

    xbuf[8:8 + t, :] = qkv_ref[...]
    conv = cw_ref[0:1, :] * xbuf[pl.ds(8 - CONV_K + 1, t), :]
    for kk in range(1, CONV_K):
        conv = conv + cw_ref[kk:kk + 1, :] * xbuf[pl.ds(8 - CONV_K + 1 + kk, t), :]
    xbuf[0:8, :] = xbuf[t:t + 8, :]
    act = _silu(conv)
    q, k, v = act[:, :MIX_W], act[:, MIX_W:2 * MIX_W], act[:, 2 * MIX_W:]
    q = q * lax.rsqrt(_head_sum(q * q, head_ones) + 1e-6) * HEAD_DIM ** -0.5
    k = k * lax.rsqrt(_head_sum(k * k, head_ones) + 1e-6)

    sm, smt = sm_ref[...], smt_ref[...]
    la_col = -jnp.exp(alr_ref[...]) * _softplus(sm + dtr_ref[...])
    la_row = -jnp.exp(alc_ref[...]) * _softplus(smt + dtc_ref[...])
    beta_col = _sigmoid(sm)
    bc_col = _xdot(masks["incl_b"], la_col)
    last_col = _xdot(masks["same_b"], la_col)
    bc_row = _dot_x(la_row, masks["incl_t_b"])
    exp_a, exp_b = masks["exp_a"], masks["exp_b"]
    bc_x, last_x, beta_x = _dot_x(bc_col, exp_a), _dot_x(last_col, exp_a), _dot_x(beta_col, exp_b)
    eb = jnp.exp(bc_x)
    q_dec = q * eb
    k_dec = k * jnp.exp(last_x - bc_x)
    rhs_v = beta_x * v
    rhs_k = beta_x * eb * k
    rhs = jnp.concatenate([rhs_v, rhs_k], axis=1).astype(BF16)

    kb = k.astype(BF16)
    lane_head = masks["lane_head"]
    heads = range(N_HEADS)
    mh = [lane_head == hd for hd in heads]
    gram = [_dot_nt(jnp.where(mh[hd], k, 0.0).astype(BF16), kb) for hd in heads]
    qk = [_dot_nt(jnp.where(mh[hd], q, 0.0).astype(BF16), kb) for hd in heads]
    yield
    dec = [jnp.exp(jnp.minimum(bc_col[:, SM_A + hd:SM_A + hd + 1] - bc_row[SM_A + hd:SM_A + hd + 1, :], 0.0))
           for hd in heads]
    attn = [jnp.where(incl, qk[hd] * dec[hd], 0.0) for hd in heads]
    p = [jnp.where(strict, -(beta_col[:, SM_B + hd:SM_B + hd + 1] * gram[hd] * dec[hd]), 0.0) for hd in heads]
    tr = list(p)
    for _ in range(5):
        pb = [p[hd].astype(BF16) for hd in heads]
        p = [_dot(pb[hd], pb[hd]) for hd in heads]
        yield
        tr = [tr[hd] + p[hd] + _dot(tr[hd].astype(BF16), p[hd].astype(BF16)) for hd in heads]
    sol = [_dot(tr[hd].astype(BF16), rhs) for hd in heads]
    yield
    u0, w = rhs_v, rhs_k
    for hd in heads:
        u0 = u0 + jnp.where(mh[hd], sol[hd][:, :MIX_W], 0.0)
        w = w + jnp.where(mh[hd], sol[hd][:, MIX_W:], 0.0)

    head_blk = masks["head_blk"]
    k_dec_t = k_dec.T.astype(BF16)
    outs = []
    for n in range(t // c):
        rows = slice(n * c, (n + 1) * c)
        s_old = state[...]
        sb = s_old.astype(BF16)
        u_n = u0[rows] - _dot(w[rows].astype(BF16), sb)
        yield
        parts = ([jnp.zeros((n * c, MIX_W), F32)] if n else []) + [u_n]
        if t - (n + 1) * c:
            parts.append(jnp.zeros((t - (n + 1) * c, MIX_W), F32))
        u_full = jnp.concatenate(parts, axis=0).astype(BF16)
        oa = _dot(jnp.concatenate([a[rows] for a in attn], axis=0).astype(BF16), u_full)
        o_n = _dot(q_dec[rows].astype(BF16), sb)
        for hd in range(N_HEADS):
            o_n = o_n + jnp.where(lane_head == hd, oa[hd * c:(hd + 1) * c], 0.0)
        outs.append(o_n)
        g_row = jnp.exp(last_x[n * c:n * c + 1, :])
        state[...] = g_row * s_old + jnp.where(head_blk, _dot(k_dec_t, u_full), 0.0)
        yield
    o = jnp.concatenate(outs, axis=0)
    o = o * lax.rsqrt(_head_sum(o * o, head_ones) * (1.0 / HEAD_DIM) + 1e-6) * nw_ref[...]
    o_ref[...] = (o * _silu(z_ref[...])).astype(o_ref.dtype)


def _lane_vec(vals, off):
    return jnp.zeros((1, LANES), F32).at[0, off:off + vals.shape[0]].set(vals)


def _gdn(qkv, small, small_t, z, conv_w, a_log, dt_bias, norm_w):
    bsz, s, _ = qkv.shape
    t = GDN_T
    alr, dtr = _lane_vec(a_log, SM_A), _lane_vec(dt_bias, SM_A)
    nw = jnp.tile(norm_w, N_HEADS).reshape(1, MIX_W)
    const = lambda shape: pl.BlockSpec(shape, lambda b, i: (0,) * len(shape))
    nb = GDN_NB if bsz % GDN_NB == 0 else 1
    return pl.pallas_call(
        _gdn_body, grid=(bsz // nb, s // t),
        in_specs=[pl.BlockSpec((nb, t, 3 * MIX_W), lambda b, i: (b, i, 0)),
                  pl.BlockSpec((nb, t, LANES), lambda b, i: (b, i, 0)),
                  pl.BlockSpec((nb, LANES, t), lambda b, i: (b, 0, i)),
                  pl.BlockSpec((nb, t, MIX_W), lambda b, i: (b, i, 0)),
                  const((CONV_K, 3 * MIX_W)), const((1, LANES)), const((1, LANES)),
                  const((LANES, 1)), const((LANES, 1)), const((1, MIX_W))],
        out_specs=pl.BlockSpec((nb, t, MIX_W), lambda b, i: (b, i, 0)),
        out_shape=jax.ShapeDtypeStruct((bsz, s, MIX_W), BF16),
        scratch_shapes=[pltpu.VMEM((nb, t + 8, 3 * MIX_W), F32), pltpu.VMEM((nb, MIX_W, MIX_W), F32)],
        compiler_params=_cp(("arbitrary", "arbitrary")), name="gdn",
    )(qkv, small, small_t, z, conv_w, alr, dtr, alr.reshape(LANES, 1), dtr.reshape(LANES, 1), nw)


RET_T = 1024
_RET_LOG_GAMMA = tuple(math.log1p(-(2.0 ** (-5.0 - h))) for h in range(N_HEADS))


def _per_head(lane_head, vals):
    out = jnp.full(lane_head.shape, vals[0], F32)
    for hd in range(1, N_HEADS):
        out = jnp.where(lane_head == hd, vals[hd], out)
    return out


def _ret_body(r_ref, pos_ref, gnw_ref, o_ref, state):
    t, c = r_ref.shape[1], RET_CHUNK

    @pl.when(pl.program_id(1) == 0)
    def _():
        state[...] = jnp.zeros_like(state)

    x = r_ref[0]
    lane = _iota((1, MIX_W), 1)
    lane_head = _div(lane, HEAD_DIM)
    half = HEAD_DIM // 2
    inv_freq = jnp.exp(_mod(_iota((1, LANES), 1), half).astype(F32) * (-math.log(ROPE_BASE) / half))
    ang = pos_ref[0].astype(F32) * inv_freq
    cos, sin = jnp.cos(ang), jnp.sin(ang)
    cos, sin = jnp.concatenate([cos, cos], axis=1), jnp.concatenate([sin, sin], axis=1)
    first = _mod(lane, HEAD_DIM) < half

    def rope(a):
        rot = jnp.where(first, -pltpu.roll(a, MIX_W - half, 1), pltpu.roll(a, half, 1))
        return a * cos + rot * sin

    q, k = rope(x[:, :MIX_W]), rope(x[:, MIX_W:2 * MIX_W])
    v, g = x[:, 2 * MIX_W:3 * MIX_W], x[:, 3 * MIX_W:]

    lg = _per_head(lane_head, _RET_LOG_GAMMA)
    cidx = _iota((c, 1), 0).astype(F32)
    xi = jnp.exp((cidx + 1.0) * lg)
    zeta = jnp.exp((c - 1.0 - cidx) * lg)
    chunk_decay = jnp.exp(float(c) * lg)
    rel = (_iota((c, c), 0) - _iota((c, c), 1)).astype(F32)
    dec = jnp.concatenate([jnp.where(rel >= 0, jnp.exp(jnp.maximum(rel, 0.0) * _RET_LOG_GAMMA[hd]), 0.0)
                           for hd in range(N_HEADS)], axis=0)
    head_blk = _head_ones()
    outs = []
    for n in range(t // c):
        rows = slice(n * c, (n + 1) * c)
        qn, kn, vn = q[rows], k[rows], v[rows].astype(BF16)
        qs = jnp.concatenate([jnp.where(lane_head == hd, qn, 0.0) for hd in range(N_HEADS)], axis=0).astype(BF16)
        sc = _dot_nt(qs, kn.astype(BF16)) * dec
        res = _dot(sc.astype(BF16), vn)
        s_old = state[...]
        o_n = _dot(qn.astype(BF16), s_old.astype(BF16)) * xi
        for hd in range(N_HEADS):
            o_n = o_n + jnp.where(lane_head == hd, res[hd * c:(hd + 1) * c], 0.0)
        outs.append(o_n)
        kv = _dot((kn * zeta).T.astype(BF16), vn)
        state[...] = chunk_decay * s_old + jnp.where(head_blk, kv, 0.0)
    o = jnp.concatenate(outs, axis=0)
    head_ones = head_blk.astype(BF16)
    mu = _head_sum(o, head_ones) * (1.0 / HEAD_DIM)
    xc = o - mu
    var = _head_sum(xc * xc, head_ones) * (1.0 / HEAD_DIM)
    y = xc * lax.rsqrt(var + LN_EPS) * gnw_ref[...]
    o_ref[0] = (_silu(g) * y).astype(o_ref.dtype)


def _retention(r, positions, gn_w):
    bsz, s, _ = r.shape
    t = min(RET_T, s)
    return pl.pallas_call(
        _ret_body, grid=(bsz, s // t),
        in_specs=[pl.BlockSpec((1, t, 4 * MIX_W), lambda b, i: (b, i, 0)),
                  pl.BlockSpec((1, t, 1), lambda b, i: (b, i, 0)),
                  pl.BlockSpec((1, MIX_W), lambda b, i: (0, 0))],
        out_specs=pl.BlockSpec((1, t, MIX_W), lambda b, i: (b, i, 0)),
        out_shape=jax.ShapeDtypeStruct((bsz, s, MIX_W), BF16),
        scratch_shapes=[pltpu.VMEM((MIX_W, MIX_W), F32)],
        compiler_params=_cp(("arbitrary", "arbitrary")), name="retention",
    )(r, positions.reshape(bsz, s, 1), gn_w.reshape(1, MIX_W))


def _nsacmp_body(kc_ref, vc_ref, pe_ref, kw1_ref, kw2_ref, vw1_ref, vw2_ref, ko_ref, vo_ref):
    half = NSA_CMP_STRIDE * HEAD_DIM
    nc = kc_ref.shape[1]
    def hidden(x_ref, w1_ref):
        x = x_ref[0]
        top = _dot_hp(x, w1_ref[0:half, :])
        bot = _dot_hp(x, w1_ref[half:2 * half, :])
        bias = _dot_hp(pe_ref[...], w1_ref[...])[0:1, :]
        return _silu(top + pltpu.roll(bot, nc - 1, 0) + bias)

    ko_ref[0] = _dot_hp(hidden(kc_ref, kw1_ref), kw2_ref[...])
    a1, a2, _ = _split3(vw2_ref[...])
    b1, b2, _ = _split3(hidden(vc_ref, vw1_ref))
    vo_ref[0] = _dot_nt(a1, b1) + (_dot_nt(a1, b2) + _dot_nt(a2, b1))


def _nsa_compress(kc, vc, pe, ck_w1, ck_w2, cv_w1, cv_w2):
    bsz, s, hd = kc.shape
    nc = s // NSA_CMP_STRIDE
    wide = NSA_CMP_STRIDE * hd
    pe8 = jnp.broadcast_to(pe.reshape(1, NSA_CMP_LEN * hd), (8, NSA_CMP_LEN * hd))
    cv_w2t = cv_w2.T
    const = lambda a: pl.BlockSpec(a.shape, lambda b: (0,) * a.ndim)
    xspec = pl.BlockSpec((1, nc, wide), lambda b: (b, 0, 0))
    return pl.pallas_call(
        _nsacmp_body, grid=(bsz,),
        in_specs=[xspec, xspec, const(pe8), const(ck_w1), const(ck_w2), const(cv_w1), const(cv_w2t)],
        out_specs=[pl.BlockSpec((1, nc, hd), lambda b: (b, 0, 0)), pl.BlockSpec((1, hd, nc), lambda b: (b, 0, 0))],
        out_shape=[jax.ShapeDtypeStruct((bsz, nc, hd), F32), jax.ShapeDtypeStruct((bsz, hd, nc), F32)],
        compiler_params=_cp(("arbitrary",)), name="nsa_compress",
    )(kc.reshape(bsz, nc, wide), vc.reshape(bsz, nc, wide), pe8, ck_w1, ck_w2, cv_w1, cv_w2t)


NSA_TQ = 256
NSA_TK = 512
NSA_UNROLL = 2


def _softmax_weights(s, valid):
    s = jnp.where(valid, s, NEG_BIG)
    m = jnp.max(s, axis=0, keepdims=True)
    m = jnp.where(m > 0.5 * NEG_BIG, m, 0.0)
    return jnp.exp2(s - m)


def _with_ones(vt):
    return jnp.concatenate([vt, jnp.ones((SUM_ROWS, vt.shape[1]), BF16)], axis=0)


def _nsa_body(qt_ref, kc_ref, vct_ref, ks_ref, vst_ref, kw_ref, vwt_ref, smt_ref, o_ref, kaug):
    i = pl.program_id(1)
    tq, hd = qt_ref.shape[2], HEAD_DIM
    s, nc = ks_ref.shape[1], kc_ref.shape[1]
    nsel = s // NSA_SEL_LEN
    n_top = min(NSA_TOP_N, nsel)
    cols = N_HEADS * tq

    @pl.when(i == 0)
    def _():
        place = (_iota((hd, LANES), 0) == _iota((hd, LANES), 1)).astype(BF16)
        onehot = (_iota((s, LANES), 1) - hd) == _div(_iota((s, LANES), 0), NSA_SEL_LEN)
        kaug[...] = (_dot(ks_ref[0], place) + onehot.astype(F32)).astype(BF16)

    t0 = i * tq
    qt = qt_ref[0]
    q4 = jnp.concatenate([qt[h * hd:(h + 1) * hd, :] for h in range(N_HEADS)], axis=1)
    tpos = t0 + _iota((1, tq), 1)
    tpos4 = t0 + _mod(_iota((1, cols), 1), tq)

    def cmp_branch():
        sc = _dot(kc_ref[0].astype(BF16), q4)
        yield
        cmp_end = _iota((nc, 1), 0) * NSA_CMP_STRIDE + (NSA_CMP_LEN - 1)
        e = _softmax_weights(sc, cmp_end <= tpos4)
        den = jnp.sum(e, axis=0, keepdims=True)
        p = e * (1.0 / jnp.where(den > 0.0, den, 1.0))
        yield
        return p, _dot(vct_ref[0].astype(BF16), p.astype(BF16))

    def win_branch():
        wlen = NSA_WINDOW + tq
        start = pl.multiple_of(jnp.maximum(t0 - NSA_WINDOW, 0), tq)
        sc = _dot(kw_ref[0, pl.ds(start, wlen), :], q4)
        yield
        kpos = start + _iota((wlen, 1), 0)
        e = _softmax_weights(sc, (kpos <= tpos4) & (kpos > tpos4 - NSA_WINDOW))
        yield
        r = _dot(_with_ones(vwt_ref[0, :, pl.ds(start, wlen)]), e.astype(BF16))
        return r[:hd] / r[hd:hd + 1]

    (p_cmp, o_cmp), o_win = _lockstep([cmp_branch(), win_branch()])

    psum = p_cmp[:, 0:tq] + p_cmp[:, tq:2 * tq] + p_cmp[:, 2 * tq:3 * tq] + p_cmp[:, 3 * tq:4 * tq]
    ss, cs = _iota((hd, nc), 0) * NSA_SEL_LEN, _iota((hd, nc), 1) * NSA_CMP_STRIDE
    overlap = jnp.clip(jnp.minimum(cs + NSA_CMP_LEN, ss + NSA_SEL_LEN) - jnp.maximum(cs, ss), 0, NSA_CMP_LEN)
    importance = _xdot((overlap.astype(F32) * (1.0 / NSA_CMP_LEN)).astype(BF16), psum)
    blk = _iota((hd, 1), 0)
    cur = _div(tpos, NSA_SEL_LEN)
    forced = (blk == 0) | (blk == cur) | (blk == cur - 1)
    score = jnp.where(forced, NSA_FORCE_SCORE, jnp.where(blk * NSA_SEL_LEN <= tpos, importance, -1.0))
    score = jnp.where(blk < nsel, score, -2.0)
    groups = [score[8 * g:8 * g + 8, :] for g in range(hd // 8)]
    gblk = _iota((8, 1), 0)
    cnts = [jnp.zeros((8, tq), jnp.int32) for _ in groups]
    for j in range(nsel):
        row = score[j:j + 1, :]
        for g in range(hd // 8):
            if 8 * g > j:
                ahead = row >= groups[g]
            elif 8 * g + 7 <= j:
                ahead = row > groups[g]
            else:
                ahead = (row > groups[g]) | ((row == groups[g]) & (gblk + 8 * g > j))
            cnts[g] = cnts[g] + ahead.astype(jnp.int32)
    bias = jnp.where(jnp.concatenate(cnts, axis=0) < n_top, 0.0, SEL_MASK_BIAS).astype(BF16)
    q_aug = jnp.concatenate([q4, jnp.concatenate([bias] * N_HEADS, axis=1)], axis=0)

    def sel_keys(c):
        return pl.ds(pl.multiple_of(c * NSA_TK, NSA_TK), NSA_TK)

    def sel_scores(c):
        return _dot(kaug[sel_keys(c), :], q_aug)

    def sel_absorb(c, sc, carry):
        m, acc = carry
        m_new = jnp.maximum(m, jnp.max(sc, axis=0, keepdims=True))
        alpha = jnp.exp2(m - m_new)
        p = jnp.exp2(sc - m_new)
        return m_new, alpha * acc + _dot(_with_ones(vst_ref[0, :, sel_keys(c)]), p.astype(BF16))

    def sel_step(c, carry, masked=False):
        sc = sel_scores(c)
        if masked:
            sc = jnp.where(c * NSA_TK + _iota((NSA_TK, 1), 0) <= tpos4, sc, NEG_BIG)
        return sel_absorb(c, sc, carry)

    def sel_group(g, carry):
        chunks = [g * NSA_UNROLL + u for u in range(NSA_UNROLL)]
        ahead = sel_scores(chunks[0])
        for u, c in enumerate(chunks):
            sc, ahead = ahead, (sel_scores(chunks[u + 1]) if u + 1 < NSA_UNROLL else None)
            carry = sel_absorb(c, sc, carry)
        return carry

    n_full = _div(t0, NSA_TK)
    n_groups = _div(n_full, NSA_UNROLL)
    carry = (jnp.full((1, cols), NEG_BIG, F32), jnp.zeros((hd + SUM_ROWS, cols), F32))
    carry = lax.fori_loop(0, n_groups, sel_group, carry)
    carry = lax.fori_loop(n_groups * NSA_UNROLL, n_full, sel_step, carry)
    _, acc_sel = sel_step(n_full, carry, masked=True)
    o_sel = acc_sel[:hd] / acc_sel[hd:hd + 1]

    gates =_sigmoid(smt_ref[0][SM_GATE:SM_GATE + 16, :])
    outs = []
    for h in range(N_HEADS):
        c = slice(h * tq, (h + 1) * tq)
        outs.append(gates[3 * h:3 * h + 1, :] * o_cmp[:, c] + gates[3 * h + 1:3 * h + 2, :] * o_sel[:, c]
                    + gates[3 * h + 2:3 * h + 3, :] * o_win[:, c])
    o_ref[0] = jnp.concatenate(outs, axis=0).T.astype(o_ref.dtype)


def _nsa_attn(qt, k_cmp, v_cmp_t, ks, vst, kw, vwt, small_t):
    bsz, _, s = qt.shape
    tq, hd = NSA_TQ, HEAD_DIM
    assert s % NSA_TK == 0 and NSA_TK % tq == 0 and s >= NSA_WINDOW + tq and s // NSA_SEL_LEN <= hd
    nc = k_cmp.shape[1]
    keys = pl.BlockSpec((1, s, hd), lambda b, i: (b, 0, 0))
    vals = pl.BlockSpec((1, hd, s), lambda b, i: (b, 0, 0))
    return pl.pallas_call(
        _nsa_body, grid=(bsz, s // tq),
        in_specs=[pl.BlockSpec((1, MIX_W, tq), lambda b, i: (b, 0, i)),
                  pl.BlockSpec((1, nc, hd), lambda b, i: (b, 0, 0)),
                  pl.BlockSpec((1, hd, nc), lambda b, i: (b, 0, 0)),
                  keys, vals, keys, vals,
                  pl.BlockSpec((1, LANES, tq), lambda b, i: (b, 0, i))],
        out_specs=pl.BlockSpec((1, tq, MIX_W), lambda b, i: (b, i, 0)),
        out_shape=jax.ShapeDtypeStruct((bsz, s, MIX_W), BF16),
        scratch_shapes=[pltpu.VMEM((s, LANES), BF16)],
        compiler_params=_cp(("arbitrary", "arbitrary")), name="nsa_attn",
    )(qt, k_cmp, v_cmp_t, ks, vst, kw, vwt, small_t)


MERGE_T = 1024


def _layer_norm(r, g, b):
    mu = jnp.mean(r, axis=-1, keepdims=True)
    xc = r - mu
    var = jnp.mean(xc * xc, axis=-1, keepdims=True)
    return xc * lax.rsqrt(var + LN_EPS) * g + b


def _merge_body(alpha, x_ref, sc_ref, sh_ref, gt_ref, og_ref, or_ref, on_ref, of_ref, wg_ref, bp_ref, wo_ref,
                lng_ref, lnb_ref, sc2_ref, sh2_ref, rwt_ref, o_ref, hp_ref, lg_ref):
    x = x_ref[0]
    h = (x * sc_ref[0] + sh_ref[0]).astype(BF16)

    merged = None
    for br, o_br in enumerate((og_ref, or_ref, on_ref, of_ref)):
        term = _sigmoid(_dot(h, wg_ref[br])) * _dot(o_br[0], bp_ref[br])
        merged = term if merged is None else merged + term
    y = _dot(merged.astype(BF16), wo_ref[...])
    x_new = _layer_norm(alpha * x + gt_ref[0] * y, lng_ref[...], lnb_ref[...])
    o_ref[0] = x_new
    h2 = x_new * sc2_ref[0] + sh2_ref[0]
    half = h2.shape[1] // 2
    hp_ref[0] = _pack_bf16_pair(h2[:, :half], h2[:, half:])
    a1, a2, _ = _split3(rwt_ref[...])
    b1, b2, _ = _split3(h2)
    lg_ref[0] = _dot_nt(a1, b1) + (_dot_nt(a1, b2) + _dot_nt(a2, b1))


def _merge(x, sc1p, shift, gate, o_gdn, o_ret, o_nsa, o_fox, w_gate, branch_proj, w_out, ln_g, ln_b,
           sc1p_moe, shift_moe, router_w, alpha):
    bsz, s, d = x.shape
    t = min(MERGE_T, s)
    tok = lambda w: pl.BlockSpec((1, t, w), lambda b, i: (b, i, 0))
    vec = pl.BlockSpec((1, 1, d), lambda b, i: (b, 0, 0))
    const = lambda a: pl.BlockSpec(a.shape, lambda b, i: (0,) * a.ndim, pipeline_mode=pl.Buffered(1))
    lng, lnb, rwt = ln_g.reshape(1, d), ln_b.reshape(1, d), router_w.T
    return pl.pallas_call(
        functools.partial(_merge_body, alpha), grid=(bsz, s // t),
        in_specs=[tok(d), vec, vec, vec, tok(MIX_W), tok(MIX_W), tok(MIX_W), tok(MIX_W),
                  const(w_gate), const(branch_proj), const(w_out), const(lng), const(lnb), vec, vec, const(rwt)],
        out_specs=[tok(d), tok(d // 2), pl.BlockSpec((1, N_EXPERTS, t), lambda b, i: (b, 0, i))],
        out_shape=[jax.ShapeDtypeStruct((bsz, s, d), F32), jax.ShapeDtypeStruct((bsz, s, d // 2), jnp.uint32),
                   jax.ShapeDtypeStruct((bsz, N_EXPERTS, s), F32)],
        compiler_params=_cp(("arbitrary", "arbitrary"), VMEM_LIMIT), name="merge",
    )(x, sc1p, shift, gate, o_gdn, o_ret, o_nsa, o_fox, w_gate, branch_proj, w_out, lng, lnb,
      sc1p_moe, shift_moe, rwt)


MOE_TM = 256


def _pack_bf16_pair(a, b):
    hi = lax.bitcast_convert_type(a.astype(BF16).astype(F32), jnp.uint32)
    lo = lax.bitcast_convert_type(b.astype(BF16).astype(F32), jnp.uint32)
    return hi | lax.shift_right_logical(lo, jnp.uint32(16))


def _unpack_bf16_pair(w):
    a = lax.bitcast_convert_type(w & jnp.uint32(0xFFFF0000), F32)
    b = lax.bitcast_convert_type(lax.shift_left(w, jnp.uint32(16)), F32)
    return a, b


def _router_body(tm, lg_ref, rb_ref, pos_ref, wcol_ref, tile_ref):
    s = lg_ref.shape[2]
    scores = _sigmoid(lg_ref[0])
    biased = scores + rb_ref[...]
    b = [biased[e:e + 1, :] for e in range(N_EXPERTS)]
    sc = [scores[e:e + 1, :] for e in range(N_EXPERTS)]
    gs = []
    for g in range(N_GROUPS):
        m = [b[EXPERTS_PER_GROUP * g + j] for j in range(EXPERTS_PER_GROUP)]
        best = m[0] + m[1]
        for u in range(EXPERTS_PER_GROUP):
            for v in range(u + 1, EXPERTS_PER_GROUP):
                if (u, v) != (0, 1):
                    best = jnp.maximum(best, m[u] + m[v])
        gs.append(best)
    gsel, best = jnp.zeros((1, s), jnp.int32), gs[0]
    for g in range(1, N_GROUPS):
        take = gs[g] > best
        gsel = jnp.where(take, g, gsel)
        best = jnp.where(take, gs[g], best)
    first, second = [], []
    for e in range(N_EXPERTS):
        g = e // EXPERTS_PER_GROUP
        cnt = jnp.zeros((1, s), jnp.int32)
        for j in range(EXPERTS_PER_GROUP * g, EXPERTS_PER_GROUP * (g + 1)):
            if j != e:
                ahead = (b[j] >= b[e]) if j < e else (b[j] > b[e])
                cnt = cnt + ahead.astype(jnp.int32)
        first.append((gsel == g) & (cnt == 0))
        second.append((gsel == g) & (cnt == 1))
    s0 = sum(jnp.where(first[e], sc[e], 0.0) for e in range(N_EXPERTS))
    s1 = sum(jnp.where(second[e], sc[e], 0.0) for e in range(N_EXPERTS))
    den = s0 + s1
    w0, w1 = s0 / den, s1 / den
    blk = min(512, s)
    eye = (_iota((blk, blk), 0) == _iota((blk, blk), 1)).astype(BF16)
    rowid = _iota((LANES, blk), 0)
    for j in range(s // blk):
        cols = slice(j * blk, (j + 1) * blk)
        wb = jnp.where(rowid == 0, w0[:, cols], jnp.where(rowid == 1, w1[:, cols], 0.0))
        wcol_ref[0, cols, :] = sum(_dot_nt(eye, part) for part in _split3(wb))
    onehot = jnp.concatenate([(first[e] | second[e]).astype(F32) for e in range(N_EXPERTS)], axis=0)
    before = (_iota((blk, blk), 0) < _iota((blk, blk), 1)).astype(BF16)
    carry, ranks = jnp.zeros((N_EXPERTS, 1), F32), []
    for j in range(s // blk):
        ob = onehot[:, j * blk:(j + 1) * blk]
        ranks.append(_dot(ob.astype(BF16), before) + carry)
        carry = carry + jnp.sum(ob, axis=1, keepdims=True)
    rank = jnp.concatenate(ranks, axis=1)
    ntile = jnp.floor((carry + (tm - 1.0)) * (1.0 / tm))
    lower = (_iota((N_EXPERTS, N_EXPERTS), 1) < _iota((N_EXPERTS, N_EXPERTS), 0)).astype(BF16)
    toff = _dot(lower, jnp.broadcast_to(ntile, (N_EXPERTS, LANES)).astype(BF16))[:, 0:1]
    slot = toff * float(tm) + rank
    pos0 = sum(jnp.where(first[e], slot[e:e + 1, :], 0.0) for e in range(N_EXPERTS))
    pos1 = sum(jnp.where(second[e], slot[e:e + 1, :], 0.0) for e in range(N_EXPERTS))
    pos_ref[0] = jnp.concatenate([pos0, pos1], axis=1).astype(jnp.int32)
    tend = toff + ntile
    tid = _iota((1, LANES), 1).astype(F32)
    texp = jnp.sum((tend <= tid).astype(F32), axis=0, keepdims=True)
    ntot = jnp.broadcast_to(jnp.sum(ntile, axis=0, keepdims=True), (1, LANES))
    diag = _iota((N_EXPERTS, LANES), 0) == _iota((N_EXPERTS, LANES), 1)
    to_lanes = lambda col: jnp.sum(jnp.where(diag, col, 0.0), axis=0, keepdims=True)
    tile_ref[0] = jnp.concatenate([jnp.minimum(texp, N_EXPERTS - 1.0), ntot, to_lanes(carry), to_lanes(toff)],
                                  axis=0).astype(jnp.int32)


def _router(logits, router_b, tm):
    bsz, _, s = logits.shape
    return pl.pallas_call(
        functools.partial(_router_body, tm), grid=(bsz,),
        in_specs=[pl.BlockSpec((1, N_EXPERTS, s), lambda b: (b, 0, 0)),
                  pl.BlockSpec((N_EXPERTS, 1), lambda b: (0, 0))],
        out_specs=[pl.BlockSpec((1, 1, 2 * s), lambda b: (b, 0, 0)),
                   pl.BlockSpec((1, s, LANES), lambda b: (b, 0, 0)),
                   pl.BlockSpec((1, 4, LANES), lambda b: (b, 0, 0))],
        out_shape=[jax.ShapeDtypeStruct((bsz, 1, 2 * s), jnp.int32),
                   jax.ShapeDtypeStruct((bsz, s, LANES), F32),
                   jax.ShapeDtypeStruct((bsz, 4, LANES), jnp.int32)],
        compiler_params=_cp(("arbitrary",)), name="router",
    )(logits, router_b.reshape(N_EXPERTS, 1))


LN_ROWS = 512


def _moe_body(alpha, tm, nt, tiles_ref, hp_ref, x_ref, wcol_ref, gt_ref, pos_ref, posn_ref,
              w1_ref, w3_ref, w2_ref, lng_ref, lnb_ref, o_ref, src, xs, ysall, st0, st1):
    b, i = pl.program_id(0), pl.program_id(1)
    s = hp_ref.shape[1]
    tb = b * (4 * LANES)
    span = src.shape[0] // 2
    mine, nxt = lax.rem(b, 2) * span, lax.rem(b + 1, 2) * span

    def pad(p, c):
        src[mine + p] = 0
        return c

    def gather(buf, base, lo, hi):
        for r in range(lo, hi):
            xs[buf, pl.ds(r, 1), :] = hp_ref[0, pl.ds(src[mine + base + r], 1), :]

    @pl.when(i == 0)
    def _():
        for e in range(N_EXPERTS):
            cnt, first = tiles_ref[tb + 2 * LANES + e], tiles_ref[tb + 3 * LANES + e]
            lo = first * tm + cnt
            hi = (first + lax.shift_right_logical(cnt + (tm - 1), int(math.log2(tm)))) * tm
            lax.fori_loop(lo, hi, pad, 0)

        @pl.when(b == 0)
        def _():
            def fill(t, c):
                for k in range(2):
                    src[mine + pos_ref[0, 0, k * s + t]] = t
                return c
            lax.fori_loop(0, s, fill, 0, unroll=8)
        last = tiles_ref[tb + LANES] * tm
        lax.fori_loop(last, last + tm, pad, 0)
        gather(0, 0, 0, tm)

    @pl.when(i < tiles_ref[tb + LANES])
    def _():
        base = pl.multiple_of(i * tm, tm)
        cur = lax.rem(i, 2)
        f, d = w1_ref.shape[2], w2_ref.shape[2]
        n_piece = 2 * (f // MXU_N) + d // MXU_N
        rows = tm // n_piece
        piece = iter(range(n_piece))

        fill_rows = tm // (2 * n_piece)

        def gather_ahead():
            p = next(piece)
            gather(1 - cur, base + tm, p * rows, (p + 1) * rows)
            for r in range(fill_rows):
                t = jnp.minimum(i * (tm // 2) + p * fill_rows + r, s - 1)
                for k in range(2):
                    src[nxt + posn_ref[0, 0, k * s + t]] = t

        xa, xb = _unpack_bf16_pair(xs[cur])
        x_t = jnp.concatenate([xa, xb], axis=1).astype(BF16)
        up = []
        for w_ref in (w1_ref, w3_ref):
            cols = []
            for j in range(f // MXU_N):
                cols.append(_dot(x_t, w_ref[0, :, j * MXU_N:(j + 1) * MXU_N]))
                gather_ahead()
            up.append(jnp.concatenate(cols, axis=1))
        hid = (_silu(up[0]) * up[1]).astype(BF16)
        ys = []
        for j in range(d // MXU_N):
            ys.append(_dot(hid, w2_ref[0, :, j * MXU_N:(j + 1) * MXU_N]))
            gather_ahead()
        half = d // (2 * MXU_N)
        ysall[pl.ds(base, tm), :] = _pack_bf16_pair(jnp.concatenate(ys[:half], axis=1),
                                                    jnp.concatenate(ys[half:], axis=1))

    @pl.when(i >= nt)
    def _():
        t0 = (i - nt) * LN_ROWS

        def fetch(r, c):
            st0[pl.ds(r, 1), :] = ysall[pl.ds(pos_ref[0, 0, t0 + r], 1), :]
            st1[pl.ds(r, 1), :] = ysall[pl.ds(pos_ref[0, 0, s + t0 + r], 1), :]
            return c
        lax.fori_loop(0, LN_ROWS, fetch, 0, unroll=True)
        a0, b0 = _unpack_bf16_pair(st0[...])
        a1, b1 = _unpack_bf16_pair(st1[...])
        w0, w1 = wcol_ref[0][:, 0:1], wcol_ref[0][:, 1:2]
        y = jnp.concatenate([w0 * a0 + w1 * a1, w0 * b0 + w1 * b1], axis=1)
        o_ref[0] = _layer_norm(alpha * x_ref[0] + gt_ref[0] * y, lng_ref[...], lnb_ref[...])


def _moe(x, gate, hp, pos, wcol, tiles, w1, w3, w2, ln_g, ln_b, alpha, tm):
    bsz, s, d = x.shape
    f = w1.shape[2]
    nt = 2 * s // tm + N_EXPERTS
    assert nt <= LANES and tm & (tm - 1) == 0 and s % LN_ROWS == 0
    blk = lambda i: jnp.maximum(i - nt, 0)
    vec = pl.BlockSpec((1, 1, d), lambda b, i, tl: (b, 0, 0))
    tok = lambda w: pl.BlockSpec((1, LN_ROWS, w), lambda b, i, tl: (b, blk(i), 0))
    row = pl.BlockSpec((1, d), lambda b, i, tl: (0, 0))
    expert = lambda shape: pl.BlockSpec(shape, lambda b, i, tl: (tl[b * (4 * LANES) + jnp.minimum(i, nt - 1)], 0, 0))
    grid_spec = pltpu.PrefetchScalarGridSpec(
        num_scalar_prefetch=1, grid=(bsz, nt + s // LN_ROWS),
        in_specs=[pl.BlockSpec((1, s, d // 2), lambda b, i, tl: (b, 0, 0), pipeline_mode=pl.Buffered(1)),
                  tok(d), tok(LANES), vec,
                  pl.BlockSpec((1, 1, 2 * s), lambda b, i, tl: (b, 0, 0), memory_space=pltpu.SMEM),
                  pl.BlockSpec((1, 1, 2 * s), lambda b, i, tl: (jnp.minimum(b + 1, bsz - 1), 0, 0),
                               memory_space=pltpu.SMEM),
                  expert((1, d, f)), expert((1, d, f)), expert((1, f, d)), row, row],
        out_specs=tok(d),
        scratch_shapes=[pltpu.SMEM((2 * (nt + 1) * tm,), jnp.int32), pltpu.VMEM((2, tm, d // 2), jnp.uint32),
                        pltpu.VMEM((nt * tm, d // 2), jnp.uint32),
                        pltpu.VMEM((LN_ROWS, d // 2), jnp.uint32), pltpu.VMEM((LN_ROWS, d // 2), jnp.uint32)])
    return pl.pallas_call(
        functools.partial(_moe_body, alpha, tm, nt), grid_spec=grid_spec,
        out_shape=jax.ShapeDtypeStruct((bsz, s, d), F32),
        compiler_params=_cp(("arbitrary", "arbitrary"), VMEM_LIMIT), name="moe_experts",
    )(tiles.reshape(-1), hp, x, wcol, gate, pos, pos, w1, w3, w2, ln_g.reshape(1, d), ln_b.reshape(1, d))


def kernel(x, c, positions, ada_w, ada_b, w_in, gdn_conv_w, gdn_a_log, gdn_dt_bias, gdn_norm_w, ret_gn_w,
           nsa_cmp_pe, nsa_ck_w1, nsa_ck_w2, nsa_cv_w1, nsa_cv_w2, fox_f_bias, branch_proj, w_gate, w_out,
           ln_g, ln_b, router_w, router_b, exp_w1, exp_w3, exp_w2):
    depth, d = w_in.shape[0], x.shape[-1]
    alpha = (2.0 * depth) ** 0.25
    mod = _ada_mod(c, ada_w, ada_b)
    for l in range(depth):
        shift, sc1p, gate = mod[l, 0, :, :, :d], mod[l, 0, :, :, d:2 * d], mod[l, 0, :, :, 2 * d:]
        wcat, wcat_t = _cat_in_weights(w_in[l])
        (gqkv, gz, r, kc, vc, ks, kw, fq, fk, sm,
         smt, nqt, vst, vwt, fvt) = _in_proj(x, sc1p, shift, wcat, wcat_t)
        o_gdn = _gdn(gqkv, sm, smt, gz, gdn_conv_w[l], gdn_a_log[l], gdn_dt_bias[l], gdn_norm_w[l])
        o_ret = _retention(r, positions, ret_gn_w[l])
        k_cmp, v_cmp_t = _nsa_compress(kc, vc, nsa_cmp_pe[l], nsa_ck_w1[l], nsa_ck_w2[l], nsa_cv_w1[l], nsa_cv_w2[l])
        o_nsa = _nsa_attn(nqt, k_cmp, v_cmp_t, ks, vst, kw, vwt, smt)
        o_fox = _fox_attn(fq, _fox_cum(sm, fk, fox_f_bias[l]), fvt)
        shift2, sc1p2, gate2 = mod[l, 1, :, :, :d], mod[l, 1, :, :, d:2 * d], mod[l, 1, :, :, 2 * d:]
        x, hp, logits = _merge(x, sc1p, shift, gate, o_gdn, o_ret, o_nsa, o_fox, w_gate[l].astype(BF16),
                               branch_proj[l].astype(BF16), w_out[l].astype(BF16), ln_g[l, 0], ln_b[l, 0],
                               sc1p2, shift2, router_w, alpha)
        pos, wcol, tiles = _router(logits, router_b, MOE_TM)
        x = _moe(x, gate2, hp, pos, wcol, tiles, exp_w1[l].astype(BF16), exp_w3[l].astype(BF16),
                 exp_w2[l].astype(BF16), ln_g[l, 1], ln_b[l, 1], alpha, MOE_TM)
    return x
```

```python
import functools
import math

import jax
import jax.numpy as jnp
import numpy as np
from jax import lax
from jax.experimental import pallas as pl
from jax.experimental.pallas import tpu as pltpu

F32 = jnp.float32
BF16 = jnp.bfloat16

N_HEADS = 4
HEAD_DIM = 64
MIX_W = N_HEADS * HEAD_DIM
GDN_CHUNK = 64
CONV_K = 4
RET_CHUNK = 128
ROPE_BASE = 10000.0
NSA_CMP_LEN = 32
NSA_CMP_STRIDE = 16
NSA_SEL_LEN = 64
NSA_TOP_N = 16
NSA_WINDOW = 512
NSA_FORCE_SCORE = 1.0e4
N_EXPERTS = 16
N_GROUPS = 4
EXPERTS_PER_GROUP = N_EXPERTS // N_GROUPS
LN_EPS = 1e-5
LANES = 128
MXU_N = 256
SUM_ROWS = 16
NEG_BIG = -1.0e30
SEL_MASK_BIAS = NEG_BIG
LOG2E = math.log2(math.e)

IN_WIDTHS = (
    MIX_W, MIX_W, MIX_W, N_HEADS, N_HEADS, MIX_W,
    MIX_W, MIX_W, MIX_W, MIX_W,
    MIX_W, HEAD_DIM, HEAD_DIM, HEAD_DIM, HEAD_DIM, HEAD_DIM, HEAD_DIM, 3 * N_HEADS,
    MIX_W, MIX_W, MIX_W, N_HEADS,
)
_IN_OFF = np.concatenate([[0], np.cumsum(IN_WIDTHS)]).astype(int)
(_GQ, _GK, _GV, _GA, _GB, _GZ, _RQ, _RK, _RV, _RG, _NQ, _NKC, _NVC, _NKS, _NVS, _NKW, _NVW, _NGATE,
 _FQ, _FK, _FV, _FF) = range(22)
SM_A, SM_B, SM_GATE, SM_F = 0, 4, 8, 20

VMEM_LIMIT = 56 * 1024 * 1024


def _cp(sem, vmem=None):
    return pltpu.CompilerParams(dimension_semantics=sem, vmem_limit_bytes=vmem)


def _sigmoid(x):
    return 1.0 / (1.0 + jnp.exp(-x))


def _silu(x):
    return x * _sigmoid(x)


def _softplus(x):
    return jnp.maximum(x, 0.0) + jnp.log1p(jnp.exp(-jnp.abs(x)))


def _dot(a, b):
    return jnp.dot(a, b, preferred_element_type=F32)


def _dot_nt(a, b):
    return lax.dot_general(a, b, (((1,), (1,)), ((), ())), preferred_element_type=F32)


def _split3(a):
    a1 = a.astype(BF16)
    r = a - a1.astype(F32)
    a2 = r.astype(BF16)
    a3 = (r - a2.astype(F32)).astype(BF16)
    return a1, a2, a3


def _dot_x(a, m):
    a1, a2, a3 = _split3(a)
    return _dot(a1, m) + _dot(a2, m) + _dot(a3, m)


def _xdot(m, a):
    a1, a2, a3 = _split3(a)
    return _dot(m, a1) + _dot(m, a2) + _dot(m, a3)


def _dot_hp(a, b):
    a1, a2, _ = _split3(a)
    b1, b2, _ = _split3(b)
    return _dot(a1, b1) + (_dot(a1, b2) + _dot(a2, b1))


def _iota(shape, dim):
    return lax.broadcasted_iota(jnp.int32, shape, dim)


def _div(x, n):
    return lax.shift_right_logical(x, int(math.log2(n)))


def _mod(x, n):
    return x & (n - 1)


def _head_ones():
    return _div(_iota((MIX_W, MIX_W), 0), HEAD_DIM) == _div(_iota((MIX_W, MIX_W), 1), HEAD_DIM)


def _head_sum(x, ones):
    a1, a2, _ = _split3(x)
    return _dot(a1, ones) + _dot(a2, ones)


def _ada_body(c_ref, w_ref, b_ref, o_ref):
    mod = _dot_hp(_silu(c_ref[...]), w_ref[0]) + b_ref[0]
    o_ref[0] = jnp.where(pl.program_id(1) == 1, 1.0 + mod, mod)


def _ada_mod(c, ada_w, ada_b):
    depth = ada_w.shape[0]
    bsz, d = c.shape
    n = depth * 2
    w = ada_w.reshape(n, d, 3 * d)
    b = ada_b.reshape(n, 1, 3 * d)
    out = pl.pallas_call(
        _ada_body, grid=(n, 3),
        in_specs=[pl.BlockSpec((bsz, d), lambda i, j: (0, 0)),
                  pl.BlockSpec((1, d, d), lambda i, j: (i, 0, j)),
                  pl.BlockSpec((1, 1, d), lambda i, j: (i, 0, j))],
        out_specs=pl.BlockSpec((1, bsz, d), lambda i, j: (i, 0, j)),
        out_shape=jax.ShapeDtypeStruct((n, bsz, 3 * d), F32),
        compiler_params=_cp(("arbitrary", "arbitrary")), name="ada_mod",
    )(c, w, b)
    return out.reshape(depth, 2, bsz, 1, 3 * d)


IN_TS = 512
_W_GROUPS = (768, 256, 1024, 256, 512, 128)
_W_OFF = np.concatenate([[0], np.cumsum(_W_GROUPS)]).astype(int)
IN_CAT_W = int(_W_OFF[-1])
_WT_GROUPS = (LANES, MIX_W, HEAD_DIM, HEAD_DIM, MIX_W)
_WT_OFF = np.concatenate([[0], np.cumsum(_WT_GROUPS)]).astype(int)
IN_CAT_T = int(_WT_OFF[-1])


def _cat_in_weights(w_in):
    def col(i):
        return w_in[:, _IN_OFF[i]:_IN_OFF[i + 1]]
    small = jnp.concatenate([col(_GA), col(_GB), col(_NGATE), col(_FF)], axis=1)
    small = jnp.pad(small, ((0, 0), (0, LANES - small.shape[1])))
    scale = HEAD_DIM ** -0.5
    cat = jnp.concatenate([
        col(_GQ), col(_GK), col(_GV), col(_GZ),
        col(_RQ), col(_RK) * scale, col(_RV), col(_RG),
        col(_NKC), col(_NVC), col(_NKS), col(_NKW),
        col(_FQ) * (scale * LOG2E), col(_FK),
        small], axis=1)
    cat_t = jnp.concatenate([small, col(_NQ) * (scale * LOG2E), col(_NVS), col(_NVW), col(_FV)], axis=1).T
    return cat.astype(BF16), cat_t.astype(BF16)


def _inproj_body(x_ref, sc_ref, sh_ref, w_ref, wt_ref,
                 gqkv_ref, gz_ref, r_ref, kc_ref, vc_ref, ks_ref, kw_ref, fq_ref, fk_ref, sm_ref,
                 smt_ref, nqt_ref, vst_ref, vwt_ref, fvt_ref):
    h = (x_ref[0] * sc_ref[0] + sh_ref[0]).astype(BF16)

    def proj(g):
        return _dot(h, w_ref[:, _W_OFF[g]:_W_OFF[g + 1]])

    gqkv_ref[0] = proj(0)
    gz_ref[0] = proj(1)
    r_ref[0] = proj(2)
    nkv = proj(3)
    for i, ref in enumerate((kc_ref, vc_ref, ks_ref, kw_ref)):
        ref[0] = nkv[:, i * HEAD_DIM:(i + 1) * HEAD_DIM].astype(ref.dtype)
    f = proj(4)
    for j, ref in enumerate((fq_ref, fk_ref)):
        for hd in range(N_HEADS):
            lo = j * MIX_W + hd * HEAD_DIM
            ref[0, hd] = f[:, lo:lo + HEAD_DIM].astype(BF16)
    sm_ref[0] = proj(5)
    for g, ref in enumerate((smt_ref, nqt_ref, vst_ref, vwt_ref, fvt_ref)):
        ref[0] = _dot_nt(wt_ref[_WT_OFF[g]:_WT_OFF[g + 1], :], h).astype(ref.dtype)


def _in_proj(x, sc1p, shift, wcat, wcat_t):
    bsz, s, d = x.shape
    ts = min(IN_TS, s)
    tok = lambda w, dt: jax.ShapeDtypeStruct((bsz, s, w), dt)
    chan = lambda w, dt: jax.ShapeDtypeStruct((bsz, w, s), dt)
    hm = jax.ShapeDtypeStruct((bsz, N_HEADS, s, HEAD_DIM), BF16)
    out_shape = [tok(768, F32), tok(256, F32), tok(1024, F32),
                 tok(64, F32), tok(64, F32), tok(64, BF16), tok(64, BF16), hm, hm, tok(LANES, F32),
                 chan(LANES, F32), chan(MIX_W, BF16), chan(HEAD_DIM, BF16), chan(HEAD_DIM, BF16), chan(MIX_W, BF16)]
    tspec = lambda w: pl.BlockSpec((1, ts, w), lambda b, i: (b, i, 0))
    cspec = lambda w: pl.BlockSpec((1, w, ts), lambda b, i: (b, 0, i))
    hspec = pl.BlockSpec((1, N_HEADS, ts, HEAD_DIM), lambda b, i: (b, 0, i, 0))
    out_specs = [tspec(768), tspec(256), tspec(1024),
                 tspec(64), tspec(64), tspec(64), tspec(64), hspec, hspec, tspec(LANES),
                 cspec(LANES), cspec(MIX_W), cspec(HEAD_DIM), cspec(HEAD_DIM), cspec(MIX_W)]
    vec = pl.BlockSpec((1, 1, d), lambda b, i: (b, 0, 0))
    return pl.pallas_call(
        _inproj_body, grid=(bsz, s // ts),
        in_specs=[tspec(d), vec, vec,
                  pl.BlockSpec((d, IN_CAT_W), lambda b, i: (0, 0)),
                  pl.BlockSpec((IN_CAT_T, d), lambda b, i: (0, 0))],
        out_specs=out_specs, out_shape=out_shape,
        compiler_params=_cp(("arbitrary", "arbitrary"), VMEM_LIMIT), name="in_proj",
    )(x, sc1p, shift, wcat, wcat_t)


CUM_T = 512


def _foxcum_body(sm_ref, k_ref, brow_ref, kaug_ref, carry_r):
    @pl.when(pl.program_id(1) == 0)
    def _():
        carry_r[...] = jnp.zeros_like(carry_r)

    t = sm_ref.shape[1]

    def log_sigmoid(v):
        return jnp.minimum(v, 0.0) - jnp.log1p(jnp.exp(-jnp.abs(v)))

    lower = (_iota((t, t), 0) >= _iota((t, t), 1)).astype(BF16)
    cc = _xdot(lower, log_sigmoid(sm_ref[0] + brow_ref[...])) + carry_r[...]
    carry_r[...] = cc[t - 1:t, :]
    parts = _split3(-cc * LOG2E)
    place = (_iota((HEAD_DIM, LANES), 0) == _iota((HEAD_DIM, LANES), 1)).astype(BF16)
    src, dst = _iota((LANES, LANES), 0), _iota((LANES, LANES), 1)
    for hd in range(N_HEADS):
        aug = _dot(k_ref[0, hd], place)
        for j, part in enumerate(parts):
            aug = aug + _dot(part, ((src == SM_F + hd) & (dst == HEAD_DIM + j)).astype(BF16))
        kaug_ref[0, hd] = aug.astype(BF16)


def _fox_cum(small, k, f_bias):
    bsz, s, _ = small.shape
    t = min(CUM_T, s)
    brow = jnp.zeros((1, LANES), F32).at[0, SM_F:SM_F + N_HEADS].set(f_bias)
    return pl.pallas_call(
        _foxcum_body, grid=(bsz, s // t),
        in_specs=[pl.BlockSpec((1, t, LANES), lambda b, i: (b, i, 0)),
                  pl.BlockSpec((1, N_HEADS, t, HEAD_DIM), lambda b, i: (b, 0, i, 0)),
                  pl.BlockSpec((1, LANES), lambda b, i: (0, 0))],
        out_specs=pl.BlockSpec((1, N_HEADS, t, LANES), lambda b, i: (b, 0, i, 0)),
        out_shape=jax.ShapeDtypeStruct((bsz, N_HEADS, s, LANES), BF16),
        scratch_shapes=[pltpu.VMEM((1, LANES), F32)],
        compiler_params=_cp(("arbitrary", "arbitrary")), name="fox_cum",
    )(small, k, brow)


FOX_T = 512
FOX_UNROLL = 2


def _lockstep(gens):
    out, live = [None] * len(gens), list(range(len(gens)))
    while live:
        still = []
        for g in live:
            try:
                next(gens[g])
                still.append(g)
            except StopIteration as stop:
                out[g] = stop.value
        live = still
    return out


def _fox_body(q_ref, k_ref, vt_ref, o_ref):
    i = pl.program_id(1)
    t = q_ref.shape[2]
    place = (_iota((HEAD_DIM, LANES), 0) == _iota((HEAD_DIM, LANES), 1)).astype(BF16)
    lane = _iota((1, LANES), 1)
    ones = ((lane >= HEAD_DIM) & (lane < HEAD_DIM + 3)).astype(F32)
    causal = _iota((t, t), 0) <= _iota((t, t), 1)
    q_aug = [(_dot(q_ref[0, hd], place) + ones).astype(BF16) for hd in range(N_HEADS)]

    ones_rows = jnp.ones((SUM_ROWS, t), BF16)

    def head_steps(hd, chunks, carry, masked):
        m, acc = carry
        keys = [pl.ds(pl.multiple_of(j * t, t), t) for j in chunks]
        ahead = _dot_nt(k_ref[0, hd, keys[0], :], q_aug[hd])
        yield
        for u in range(len(chunks)):
            s = ahead
            if u + 1 < len(chunks):
                ahead = _dot_nt(k_ref[0, hd, keys[u + 1], :], q_aug[hd])
            if masked:
                s = jnp.where(causal, s, NEG_BIG)
            m_new = jnp.maximum(m, jnp.max(s, axis=0, keepdims=True))
            yield
            alpha = jnp.exp2(m - m_new)
            p = jnp.exp2(s - m_new)
            yield
            vt_aug = jnp.concatenate([vt_ref[0, hd * HEAD_DIM:(hd + 1) * HEAD_DIM, keys[u]], ones_rows], axis=0)
            m, acc = m_new, alpha * acc + _dot(vt_aug, p.astype(BF16))
            yield
        return m, acc

    def steps(chunks, carries, masked=False):
        return tuple(_lockstep([head_steps(hd, chunks, carries[hd], masked) for hd in range(N_HEADS)]))

    carries = tuple((jnp.full((1, t), NEG_BIG, F32), jnp.zeros((HEAD_DIM + SUM_ROWS, t), F32))
                    for _ in range(N_HEADS))
    n_groups = _div(i, FOX_UNROLL)
    carries = lax.fori_loop(0, n_groups, lambda g, c: steps([g * FOX_UNROLL + u for u in range(FOX_UNROLL)], c),
                            carries)
    carries = lax.fori_loop(n_groups * FOX_UNROLL, i, lambda j, c: steps([j], c), carries)
    carries = steps([i], carries, masked=True)
    o_ref[0] = jnp.concatenate([acc[:HEAD_DIM] / acc[HEAD_DIM:HEAD_DIM + 1] for (_, acc) in carries],
                               axis=0).T.astype(o_ref.dtype)


def _fox_attn(q, kaug, vt):
    bsz, nh, s, hd = q.shape
    t = min(FOX_T, s)
    return pl.pallas_call(
        _fox_body, grid=(bsz, s // t),
        in_specs=[pl.BlockSpec((1, nh, t, hd), lambda b, i: (b, 0, i, 0)),
                  pl.BlockSpec((1, nh, s, LANES), lambda b, i: (b, 0, 0, 0)),
                  pl.BlockSpec((1, MIX_W, s), lambda b, i: (b, 0, 0))],
        out_specs=pl.BlockSpec((1, t, MIX_W), lambda b, i: (b, i, 0)),
        out_shape=jax.ShapeDtypeStruct((bsz, s, MIX_W), BF16),
        compiler_params=_cp(("arbitrary", "arbitrary")), name="fox_attn",
    )(q, kaug, vt)


GDN_T = 256
GDN_NB = 4


def _gdn_body(qkv_ref, sm_ref, smt_ref, z_ref, cw_ref, alr_ref, dtr_ref, alc_ref, dtc_ref, nw_ref,
              o_ref, xbuf, state):
    t, c = GDN_T, GDN_CHUNK
    ri, ci = _iota((t, t), 0), _iota((t, t), 1)
    same = _div(ri, c) == _div(ci, c)
    er, ec = _iota((LANES, MIX_W), 0), _div(_iota((LANES, MIX_W), 1), HEAD_DIM)
    head_blk = _head_ones()
    masks = dict(same=same, incl=same & (ci <= ri), strict=same & (ci < ri),
                 same_b=same.astype(BF16), incl_b=(same & (ci <= ri)).astype(BF16),
                 incl_t_b=(same & (ri <= ci)).astype(BF16),
                 exp_a=(er == ec + SM_A).astype(BF16),
                 exp_b=(er == ec + SM_B).astype(BF16),
                 head_blk=head_blk, head_ones=head_blk.astype(BF16),
                 lane_head=_div(_iota((1, MIX_W), 1), HEAD_DIM))
    _lockstep([_gdn_tile(masks, qkv_ref.at[bb], sm_ref.at[bb], smt_ref.at[bb], z_ref.at[bb], cw_ref, alr_ref,
                         dtr_ref, alc_ref, dtc_ref, nw_ref, o_ref.at[bb], xbuf.at[bb], state.at[bb])
               for bb in range(qkv_ref.shape[0])])


def _gdn_tile(masks, qkv_ref, sm_ref, smt_ref, z_ref, cw_ref, alr_ref, dtr_ref, alc_ref, dtc_ref, nw_ref,
              o_ref, xbuf, state):
    t, c = GDN_T, GDN_CHUNK
    incl, strict, head_ones = masks["incl"], masks["strict"], masks["head_ones"]

    @pl.when(pl.program_id(1) == 0)
    def _():
        xbuf[0:8, :] = jnp.zeros((8, 3 * MIX_W), F32)
        state[...] = jnp.zeros_like(state)

    xbuf[8:8 + t, :] = qkv_ref[...]
    conv = cw_ref[0:1, :] * xbuf[pl.ds(8 - CONV_K + 1, t), :]
    for kk in range(1, CONV_K):
        conv = conv + cw_ref[kk:kk + 1, :] * xbuf[pl.ds(8 - CONV_K + 1 + kk, t), :]
    xbuf[0:8, :] = xbuf[t:t + 8, :]
    act = _silu(conv)
    q, k, v = act[:, :MIX_W], act[:, MIX_W:2 * MIX_W], act[:, 2 * MIX_W:]
    q = q * lax.rsqrt(_head_sum(q * q, head_ones) + 1e-6) * HEAD_DIM ** -0.5
    k = k * lax.rsqrt(_head_sum(k * k, head_ones) + 1e-6)

    sm, smt = sm_ref[...], smt_ref[...]
    la_col = -jnp.exp(alr_ref[...]) * _softplus(sm + dtr_ref[...])
    la_row = -jnp.exp(alc_ref[...]) * _softplus(smt + dtc_ref[...])
    beta_col = _sigmoid(sm)
    bc_col = _xdot(masks["incl_b"], la_col)
    last_col = _xdot(masks["same_b"], la_col)
    bc_row = _dot_x(la_row, masks["incl_t_b"])
    exp_a, exp_b = masks["exp_a"], masks["exp_b"]
    bc_x, last_x, beta_x = _dot_x(bc_col, exp_a), _dot_x(last_col, exp_a), _dot_x(beta_col, exp_b)
    eb = jnp.exp(bc_x)
    q_dec = q * eb
    k_dec = k * jnp.exp(last_x - bc_x)
    rhs_v = beta_x * v
    rhs_k = beta_x * eb * k
    rhs = jnp.concatenate([rhs_v, rhs_k], axis=1).astype(BF16)

    kb = k.astype(BF16)
    lane_head = masks["lane_head"]
    heads = range(N_HEADS)
    mh = [lane_head == hd for hd in heads]
    gram = [_dot_nt(jnp.where(mh[hd], k, 0.0).astype(BF16), kb) for hd in heads]
    qk = [_dot_nt(jnp.where(mh[hd], q, 0.0).astype(BF16), kb) for hd in heads]
    yield
    dec = [jnp.exp(jnp.minimum(bc_col[:, SM_A + hd:SM_A + hd + 1] - bc_row[SM_A + hd:SM_A + hd + 1, :], 0.0))
           for hd in heads]
    attn = [jnp.where(incl, qk[hd] * dec[hd], 0.0) for hd in heads]
    p = [jnp.where(strict, -(beta_col[:, SM_B + hd:SM_B + hd + 1] * gram[hd] * dec[hd]), 0.0) for hd in heads]
    tr = list(p)
    for _ in range(5):
        pb = [p[hd].astype(BF16) for hd in heads]
        p = [_dot(pb[hd], pb[hd]) for hd in heads]
        yield
        tr = [tr[hd] + p[hd] + _dot(tr[hd].astype(BF16), p[hd].astype(BF16)) for hd in heads]
    sol = [_dot(tr[hd].astype(BF16), rhs) for hd in heads]
    yield
    u0, w = rhs_v, rhs_k
    for hd in heads:
        u0 = u0 + jnp.where(mh[hd], sol[hd][:, :MIX_W], 0.0)
        w = w + jnp.where(mh[hd], sol[hd][:, MIX_W:], 0.0)

    head_blk = masks["head_blk"]
    k_dec_t = k_dec.T.astype(BF16)
    outs = []
    for n in range(t // c):
        rows = slice(n * c, (n + 1) * c)
        s_old = state[...]
        sb = s_old.astype(BF16)
        u_n = u0[rows] - _dot(w[rows].astype(BF16), sb)
        yield
        parts = ([jnp.zeros((n * c, MIX_W), F32)] if n else []) + [u_n]
        if t - (n + 1) * c:
            parts.append(jnp.zeros((t - (n + 1) * c, MIX_W), F32))
        u_full = jnp.concatenate(parts, axis=0).astype(BF16)
        oa = _dot(jnp.concatenate([a[rows] for a in attn], axis=0).astype(BF16), u_full)
        o_n = _dot(q_dec[rows].astype(BF16), sb)
        for hd in range(N_HEADS):
            o_n = o_n + jnp.where(lane_head == hd, oa[hd * c:(hd + 1) * c], 0.0)
        outs.append(o_n)
        g_row = jnp.exp(last_x[n * c:n * c + 1, :])
        state[...] = g_row * s_old + jnp.where(head_blk, _dot(k_dec_t, u_full), 0.0)
        yield
    o = jnp.concatenate(outs, axis=0)
    o = o * lax.rsqrt(_head_sum(o * o, head_ones) * (1.0 / HEAD_DIM) + 1e-6) * nw_ref[...]
    o_ref[...] = (o * _silu(z_ref[...])).astype(o_ref.dtype)


def _lane_vec(vals, off):
    return jnp.zeros((1, LANES), F32).at[0, off:off + vals.shape[0]].set(vals)


def _gdn(qkv, small, small_t, z, conv_w, a_log, dt_bias, norm_w):
    bsz, s, _ = qkv.shape
    t = GDN_T
    alr, dtr = _lane_vec(a_log, SM_A), _lane_vec(dt_bias, SM_A)
    nw = jnp.tile(norm_w, N_HEADS).reshape(1, MIX_W)
    const = lambda shape: pl.BlockSpec(shape, lambda b, i: (0,) * len(shape))
    nb = GDN_NB if bsz % GDN_NB == 0 else 1
    return pl.pallas_call(
        _gdn_body, grid=(bsz // nb, s // t),
        in_specs=[pl.BlockSpec((nb, t, 3 * MIX_W), lambda b, i: (b, i, 0)),
                  pl.BlockSpec((nb, t, LANES), lambda b, i: (b, i, 0)),
                  pl.BlockSpec((nb, LANES, t), lambda b, i: (b, 0, i)),
                  pl.BlockSpec((nb, t, MIX_W), lambda b, i: (b, i, 0)),
                  const((CONV_K, 3 * MIX_W)), const((1, LANES)), const((1, LANES)),
                  const((LANES, 1)), const((LANES, 1)), const((1, MIX_W))],
        out_specs=pl.BlockSpec((nb, t, MIX_W), lambda b, i: (b, i, 0)),
        out_shape=jax.ShapeDtypeStruct((bsz, s, MIX_W), BF16),
        scratch_shapes=[pltpu.VMEM((nb, t + 8, 3 * MIX_W), F32), pltpu.VMEM((nb, MIX_W, MIX_W), F32)],
        compiler_params=_cp(("arbitrary", "arbitrary")), name="gdn",
    )(qkv, small, small_t, z, conv_w, alr, dtr, alr.reshape(LANES, 1), dtr.reshape(LANES, 1), nw)


RET_T = 1024
_RET_LOG_GAMMA = tuple(math.log1p(-(2.0 ** (-5.0 - h))) for h in range(N_HEADS))


def _per_head(lane_head, vals):
    out = jnp.full(lane_head.shape, vals[0], F32)
    for hd in range(1, N_HEADS):
        out = jnp.where(lane_head == hd, vals[hd], out)
    return out


def _ret_body(r_ref, pos_ref, gnw_ref, o_ref, state):
    t, c = r_ref.shape[1], RET_CHUNK

    @pl.when(pl.program_id(1) == 0)
    def _():
        state[...] = jnp.zeros_like(state)

    x = r_ref[0]
    lane = _iota((1, MIX_W), 1)
    lane_head = _div(lane, HEAD_DIM)
    half = HEAD_DIM // 2
    inv_freq = jnp.exp(_mod(_iota((1, LANES), 1), half).astype(F32) * (-math.log(ROPE_BASE) / half))
    ang = pos_ref[0].astype(F32) * inv_freq
    cos, sin = jnp.cos(ang), jnp.sin(ang)
    cos, sin = jnp.concatenate([cos, cos], axis=1), jnp.concatenate([sin, sin], axis=1)
    first = _mod(lane, HEAD_DIM) < half

    def rope(a):
        rot = jnp.where(first, -pltpu.roll(a, MIX_W - half, 1), pltpu.roll(a, half, 1))
        return a * cos + rot * sin

    q, k = rope(x[:, :MIX_W]), rope(x[:, MIX_W:2 * MIX_W])
    v, g = x[:, 2 * MIX_W:3 * MIX_W], x[:, 3 * MIX_W:]

    lg = _per_head(lane_head, _RET_LOG_GAMMA)
    cidx = _iota((c, 1), 0).astype(F32)
    xi = jnp.exp((cidx + 1.0) * lg)
    zeta = jnp.exp((c - 1.0 - cidx) * lg)
    chunk_decay = jnp.exp(float(c) * lg)
    rel = (_iota((c, c), 0) - _iota((c, c), 1)).astype(F32)
    dec = jnp.concatenate([jnp.where(rel >= 0, jnp.exp(jnp.maximum(rel, 0.0) * _RET_LOG_GAMMA[hd]), 0.0)
                           for hd in range(N_HEADS)], axis=0)
    head_blk = _head_ones()
    outs = []
    for n in range(t // c):
        rows = slice(n * c, (n + 1) * c)
        qn, kn, vn = q[rows], k[rows], v[rows].astype(BF16)
        qs = jnp.concatenate([jnp.where(lane_head == hd, qn, 0.0) for hd in range(N_HEADS)], axis=0).astype(BF16)
        sc = _dot_nt(qs, kn.astype(BF16)) * dec
        res = _dot(sc.astype(BF16), vn)
        s_old = state[...]
        o_n = _dot(qn.astype(BF16), s_old.astype(BF16)) * xi
        for hd in range(N_HEADS):
            o_n = o_n + jnp.where(lane_head == hd, res[hd * c:(hd + 1) * c], 0.0)
        outs.append(o_n)
        kv = _dot((kn * zeta).T.astype(BF16), vn)
        state[...] = chunk_decay * s_old + jnp.where(head_blk, kv, 0.0)
    o = jnp.concatenate(outs, axis=0)
    head_ones = head_blk.astype(BF16)
    mu = _head_sum(o, head_ones) * (1.0 / HEAD_DIM)
    xc = o - mu
    var = _head_sum(xc * xc, head_ones) * (1.0 / HEAD_DIM)
    y = xc * lax.rsqrt(var + LN_EPS) * gnw_ref[...]
    o_ref[0] = (_silu(g) * y).astype(o_ref.dtype)


def _retention(r, positions, gn_w):
    bsz, s, _ = r.shape
    t = min(RET_T, s)
    return pl.pallas_call(
        _ret_body, grid=(bsz, s // t),
        in_specs=[pl.BlockSpec((1, t, 4 * MIX_W), lambda b, i: (b, i, 0)),
                  pl.BlockSpec((1, t, 1), lambda b, i: (b, i, 0)),
                  pl.BlockSpec((1, MIX_W), lambda b, i: (0, 0))],
        out_specs=pl.BlockSpec((1, t, MIX_W), lambda b, i: (b, i, 0)),
        out_shape=jax.ShapeDtypeStruct((bsz, s, MIX_W), BF16),
        scratch_shapes=[pltpu.VMEM((MIX_W, MIX_W), F32)],
        compiler_params=_cp(("arbitrary", "arbitrary")), name="retention",
    )(r, positions.reshape(bsz, s, 1), gn_w.reshape(1, MIX_W))


def _nsacmp_body(kc_ref, vc_ref, pe_ref, kw1_ref, kw2_ref, vw1_ref, vw2_ref, ko_ref, vo_ref):
    half = NSA_CMP_STRIDE * HEAD_DIM
    nc = kc_ref.shape[1]
    def hidden(x_ref, w1_ref):
        x = x_ref[0]
        top = _dot_hp(x, w1_ref[0:half, :])
        bot = _dot_hp(x, w1_ref[half:2 * half, :])
        bias = _dot_hp(pe_ref[...], w1_ref[...])[0:1, :]
        return _silu(top + pltpu.roll(bot, nc - 1, 0) + bias)

    ko_ref[0] = _dot_hp(hidden(kc_ref, kw1_ref), kw2_ref[...])
    a1, a2, _ = _split3(vw2_ref[...])
    b1, b2, _ = _split3(hidden(vc_ref, vw1_ref))
    vo_ref[0] = _dot_nt(a1, b1) + (_dot_nt(a1, b2) + _dot_nt(a2, b1))


def _nsa_compress(kc, vc, pe, ck_w1, ck_w2, cv_w1, cv_w2):
    bsz, s, hd = kc.shape
    nc = s // NSA_CMP_STRIDE
    wide = NSA_CMP_STRIDE * hd
    pe8 = jnp.broadcast_to(pe.reshape(1, NSA_CMP_LEN * hd), (8, NSA_CMP_LEN * hd))
    cv_w2t = cv_w2.T
    const = lambda a: pl.BlockSpec(a.shape, lambda b: (0,) * a.ndim)
    xspec = pl.BlockSpec((1, nc, wide), lambda b: (b, 0, 0))
    return pl.pallas_call(
        _nsacmp_body, grid=(bsz,),
        in_specs=[xspec, xspec, const(pe8), const(ck_w1), const(ck_w2), const(cv_w1), const(cv_w2t)],
        out_specs=[pl.BlockSpec((1, nc, hd), lambda b: (b, 0, 0)), pl.BlockSpec((1, hd, nc), lambda b: (b, 0, 0))],
        out_shape=[jax.ShapeDtypeStruct((bsz, nc, hd), F32), jax.ShapeDtypeStruct((bsz, hd, nc), F32)],
        compiler_params=_cp(("arbitrary",)), name="nsa_compress",
    )(kc.reshape(bsz, nc, wide), vc.reshape(bsz, nc, wide), pe8, ck_w1, ck_w2, cv_w1, cv_w2t)


NSA_TQ = 256
NSA_TK = 512
NSA_UNROLL = 2


def _softmax_weights(s, valid):
    s = jnp.where(valid, s, NEG_BIG)
    m = jnp.max(s, axis=0, keepdims=True)
    m = jnp.where(m > 0.5 * NEG_BIG, m, 0.0)
    return jnp.exp2(s - m)


def _with_ones(vt):
    return jnp.concatenate([vt, jnp.ones((SUM_ROWS, vt.shape[1]), BF16)], axis=0)


def _nsa_body(qt_ref, kc_ref, vct_ref, ks_ref, vst_ref, kw_ref, vwt_ref, smt_ref, o_ref, kaug):
    i = pl.program_id(1)
    tq, hd = qt_ref.shape[2], HEAD_DIM
    s, nc = ks_ref.shape[1], kc_ref.shape[1]
    nsel = s // NSA_SEL_LEN
    n_top = min(NSA_TOP_N, nsel)
    cols = N_HEADS * tq

    @pl.when(i == 0)
    def _():
        place = (_iota((hd, LANES), 0) == _iota((hd, LANES), 1)).astype(BF16)
        onehot = (_iota((s, LANES), 1) - hd) == _div(_iota((s, LANES), 0), NSA_SEL_LEN)
        kaug[...] = (_dot(ks_ref[0], place) + onehot.astype(F32)).astype(BF16)

    t0 = i * tq
    qt = qt_ref[0]
    q4 = jnp.concatenate([qt[h * hd:(h + 1) * hd, :] for h in range(N_HEADS)], axis=1)
    tpos = t0 + _iota((1, tq), 1)
    tpos4 = t0 + _mod(_iota((1, cols), 1), tq)

    def cmp_branch():
        sc = _dot(kc_ref[0].astype(BF16), q4)
        yield
        cmp_end = _iota((nc, 1), 0) * NSA_CMP_STRIDE + (NSA_CMP_LEN - 1)
        e = _softmax_weights(sc, cmp_end <= tpos4)
        den = jnp.sum(e, axis=0, keepdims=True)
        p = e * (1.0 / jnp.where(den > 0.0, den, 1.0))
        yield
        return p, _dot(vct_ref[0].astype(BF16), p.astype(BF16))

    def win_branch():
        wlen = NSA_WINDOW + tq
        start = pl.multiple_of(jnp.maximum(t0 - NSA_WINDOW, 0), tq)
        sc = _dot(kw_ref[0, pl.ds(start, wlen), :], q4)
        yield
        kpos = start + _iota((wlen, 1), 0)
        e = _softmax_weights(sc, (kpos <= tpos4) & (kpos > tpos4 - NSA_WINDOW))
        yield
        r = _dot(_with_ones(vwt_ref[0, :, pl.ds(start, wlen)]), e.astype(BF16))
        return r[:hd] / r[hd:hd + 1]

    (p_cmp, o_cmp), o_win = _lockstep([cmp_branch(), win_branch()])

    psum = p_cmp[:, 0:tq] + p_cmp[:, tq:2 * tq] + p_cmp[:, 2 * tq:3 * tq] + p_cmp[:, 3 * tq:4 * tq]
    ss, cs = _iota((hd, nc), 0) * NSA_SEL_LEN, _iota((hd, nc), 1) * NSA_CMP_STRIDE
    overlap = jnp.clip(jnp.minimum(cs + NSA_CMP_LEN, ss + NSA_SEL_LEN) - jnp.maximum(cs, ss), 0, NSA_CMP_LEN)
    importance = _xdot((overlap.astype(F32) * (1.0 / NSA_CMP_LEN)).astype(BF16), psum)
    blk = _iota((hd, 1), 0)
    cur = _div(tpos, NSA_SEL_LEN)
    forced = (blk == 0) | (blk == cur) | (blk == cur - 1)
    score = jnp.where(forced, NSA_FORCE_SCORE, jnp.where(blk * NSA_SEL_LEN <= tpos, importance, -1.0))
    score = jnp.where(blk < nsel, score, -2.0)
    groups = [score[8 * g:8 * g + 8, :] for g in range(hd // 8)]
    gblk = _iota((8, 1), 0)
    cnts = [jnp.zeros((8, tq), jnp.int32) for _ in groups]
    for j in range(nsel):
        row = score[j:j + 1, :]
        for g in range(hd // 8):
            if 8 * g > j:
                ahead = row >= groups[g]
            elif 8 * g + 7 <= j:
                ahead = row > groups[g]
            else:
                ahead = (row > groups[g]) | ((row == groups[g]) & (gblk + 8 * g > j))
            cnts[g] = cnts[g] + ahead.astype(jnp.int32)
    bias = jnp.where(jnp.concatenate(cnts, axis=0) < n_top, 0.0, SEL_MASK_BIAS).astype(BF16)
    q_aug = jnp.concatenate([q4, jnp.concatenate([bias] * N_HEADS, axis=1)], axis=0)

    def sel_keys(c):
        return pl.ds(pl.multiple_of(c * NSA_TK, NSA_TK), NSA_TK)

    def sel_scores(c):
        return _dot(kaug[sel_keys(c), :], q_aug)

    def sel_absorb(c, sc, carry):
        m, acc = carry
        m_new = jnp.maximum(m, jnp.max(sc, axis=0, keepdims=True))
        alpha = jnp.exp2(m - m_new)
        p = jnp.exp2(sc - m_new)
        return m_new, alpha * acc + _dot(_with_ones(vst_ref[0, :, sel_keys(c)]), p.astype(BF16))

    def sel_step(c, carry, masked=False):
        sc = sel_scores(c)
        if masked:
            sc = jnp.where(c * NSA_TK + _iota((NSA_TK, 1), 0) <= tpos4, sc, NEG_BIG)
        return sel_absorb(c, sc, carry)

    def sel_group(g, carry):
        chunks = [g * NSA_UNROLL + u for u in range(NSA_UNROLL)]
        ahead = sel_scores(chunks[0])
        for u, c in enumerate(chunks):
            sc, ahead = ahead, (sel_scores(chunks[u + 1]) if u + 1 < NSA_UNROLL else None)
            carry = sel_absorb(c, sc, carry)
        return carry

    n_full = _div(t0, NSA_TK)
    n_groups = _div(n_full, NSA_UNROLL)
    carry = (jnp.full((1, cols), NEG_BIG, F32), jnp.zeros((hd + SUM_ROWS, cols), F32))
    carry = lax.fori_loop(0, n_groups, sel_group, carry)
    carry = lax.fori_loop(n_groups * NSA_UNROLL, n_full, sel_step, carry)
    _, acc_sel = sel_step(n_full, carry, masked=True)
    o_sel = acc_sel[:hd] / acc_sel[hd:hd + 1]

    gates =_sigmoid(smt_ref[0][SM_GATE:SM_GATE + 16, :])
    outs = []
    for h in range(N_HEADS):
        c = slice(h * tq, (h + 1) * tq)
        outs.append(gates[3 * h:3 * h + 1, :] * o_cmp[:, c] + gates[3 * h + 1:3 * h + 2, :] * o_sel[:, c]
                    + gates[3 * h + 2:3 * h + 3, :] * o_win[:, c])
    o_ref[0] = jnp.concatenate(outs, axis=0).T.astype(o_ref.dtype)


def _nsa_attn(qt, k_cmp, v_cmp_t, ks, vst, kw, vwt, small_t):
    bsz, _, s = qt.shape
    tq, hd = NSA_TQ, HEAD_DIM
    assert s % NSA_TK == 0 and NSA_TK % tq == 0 and s >= NSA_WINDOW + tq and s // NSA_SEL_LEN <= hd
    nc = k_cmp.shape[1]
    keys = pl.BlockSpec((1, s, hd), lambda b, i: (b, 0, 0))
    vals = pl.BlockSpec((1, hd, s), lambda b, i: (b, 0, 0))
    return pl.pallas_call(
        _nsa_body, grid=(bsz, s // tq),
        in_specs=[pl.BlockSpec((1, MIX_W, tq), lambda b, i: (b, 0, i)),
                  pl.BlockSpec((1, nc, hd), lambda b, i: (b, 0, 0)),
                  pl.BlockSpec((1, hd, nc), lambda b, i: (b, 0, 0)),
                  keys, vals, keys, vals,
                  pl.BlockSpec((1, LANES, tq), lambda b, i: (b, 0, i))],
        out_specs=pl.BlockSpec((1, tq, MIX_W), lambda b, i: (b, i, 0)),
        out_shape=jax.ShapeDtypeStruct((bsz, s, MIX_W), BF16),
        scratch_shapes=[pltpu.VMEM((s, LANES), BF16)],
        compiler_params=_cp(("arbitrary", "arbitrary")), name="nsa_attn",
    )(qt, k_cmp, v_cmp_t, ks, vst, kw, vwt, small_t)


MERGE_T = 1024


def _layer_norm(r, g, b):
    mu = jnp.mean(r, axis=-1, keepdims=True)
    xc = r - mu
    var = jnp.mean(xc * xc, axis=-1, keepdims=True)
    return xc * lax.rsqrt(var + LN_EPS) * g + b


def _merge_body(alpha, x_ref, sc_ref, sh_ref, gt_ref, og_ref, or_ref, on_ref, of_ref, wg_ref, bp_ref, wo_ref,
                lng_ref, lnb_ref, sc2_ref, sh2_ref, rwt_ref, o_ref, hp_ref, lg_ref):
    x = x_ref[0]
    h = (x * sc_ref[0] + sh_ref[0]).astype(BF16)

    merged = None
    for br, o_br in enumerate((og_ref, or_ref, on_ref, of_ref)):
        term = _sigmoid(_dot(h, wg_ref[br])) * _dot(o_br[0], bp_ref[br])
        merged = term if merged is None else merged + term
    y = _dot(merged.astype(BF16), wo_ref[...])
    x_new = _layer_norm(alpha * x + gt_ref[0] * y, lng_ref[...], lnb_ref[...])
    o_ref[0] = x_new
    h2 = x_new * sc2_ref[0] + sh2_ref[0]
    half = h2.shape[1] // 2
    hp_ref[0] = _pack_bf16_pair(h2[:, :half], h2[:, half:])
    a1, a2, _ = _split3(rwt_ref[...])
    b1, b2, _ = _split3(h2)
    lg_ref[0] = _dot_nt(a1, b1) + (_dot_nt(a1, b2) + _dot_nt(a2, b1))


def _merge(x, sc1p, shift, gate, o_gdn, o_ret, o_nsa, o_fox, w_gate, branch_proj, w_out, ln_g, ln_b,
           sc1p_moe, shift_moe, router_w, alpha):
    bsz, s, d = x.shape
    t = min(MERGE_T, s)
    tok = lambda w: pl.BlockSpec((1, t, w), lambda b, i: (b, i, 0))
    vec = pl.BlockSpec((1, 1, d), lambda b, i: (b, 0, 0))
    const = lambda a: pl.BlockSpec(a.shape, lambda b, i: (0,) * a.ndim, pipeline_mode=pl.Buffered(1))
    lng, lnb, rwt = ln_g.reshape(1, d), ln_b.reshape(1, d), router_w.T
    return pl.pallas_call(
        functools.partial(_merge_body, alpha), grid=(bsz, s // t),
        in_specs=[tok(d), vec, vec, vec, tok(MIX_W), tok(MIX_W), tok(MIX_W), tok(MIX_W),
                  const(w_gate), const(branch_proj), const(w_out), const(lng), const(lnb), vec, vec, const(rwt)],
        out_specs=[tok(d), tok(d // 2), pl.BlockSpec((1, N_EXPERTS, t), lambda b, i: (b, 0, i))],
        out_shape=[jax.ShapeDtypeStruct((bsz, s, d), F32), jax.ShapeDtypeStruct((bsz, s, d // 2), jnp.uint32),
                   jax.ShapeDtypeStruct((bsz, N_EXPERTS, s), F32)],
        compiler_params=_cp(("arbitrary", "arbitrary"), VMEM_LIMIT), name="merge",
    )(x, sc1p, shift, gate, o_gdn, o_ret, o_nsa, o_fox, w_gate, branch_proj, w_out, lng, lnb,
      sc1p_moe, shift_moe, rwt)


MOE_TM = 256


def _pack_bf16_pair(a, b):
    hi = lax.bitcast_convert_type(a.astype(BF16).astype(F32), jnp.uint32)
    lo = lax.bitcast_convert_type(b.astype(BF16).astype(F32), jnp.uint32)
    return hi | lax.shift_right_logical(lo, jnp.uint32(16))


def _unpack_bf16_pair(w):
    a = lax.bitcast_convert_type(w & jnp.uint32(0xFFFF0000), F32)
    b = lax.bitcast_convert_type(lax.shift_left(w, jnp.uint32(16)), F32)
    return a, b


def _router_body(tm, lg_ref, rb_ref, pos_ref, wcol_ref, tile_ref):
    s = lg_ref.shape[2]
    scores = _sigmoid(lg_ref[0])
    biased = scores + rb_ref[...]
    b = [biased[e:e + 1, :] for e in range(N_EXPERTS)]
    sc = [scores[e:e + 1, :] for e in range(N_EXPERTS)]
    gs = []
    for g in range(N_GROUPS):
        m = [b[EXPERTS_PER_GROUP * g + j] for j in range(EXPERTS_PER_GROUP)]
        best = m[0] + m[1]
        for u in range(EXPERTS_PER_GROUP):
            for v in range(u + 1, EXPERTS_PER_GROUP):
                if (u, v) != (0, 1):
                    best = jnp.maximum(best, m[u] + m[v])
        gs.append(best)
    gsel, best = jnp.zeros((1, s), jnp.int32), gs[0]
    for g in range(1, N_GROUPS):
        take = gs[g] > best
        gsel = jnp.where(take, g, gsel)
        best = jnp.where(take, gs[g], best)
    first, second = [], []
    for e in range(N_EXPERTS):
        g = e // EXPERTS_PER_GROUP
        cnt = jnp.zeros((1, s), jnp.int32)
        for j in range(EXPERTS_PER_GROUP * g, EXPERTS_PER_GROUP * (g + 1)):
            if j != e:
                ahead = (b[j] >= b[e]) if j < e else (b[j] > b[e])
                cnt = cnt + ahead.astype(jnp.int32)
        first.append((gsel == g) & (cnt == 0))
        second.append((gsel == g) & (cnt == 1))
    s0 = sum(jnp.where(first[e], sc[e], 0.0) for e in range(N_EXPERTS))
    s1 = sum(jnp.where(second[e], sc[e], 0.0) for e in range(N_EXPERTS))
    den = s0 + s1
    w0, w1 = s0 / den, s1 / den
    blk = min(512, s)
    eye = (_iota((blk, blk), 0) == _iota((blk, blk), 1)).astype(BF16)
    rowid = _iota((LANES, blk), 0)
    for j in range(s // blk):
        cols = slice(j * blk, (j + 1) * blk)
        wb = jnp.where(rowid == 0, w0[:, cols], jnp.where(rowid == 1, w1[:, cols], 0.0))
        wcol_ref[0, cols, :] = sum(_dot_nt(eye, part) for part in _split3(wb))
    onehot = jnp.concatenate([(first[e] | second[e]).astype(F32) for e in range(N_EXPERTS)], axis=0)
    before = (_iota((blk, blk), 0) < _iota((blk, blk), 1)).astype(BF16)
    carry, ranks = jnp.zeros((N_EXPERTS, 1), F32), []
    for j in range(s // blk):
        ob = onehot[:, j * blk:(j + 1) * blk]
        ranks.append(_dot(ob.astype(BF16), before) + carry)
        carry = carry + jnp.sum(ob, axis=1, keepdims=True)
    rank = jnp.concatenate(ranks, axis=1)
    ntile = jnp.floor((carry + (tm - 1.0)) * (1.0 / tm))
    lower = (_iota((N_EXPERTS, N_EXPERTS), 1) < _iota((N_EXPERTS, N_EXPERTS), 0)).astype(BF16)
    toff = _dot(lower, jnp.broadcast_to(ntile, (N_EXPERTS, LANES)).astype(BF16))[:, 0:1]
    slot = toff * float(tm) + rank
    pos0 = sum(jnp.where(first[e], slot[e:e + 1, :], 0.0) for e in range(N_EXPERTS))
    pos1 = sum(jnp.where(second[e], slot[e:e + 1, :], 0.0) for e in range(N_EXPERTS))
    pos_ref[0] = jnp.concatenate([pos0, pos1], axis=1).astype(jnp.int32)
    tend = toff + ntile
    tid = _iota((1, LANES), 1).astype(F32)
    texp = jnp.sum((tend <= tid).astype(F32), axis=0, keepdims=True)
    ntot = jnp.broadcast_to(jnp.sum(ntile, axis=0, keepdims=True), (1, LANES))
    diag = _iota((N_EXPERTS, LANES), 0) == _iota((N_EXPERTS, LANES), 1)
    to_lanes = lambda col: jnp.sum(jnp.where(diag, col, 0.0), axis=0, keepdims=True)
    tile_ref[0] = jnp.concatenate([jnp.minimum(texp, N_EXPERTS - 1.0), ntot, to_lanes(carry), to_lanes(toff)],
                                  axis=0).astype(jnp.int32)


def _router(logits, router_b, tm):
    bsz, _, s = logits.shape
    return pl.pallas_call(
        functools.partial(_router_body, tm), grid=(bsz,),
        in_specs=[pl.BlockSpec((1, N_EXPERTS, s), lambda b: (b, 0, 0)),
                  pl.BlockSpec((N_EXPERTS, 1), lambda b: (0, 0))],
        out_specs=[pl.BlockSpec((1, 1, 2 * s), lambda b: (b, 0, 0)),
                   pl.BlockSpec((1, s, LANES), lambda b: (b, 0, 0)),
                   pl.BlockSpec((1, 4, LANES), lambda b: (b, 0, 0))],
        out_shape=[jax.ShapeDtypeStruct((bsz, 1, 2 * s), jnp.int32),
                   jax.ShapeDtypeStruct((bsz, s, LANES), F32),
                   jax.ShapeDtypeStruct((bsz, 4, LANES), jnp.int32)],
        compiler_params=_cp(("arbitrary",)), name="router",
    )(logits, router_b.reshape(N_EXPERTS, 1))


LN_ROWS = 512


def _moe_body(alpha, tm, nt, tiles_ref, hp_ref, x_ref, wcol_ref, gt_ref, pos_ref, posn_ref,
              w1_ref, w3_ref, w2_ref, lng_ref, lnb_ref, o_ref, src, xs, ysall, st0, st1):
    b, i = pl.program_id(0), pl.program_id(1)
    s = hp_ref.shape[1]
    tb = b * (4 * LANES)
    span = src.shape[0] // 2
    mine, nxt = lax.rem(b, 2) * span, lax.rem(b + 1, 2) * span

    def gather(buf, base, lo, hi):
        for r in range(lo, hi):
            xs[buf, pl.ds(r, 1), :] = hp_ref[0, pl.ds(src[mine + base + r], 1), :]

    @pl.when(i == 0)
    def _():
        @pl.when(b == 0)
        def _():
            def clear(p, c):
                src[p] = 0
                return c
            lax.fori_loop(0, src.shape[0], clear, 0, unroll=8)

            def fill(t, c):
                for k in range(2):
                    src[mine + pos_ref[0, 0, k * s + t]] = t
                return c
            lax.fori_loop(0, s, fill, 0, unroll=8)
        gather(0, 0, 0, tm)

    @pl.when(i < tiles_ref[tb + LANES])
    def _():
        base = pl.multiple_of(i * tm, tm)
        cur = lax.rem(i, 2)
        f, d = w1_ref.shape[2], w2_ref.shape[2]
        n_piece = 2 * (f // MXU_N) + d // MXU_N
        rows = tm // n_piece
        piece = iter(range(n_piece))

        fill_rows = tm // (2 * n_piece)

        def gather_ahead():
            p = next(piece)
            gather(1 - cur, base + tm, p * rows, (p + 1) * rows)
            for r in range(fill_rows):
                t = jnp.minimum(i * (tm // 2) + p * fill_rows + r, s - 1)
                for k in range(2):
                    src[nxt + posn_ref[0, 0, k * s + t]] = t

        xa, xb = _unpack_bf16_pair(xs[cur])
        x_t = jnp.concatenate([xa, xb], axis=1).astype(BF16)
        up = []
        for w_ref in (w1_ref, w3_ref):
            cols = []
            for j in range(f // MXU_N):
                cols.append(_dot(x_t, w_ref[0, :, j * MXU_N:(j + 1) * MXU_N]))
                gather_ahead()
            up.append(jnp.concatenate(cols, axis=1))
        hid = (_silu(up[0]) * up[1]).astype(BF16)
        ys = []
        for j in range(d // MXU_N):
            ys.append(_dot(hid, w2_ref[0, :, j * MXU_N:(j + 1) * MXU_N]))
            gather_ahead()
        half = d // (2 * MXU_N)
        ysall[pl.ds(base, tm), :] = _pack_bf16_pair(jnp.concatenate(ys[:half], axis=1),
                                                    jnp.concatenate(ys[half:], axis=1))

    @pl.when(i >= nt)
    def _():
        t0 = (i - nt) * LN_ROWS

        def fetch(r, c):
            st0[pl.ds(r, 1), :] = ysall[pl.ds(pos_ref[0, 0, t0 + r], 1), :]
            st1[pl.ds(r, 1), :] = ysall[pl.ds(pos_ref[0, 0, s + t0 + r], 1), :]
            return c
        lax.fori_loop(0, LN_ROWS, fetch, 0, unroll=True)
        a0, b0 = _unpack_bf16_pair(st0[...])
        a1, b1 = _unpack_bf16_pair(st1[...])
        w0, w1 = wcol_ref[0][:, 0:1], wcol_ref[0][:, 1:2]
        y = jnp.concatenate([w0 * a0 + w1 * a1, w0 * b0 + w1 * b1], axis=1)
        o_ref[0] = _layer_norm(alpha * x_ref[0] + gt_ref[0] * y, lng_ref[...], lnb_ref[...])


def _moe(x, gate, hp, pos, wcol, tiles, w1, w3, w2, ln_g, ln_b, alpha, tm):
    bsz, s, d = x.shape
    f = w1.shape[2]
    nt = 2 * s // tm + N_EXPERTS
    assert nt <= LANES and tm & (tm - 1) == 0 and s % LN_ROWS == 0
    blk = lambda i: jnp.maximum(i - nt, 0)
    vec = pl.BlockSpec((1, 1, d), lambda b, i, tl: (b, 0, 0))
    tok = lambda w: pl.BlockSpec((1, LN_ROWS, w), lambda b, i, tl: (b, blk(i), 0))
    row = pl.BlockSpec((1, d), lambda b, i, tl: (0, 0))
    expert = lambda shape: pl.BlockSpec(shape, lambda b, i, tl: (tl[b * (4 * LANES) + jnp.minimum(i, nt - 1)], 0, 0))
    grid_spec = pltpu.PrefetchScalarGridSpec(
        num_scalar_prefetch=1, grid=(bsz, nt + s // LN_ROWS),
        in_specs=[pl.BlockSpec((1, s, d // 2), lambda b, i, tl: (b, 0, 0), pipeline_mode=pl.Buffered(1)),
                  tok(d), tok(LANES), vec,
                  pl.BlockSpec((1, 1, 2 * s), lambda b, i, tl: (b, 0, 0), memory_space=pltpu.SMEM),
                  pl.BlockSpec((1, 1, 2 * s), lambda b, i, tl: (jnp.minimum(b + 1, bsz - 1), 0, 0),
                               memory_space=pltpu.SMEM),
                  expert((1, d, f)), expert((1, d, f)), expert((1, f, d)), row, row],
        out_specs=tok(d),
        scratch_shapes=[pltpu.SMEM((2 * (nt + 1) * tm,), jnp.int32), pltpu.VMEM((2, tm, d // 2), jnp.uint32),
                        pltpu.VMEM((nt * tm, d // 2), jnp.uint32),
                        pltpu.VMEM((LN_ROWS, d // 2), jnp.uint32), pltpu.VMEM((LN_ROWS, d // 2), jnp.uint32)])
    return pl.pallas_call(
        functools.partial(_moe_body, alpha, tm, nt), grid_spec=grid_spec,
        out_shape=jax.ShapeDtypeStruct((bsz, s, d), F32),
        compiler_params=_cp(("arbitrary", "arbitrary"), VMEM_LIMIT), name="moe_experts",
    )(tiles.reshape(-1), hp, x, wcol, gate, pos, pos, w1, w3, w2, ln_g.reshape(1, d), ln_b.reshape(1, d))


def kernel(x, c, positions, ada_w, ada_b, w_in, gdn_conv_w, gdn_a_log, gdn_dt_bias, gdn_norm_w, ret_gn_w,
           nsa_cmp_pe, nsa_ck_w1, nsa_ck_w2, nsa_cv_w1, nsa_cv_w2, fox_f_bias, branch_proj, w_gate, w_out,
           ln_g, ln_b, router_w, router_b, exp_w1, exp_w3, exp_w2):
    depth, d = w_in.shape[0], x.shape[-1]
    alpha = (2.0 * depth) ** 0.25
    mod = _ada_mod(c, ada_w, ada_b)
    for l in range(depth):
        shift, sc1p, gate = mod[l, 0, :, :, :d], mod[l, 0, :, :, d:2 * d], mod[l, 0, :, :, 2 * d:]
        wcat, wcat_t = _cat_in_weights(w_in[l])
        (gqkv, gz, r, kc, vc, ks, kw, fq, fk, sm,
         smt, nqt, vst, vwt, fvt) = _in_proj(x, sc1p, shift, wcat, wcat_t)
        o_gdn = _gdn(gqkv, sm, smt, gz, gdn_conv_w[l], gdn_a_log[l], gdn_dt_bias[l], gdn_norm_w[l])
        o_ret = _retention(r, positions, ret_gn_w[l])
        k_cmp, v_cmp_t = _nsa_compress(kc, vc, nsa_cmp_pe[l], nsa_ck_w1[l], nsa_ck_w2[l], nsa_cv_w1[l], nsa_cv_w2[l])
        o_nsa = _nsa_attn(nqt, k_cmp, v_cmp_t, ks, vst, kw, vwt, smt)
        o_fox = _fox_attn(fq, _fox_cum(sm, fk, fox_f_bias[l]), fvt)
        shift2, sc1p2, gate2 = mod[l, 1, :, :, :d], mod[l, 1, :, :, d:2 * d], mod[l, 1, :, :, 2 * d:]
        x, hp, logits = _merge(x, sc1p, shift, gate, o_gdn, o_ret, o_nsa, o_fox, w_gate[l].astype(BF16),
                               branch_proj[l].astype(BF16), w_out[l].astype(BF16), ln_g[l, 0], ln_b[l, 0],
                               sc1p2, shift2, router_w, alpha)
        pos, wcol, tiles = _router(logits, router_b, MOE_TM)
        x = _moe(x, gate2, hp, pos, wcol, tiles, exp_w1[l].astype(BF16), exp_w3[l].astype(BF16),
                 exp_w2[l].astype(BF16), ln_g[l, 1], ln_b[l, 1], alpha, MOE_TM)
    return x
```

```python
import functools
import math

import jax
import jax.numpy as jnp
import numpy as np
from jax import lax
from jax.experimental import pallas as pl
from jax.experimental.pallas import tpu as pltpu

F32 = jnp.float32
BF16 = jnp.bfloat16

N_HEADS = 4
HEAD_DIM = 64
MIX_W = N_HEADS * HEAD_DIM
GDN_CHUNK = 64
CONV_K = 4
RET_CHUNK = 128
ROPE_BASE = 10000.0
NSA_CMP_LEN = 32
NSA_CMP_STRIDE = 16
NSA_SEL_LEN = 64
NSA_TOP_N = 16
NSA_WINDOW = 512
NSA_FORCE_SCORE = 1.0e4
N_EXPERTS = 16
N_GROUPS = 4
EXPERTS_PER_GROUP = N_EXPERTS // N_GROUPS
LN_EPS = 1e-5
LANES = 128
MXU_N = 256
SUM_ROWS = 16
NEG_BIG = -1.0e30
SEL_MASK_BIAS = NEG_BIG
LOG2E = math.log2(math.e)

IN_WIDTHS = (
    MIX_W, MIX_W, MIX_W, N_HEADS, N_HEADS, MIX_W,
    MIX_W, MIX_W, MIX_W, MIX_W,
    MIX_W, HEAD_DIM, HEAD_DIM, HEAD_DIM, HEAD_DIM, HEAD_DIM, HEAD_DIM, 3 * N_HEADS,
    MIX_W, MIX_W, MIX_W, N_HEADS,
)
_IN_OFF = np.concatenate([[0], np.cumsum(IN_WIDTHS)]).astype(int)
(_GQ, _GK, _GV, _GA, _GB, _GZ, _RQ, _RK, _RV, _RG, _NQ, _NKC, _NVC, _NKS, _NVS, _NKW, _NVW, _NGATE,
 _FQ, _FK, _FV, _FF) = range(22)
SM_A, SM_B, SM_GATE, SM_F = 0, 4, 8, 20

VMEM_LIMIT = 56 * 1024 * 1024


def _cp(sem, vmem=None):
    return pltpu.CompilerParams(dimension_semantics=sem, vmem_limit_bytes=vmem)


def _sigmoid(x):
    return 1.0 / (1.0 + jnp.exp(-x))


def _silu(x):
    return x * _sigmoid(x)


def _softplus(x):
    return jnp.maximum(x, 0.0) + jnp.log1p(jnp.exp(-jnp.abs(x)))


def _dot(a, b):
    return jnp.dot(a, b, preferred_element_type=F32)


def _dot_nt(a, b):
    return lax.dot_general(a, b, (((1,), (1,)), ((), ())), preferred_element_type=F32)


def _split3(a):
    a1 = a.astype(BF16)
    r = a - a1.astype(F32)
    a2 = r.astype(BF16)
    a3 = (r - a2.astype(F32)).astype(BF16)
    return a1, a2, a3


def _dot_x(a, m):
    a1, a2, a3 = _split3(a)
    return _dot(a1, m) + _dot(a2, m) + _dot(a3, m)


def _xdot(m, a):
    a1, a2, a3 = _split3(a)
    return _dot(m, a1) + _dot(m, a2) + _dot(m, a3)


def _dot_hp(a, b):
    a1, a2, _ = _split3(a)
    b1, b2, _ = _split3(b)
    return _dot(a1, b1) + (_dot(a1, b2) + _dot(a2, b1))


def _iota(shape, dim):
    return lax.broadcasted_iota(jnp.int32, shape, dim)


def _div(x, n):
    return lax.shift_right_logical(x, int(math.log2(n)))


def _mod(x, n):
    return x & (n - 1)


def _head_ones():
    return _div(_iota((MIX_W, MIX_W), 0), HEAD_DIM) == _div(_iota((MIX_W, MIX_W), 1), HEAD_DIM)


def _head_sum(x, ones):
    a1, a2, _ = _split3(x)
    return _dot(a1, ones) + _dot(a2, ones)


def _ada_body(c_ref, w_ref, b_ref, o_ref):
    mod = _dot_hp(_silu(c_ref[...]), w_ref[0]) + b_ref[0]
    o_ref[0] = jnp.where(pl.program_id(1) == 1, 1.0 + mod, mod)


def _ada_mod(c, ada_w, ada_b):
    depth = ada_w.shape[0]
    bsz, d = c.shape
    n = depth * 2
    w = ada_w.reshape(n, d, 3 * d)
    b = ada_b.reshape(n, 1, 3 * d)
    out = pl.pallas_call(
        _ada_body, grid=(n, 3),
        in_specs=[pl.BlockSpec((bsz, d), lambda i, j: (0, 0)),
                  pl.BlockSpec((1, d, d), lambda i, j: (i, 0, j)),
                  pl.BlockSpec((1, 1, d), lambda i, j: (i, 0, j))],
        out_specs=pl.BlockSpec((1, bsz, d), lambda i, j: (i, 0, j)),
        out_shape=jax.ShapeDtypeStruct((n, bsz, 3 * d), F32),
        compiler_params=_cp(("arbitrary", "arbitrary")), name="ada_mod",
    )(c, w, b)
    return out.reshape(depth, 2, bsz, 1, 3 * d)


IN_TS = 1024
_W_GROUPS = (768, 256, 1024, 256, 512, 128)
_W_OFF = np.concatenate([[0], np.cumsum(_W_GROUPS)]).astype(int)
IN_CAT_W = int(_W_OFF[-1])
_WT_GROUPS = (LANES, MIX_W, HEAD_DIM, HEAD_DIM, MIX_W)
_WT_OFF = np.concatenate([[0], np.cumsum(_WT_GROUPS)]).astype(int)
IN_CAT_T = int(_WT_OFF[-1])


def _cat_in_weights(w_in):
    def col(i):
        return w_in[:, _IN_OFF[i]:_IN_OFF[i + 1]]
    small = jnp.concatenate([col(_GA), col(_GB), col(_NGATE), col(_FF)], axis=1)
    small = jnp.pad(small, ((0, 0), (0, LANES - small.shape[1])))
    scale = HEAD_DIM ** -0.5
    cat = jnp.concatenate([
        col(_GQ), col(_GK), col(_GV), col(_GZ),
        col(_RQ), col(_RK) * scale, col(_RV), col(_RG),
        col(_NKC), col(_NVC), col(_NKS), col(_NKW),
        col(_FQ) * (scale * LOG2E), col(_FK),
        small], axis=1)
    cat_t = jnp.concatenate([small, col(_NQ) * (scale * LOG2E), col(_NVS), col(_NVW), col(_FV)], axis=1).T
    return cat.astype(BF16), cat_t.astype(BF16)


def _inproj_body(x_ref, sc_ref, sh_ref, w_ref, wt_ref,
                 gqkv_ref, gz_ref, r_ref, kc_ref, vc_ref, ks_ref, kw_ref, fq_ref, fk_ref, sm_ref,
                 smt_ref, nqt_ref, vst_ref, vwt_ref, fvt_ref):
    h = (x_ref[0] * sc_ref[0] + sh_ref[0]).astype(BF16)

    def proj(g):
        return _dot(h, w_ref[:, _W_OFF[g]:_W_OFF[g + 1]])

    gqkv_ref[0] = proj(0)
    gz_ref[0] = proj(1)
    r_ref[0] = proj(2)
    nkv = proj(3)
    for i, ref in enumerate((kc_ref, vc_ref, ks_ref, kw_ref)):
        ref[0] = nkv[:, i * HEAD_DIM:(i + 1) * HEAD_DIM].astype(ref.dtype)
    f = proj(4)
    for j, ref in enumerate((fq_ref, fk_ref)):
        for hd in range(N_HEADS):
            lo = j * MIX_W + hd * HEAD_DIM
            ref[0, hd] = f[:, lo:lo + HEAD_DIM].astype(BF16)
    sm_ref[0] = proj(5)
    for g, ref in enumerate((smt_ref, nqt_ref, vst_ref, vwt_ref, fvt_ref)):
        ref[0] = _dot_nt(wt_ref[_WT_OFF[g]:_WT_OFF[g + 1], :], h).astype(ref.dtype)


def _in_proj(x, sc1p, shift, wcat, wcat_t):
    bsz, s, d = x.shape
    ts = min(IN_TS, s)
    tok = lambda w, dt: jax.ShapeDtypeStruct((bsz, s, w), dt)
    chan = lambda w, dt: jax.ShapeDtypeStruct((bsz, w, s), dt)
    hm = lambda w: jax.ShapeDtypeStruct((bsz, N_HEADS, s, w), BF16)
    out_shape = [tok(768, F32), tok(256, F32), tok(1024, F32),
                 tok(64, F32), tok(64, F32), tok(64, BF16), tok(64, BF16), hm(HEAD_DIM), hm(HEAD_DIM), tok(LANES, F32),
                 chan(LANES, F32), chan(MIX_W, BF16), chan(HEAD_DIM, BF16), chan(HEAD_DIM, BF16), chan(MIX_W, BF16)]
    tspec = lambda w: pl.BlockSpec((1, ts, w), lambda b, i: (b, i, 0))
    cspec = lambda w: pl.BlockSpec((1, w, ts), lambda b, i: (b, 0, i))
    hspec = lambda w: pl.BlockSpec((1, N_HEADS, ts, w), lambda b, i: (b, 0, i, 0))
    out_specs = [tspec(768), tspec(256), tspec(1024),
                 tspec(64), tspec(64), tspec(64), tspec(64), hspec(HEAD_DIM), hspec(HEAD_DIM), tspec(LANES),
                 cspec(LANES), cspec(MIX_W), cspec(HEAD_DIM), cspec(HEAD_DIM), cspec(MIX_W)]
    vec = pl.BlockSpec((1, 1, d), lambda b, i: (b, 0, 0))
    return pl.pallas_call(
        _inproj_body, grid=(bsz, s // ts),
        in_specs=[tspec(d), vec, vec,
                  pl.BlockSpec((d, IN_CAT_W), lambda b, i: (0, 0)),
                  pl.BlockSpec((IN_CAT_T, d), lambda b, i: (0, 0))],
        out_specs=out_specs, out_shape=out_shape,
        compiler_params=_cp(("arbitrary", "arbitrary"), VMEM_LIMIT), name="in_proj",
    )(x, sc1p, shift, wcat, wcat_t)


CUM_T = 512


def _foxcum_body(sm_ref, k_ref, brow_ref, kaug_ref, carry_r):
    @pl.when(pl.program_id(1) == 0)
    def _():
        carry_r[...] = jnp.zeros_like(carry_r)

    t = sm_ref.shape[1]

    def log_sigmoid(v):
        return jnp.minimum(v, 0.0) - jnp.log1p(jnp.exp(-jnp.abs(v)))

    lower = (_iota((t, t), 0) >= _iota((t, t), 1)).astype(BF16)
    cc = _xdot(lower, log_sigmoid(sm_ref[0] + brow_ref[...])) + carry_r[...]
    carry_r[...] = cc[t - 1:t, :]
    parts = _split3(-cc * LOG2E)
    krow, kdst = _iota((HEAD_DIM, MXU_N), 0), _iota((HEAD_DIM, MXU_N), 1)
    src, dst = _iota((LANES, MXU_N), 0), _iota((LANES, MXU_N), 1)
    for ha in range(0, N_HEADS, 2):
        aug = (_dot(k_ref[0, ha], (kdst == krow).astype(BF16))
               + _dot(k_ref[0, ha + 1], (kdst == krow + LANES).astype(BF16)))
        for j, part in enumerate(parts):
            sel = (((src == SM_F + ha) & (dst == HEAD_DIM + j))
                   | ((src == SM_F + ha + 1) & (dst == LANES + HEAD_DIM + j)))
            aug = aug + _dot(part, sel.astype(BF16))
        kaug_ref[0, ha] = aug[:, :LANES].astype(BF16)
        kaug_ref[0, ha + 1] = aug[:, LANES:].astype(BF16)


def _fox_cum(small, k, f_bias):
    bsz, s, _ = small.shape
    t = min(CUM_T, s)
    brow = jnp.zeros((1, LANES), F32).at[0, SM_F:SM_F + N_HEADS].set(f_bias)
    return pl.pallas_call(
        _foxcum_body, grid=(bsz, s // t),
        in_specs=[pl.BlockSpec((1, t, LANES), lambda b, i: (b, i, 0)),
                  pl.BlockSpec((1, N_HEADS, t, HEAD_DIM), lambda b, i: (b, 0, i, 0)),
                  pl.BlockSpec((1, LANES), lambda b, i: (0, 0))],
        out_specs=pl.BlockSpec((1, N_HEADS, t, LANES), lambda b, i: (b, 0, i, 0)),
        out_shape=jax.ShapeDtypeStruct((bsz, N_HEADS, s, LANES), BF16),
        scratch_shapes=[pltpu.VMEM((1, LANES), F32)],
        compiler_params=_cp(("arbitrary", "arbitrary")), name="fox_cum",
    )(small, k, brow)


FOX_T = 512
FOX_UNROLL = 2


def _lockstep(gens):
    out, live = [None] * len(gens), list(range(len(gens)))
    while live:
        still = []
        for g in live:
            try:
                next(gens[g])
                still.append(g)
            except StopIteration as stop:
                out[g] = stop.value
        live = still
    return out


def _fox_body(q_ref, k_ref, vt_ref, o_ref):
    i = pl.program_id(1)
    t = q_ref.shape[2]
    place = (_iota((HEAD_DIM, LANES), 0) == _iota((HEAD_DIM, LANES), 1)).astype(BF16)
    lane = _iota((1, LANES), 1)
    ones = ((lane >= HEAD_DIM) & (lane < HEAD_DIM + 3)).astype(F32)
    causal = _iota((t, t), 0) <= _iota((t, t), 1)
    q_aug = [(_dot(q_ref[0, hd], place) + ones).astype(BF16) for hd in range(N_HEADS)]

    ones_rows = jnp.ones((SUM_ROWS, t), BF16)

    def head_steps(hd, chunks, carry, masked):
        m, acc = carry
        keys = [pl.ds(pl.multiple_of(j * t, t), t) for j in chunks]
        ahead = _dot_nt(k_ref[0, hd, keys[0], :], q_aug[hd])
        yield
        for u in range(len(chunks)):
            s = ahead
            if u + 1 < len(chunks):
                ahead = _dot_nt(k_ref[0, hd, keys[u + 1], :], q_aug[hd])
            if masked:
                s = jnp.where(causal, s, NEG_BIG)
            m_new = jnp.maximum(m, jnp.max(s, axis=0, keepdims=True))
            yield
            alpha = jnp.exp2(m - m_new)
            p = jnp.exp2(s - m_new)
            yield
            vt_aug = jnp.concatenate([vt_ref[0, hd * HEAD_DIM:(hd + 1) * HEAD_DIM, keys[u]], ones_rows], axis=0)
            m, acc = m_new, alpha * acc + _dot(vt_aug, p.astype(BF16))
            yield
        return m, acc

    def steps(chunks, carries, masked=False):
        return tuple(_lockstep([head_steps(hd, chunks, carries[hd], masked) for hd in range(N_HEADS)]))

    carries = tuple((jnp.full((1, t), NEG_BIG, F32), jnp.zeros((HEAD_DIM + SUM_ROWS, t), F32))
                    for _ in range(N_HEADS))
    n_groups = _div(i, FOX_UNROLL)
    carries = lax.fori_loop(0, n_groups, lambda g, c: steps([g * FOX_UNROLL + u for u in range(FOX_UNROLL)], c),
                            carries)
    carries = lax.fori_loop(n_groups * FOX_UNROLL, i, lambda j, c: steps([j], c), carries)
    carries = steps([i], carries, masked=True)
    o_ref[0] = jnp.concatenate([acc[:HEAD_DIM] / acc[HEAD_DIM:HEAD_DIM + 1] for (_, acc) in carries],
                               axis=0).T.astype(o_ref.dtype)


def _fox_attn(q, kaug, vt):
    bsz, nh, s, hd = q.shape
    t = min(FOX_T, s)
    return pl.pallas_call(
        _fox_body, grid=(bsz, s // t),
        in_specs=[pl.BlockSpec((1, nh, t, hd), lambda b, i: (b, 0, i, 0)),
                  pl.BlockSpec((1, nh, s, LANES), lambda b, i: (b, 0, 0, 0)),
                  pl.BlockSpec((1, MIX_W, s), lambda b, i: (b, 0, 0))],
        out_specs=pl.BlockSpec((1, t, MIX_W), lambda b, i: (b, i, 0)),
        out_shape=jax.ShapeDtypeStruct((bsz, s, MIX_W), BF16),
        compiler_params=_cp(("arbitrary", "arbitrary")), name="fox_attn",
    )(q, kaug, vt)


GDN_T = 256
GDN_NB = 4


def _gdn_body(qkv_ref, sm_ref, smt_ref, z_ref, cw_ref, alr_ref, dtr_ref, alc_ref, dtc_ref, nw_ref,
              o_ref, xbuf, state):
    t, c = GDN_T, GDN_CHUNK
    ri, ci = _iota((t, t), 0), _iota((t, t), 1)
    same = _div(ri, c) == _div(ci, c)
    er, ec = _iota((LANES, MIX_W), 0), _div(_iota((LANES, MIX_W), 1), HEAD_DIM)
    head_blk = _head_ones()
    masks = dict(same=same, incl=same & (ci <= ri), strict=same & (ci < ri),
                 same_b=same.astype(BF16), incl_b=(same & (ci <= ri)).astype(BF16),
                 incl_t_b=(same & (ri <= ci)).astype(BF16),
                 exp_a=(er == ec + SM_A).astype(BF16),
                 exp_b=(er == ec + SM_B).astype(BF16),
                 head_blk=head_blk, head_ones=head_blk.astype(BF16),
                 lane_head=_div(_iota((1, MIX_W), 1), HEAD_DIM))
    _lockstep([_gdn_tile(masks, qkv_ref.at[bb], sm_ref.at[bb], smt_ref.at[bb], z_ref.at[bb], cw_ref, alr_ref,
                         dtr_ref, alc_ref, dtc_ref, nw_ref, o_ref.at[bb], xbuf.at[bb], state.at[bb])
               for bb in range(qkv_ref.shape[0])])


def _gdn_tile(masks, qkv_ref, sm_ref, smt_ref, z_ref, cw_ref, alr_ref, dtr_ref, alc_ref, dtc_ref, nw_ref,
              o_ref, xbuf, state):
    t, c = GDN_T, GDN_CHUNK
    incl, strict, head_ones = masks["incl"], masks["strict"], masks["head_ones"]

    @pl.when(pl.program_id(1) == 0)
    def _():
        xbuf[0:8, :] = jnp.zeros((8, 3 * MIX_W), F32)
        state[...] = jnp.zeros_like(state)

    xbuf[8:8 + t, :] = qkv_ref[...]
    conv = cw_ref[0:1, :] * xbuf[pl.ds(8 - CONV_K + 1, t), :]
    for kk in range(1, CONV_K):
        conv = conv + cw_ref[kk:kk + 1, :] * xbuf[pl.ds(8 - CONV_K + 1 + kk, t), :]
    xbuf[0:8, :] = xbuf[t:t + 8, :]
    act = _silu(conv)
    q, k, v = act[:, :MIX_W], act[:, MIX_W:2 * MIX_W], act[:, 2 * MIX_W:]
    q = q * lax.rsqrt(_head_sum(q * q, head_ones) + 1e-6) * HEAD_DIM ** -0.5
    k = k * lax.rsqrt(_head_sum(k * k, head_ones) + 1e-6)

    sm, smt = sm_ref[...], smt_ref[...]
    la_col = -jnp.exp(alr_ref[...]) * _softplus(sm + dtr_ref[...])
    la_row = -jnp.exp(alc_ref[...]) * _softplus(smt + dtc_ref[...])
    beta_col = _sigmoid(sm)
    bc_col = _xdot(masks["incl_b"], la_col)
    last_col = _xdot(masks["same_b"], la_col)
    bc_row = _dot_x(la_row, masks["incl_t_b"])
    exp_a, exp_b = masks["exp_a"], masks["exp_b"]
    bc_x, last_x, beta_x = _dot_x(bc_col, exp_a), _dot_x(last_col, exp_a), _dot_x(beta_col, exp_b)
    eb = jnp.exp(bc_x)
    q_dec = q * eb
    k_dec = k * jnp.exp(last_x - bc_x)
    rhs_v = beta_x * v
    rhs_k = beta_x * eb * k
    rhs = jnp.concatenate([rhs_v, rhs_k], axis=1).astype(BF16)

    kb = k.astype(BF16)
    lane_head = masks["lane_head"]
    heads = range(N_HEADS)
    mh = [lane_head == hd for hd in heads]
    gram = [_dot_nt(jnp.where(mh[hd], k, 0.0).astype(BF16), kb) for hd in heads]
    qk = [_dot_nt(jnp.where(mh[hd], q, 0.0).astype(BF16), kb) for hd in heads]
    yield
    dec = [jnp.exp(jnp.minimum(bc_col[:, SM_A + hd:SM_A + hd + 1] - bc_row[SM_A + hd:SM_A + hd + 1, :], 0.0))
           for hd in heads]
    attn = [jnp.where(incl, qk[hd] * dec[hd], 0.0) for hd in heads]
    p = [jnp.where(strict, -(beta_col[:, SM_B + hd:SM_B + hd + 1] * gram[hd] * dec[hd]), 0.0) for hd in heads]
    tr = list(p)
    for _ in range(5):
        pb = [p[hd].astype(BF16) for hd in heads]
        p = [_dot(pb[hd], pb[hd]) for hd in heads]
        yield
        tr = [tr[hd] + p[hd] + _dot(tr[hd].astype(BF16), p[hd].astype(BF16)) for hd in heads]
    sol = [_dot(tr[hd].astype(BF16), rhs) for hd in heads]
    yield
    u0, w = rhs_v, rhs_k
    for hd in heads:
        u0 = u0 + jnp.where(mh[hd], sol[hd][:, :MIX_W], 0.0)
        w = w + jnp.where(mh[hd], sol[hd][:, MIX_W:], 0.0)

    head_blk = masks["head_blk"]
    k_dec_t = k_dec.T.astype(BF16)
    outs = []
    for n in range(t // c):
        rows = slice(n * c, (n + 1) * c)
        s_old = state[...]
        sb = s_old.astype(BF16)
        u_n = u0[rows] - _dot(w[rows].astype(BF16), sb)
        yield
        parts = ([jnp.zeros((n * c, MIX_W), F32)] if n else []) + [u_n]
        if t - (n + 1) * c:
            parts.append(jnp.zeros((t - (n + 1) * c, MIX_W), F32))
        u_full = jnp.concatenate(parts, axis=0).astype(BF16)
        oa = _dot(jnp.concatenate([a[rows] for a in attn], axis=0).astype(BF16), u_full)
        o_n = _dot(q_dec[rows].astype(BF16), sb)
        for hd in range(N_HEADS):
            o_n = o_n + jnp.where(lane_head == hd, oa[hd * c:(hd + 1) * c], 0.0)
        outs.append(o_n)
        g_row = jnp.exp(last_x[n * c:n * c + 1, :])
        state[...] = g_row * s_old + jnp.where(head_blk, _dot(k_dec_t, u_full), 0.0)
        yield
    o = jnp.concatenate(outs, axis=0)
    o = o * lax.rsqrt(_head_sum(o * o, head_ones) * (1.0 / HEAD_DIM) + 1e-6) * nw_ref[...]
    o_ref[...] = (o * _silu(z_ref[...])).astype(o_ref.dtype)


def _lane_vec(vals, off):
    return jnp.zeros((1, LANES), F32).at[0, off:off + vals.shape[0]].set(vals)


def _gdn(qkv, small, small_t, z, conv_w, a_log, dt_bias, norm_w):
    bsz, s, _ = qkv.shape
    t = GDN_T
    alr, dtr = _lane_vec(a_log, SM_A), _lane_vec(dt_bias, SM_A)
    nw = jnp.tile(norm_w, N_HEADS).reshape(1, MIX_W)
    const = lambda shape: pl.BlockSpec(shape, lambda b, i: (0,) * len(shape))
    nb = GDN_NB if bsz % GDN_NB == 0 else 1
    return pl.pallas_call(
        _gdn_body, grid=(bsz // nb, s // t),
        in_specs=[pl.BlockSpec((nb, t, 3 * MIX_W), lambda b, i: (b, i, 0)),
                  pl.BlockSpec((nb, t, LANES), lambda b, i: (b, i, 0)),
                  pl.BlockSpec((nb, LANES, t), lambda b, i: (b, 0, i)),
                  pl.BlockSpec((nb, t, MIX_W), lambda b, i: (b, i, 0)),
                  const((CONV_K, 3 * MIX_W)), const((1, LANES)), const((1, LANES)),
                  const((LANES, 1)), const((LANES, 1)), const((1, MIX_W))],
        out_specs=pl.BlockSpec((nb, t, MIX_W), lambda b, i: (b, i, 0)),
        out_shape=jax.ShapeDtypeStruct((bsz, s, MIX_W), BF16),
        scratch_shapes=[pltpu.VMEM((nb, t + 8, 3 * MIX_W), F32), pltpu.VMEM((nb, MIX_W, MIX_W), F32)],
        compiler_params=_cp(("arbitrary", "arbitrary")), name="gdn",
    )(qkv, small, small_t, z, conv_w, alr, dtr, alr.reshape(LANES, 1), dtr.reshape(LANES, 1), nw)


RET_T = 1024
_RET_LOG_GAMMA = tuple(math.log1p(-(2.0 ** (-5.0 - h))) for h in range(N_HEADS))


def _per_head(lane_head, vals):
    out = jnp.full(lane_head.shape, vals[0], F32)
    for hd in range(1, N_HEADS):
        out = jnp.where(lane_head == hd, vals[hd], out)
    return out


def _ret_body(r_ref, pos_ref, gnw_ref, o_ref, state):
    t, c = r_ref.shape[1], RET_CHUNK

    @pl.when(pl.program_id(1) == 0)
    def _():
        state[...] = jnp.zeros_like(state)

    x = r_ref[0]
    lane = _iota((1, MIX_W), 1)
    lane_head = _div(lane, HEAD_DIM)
    half = HEAD_DIM // 2
    inv_freq = jnp.exp(_mod(_iota((1, LANES), 1), half).astype(F32) * (-math.log(ROPE_BASE) / half))
    ang = pos_ref[0].astype(F32) * inv_freq
    cos, sin = jnp.cos(ang), jnp.sin(ang)
    cos, sin = jnp.concatenate([cos, cos], axis=1), jnp.concatenate([sin, sin], axis=1)
    first = _mod(lane, HEAD_DIM) < half

    def rope(a):
        rot = jnp.where(first, -pltpu.roll(a, MIX_W - half, 1), pltpu.roll(a, half, 1))
        return a * cos + rot * sin

    q, k = rope(x[:, :MIX_W]), rope(x[:, MIX_W:2 * MIX_W])
    v, g = x[:, 2 * MIX_W:3 * MIX_W], x[:, 3 * MIX_W:]

    lg = _per_head(lane_head, _RET_LOG_GAMMA)
    cidx = _iota((c, 1), 0).astype(F32)
    xi = jnp.exp((cidx + 1.0) * lg)
    zeta = jnp.exp((c - 1.0 - cidx) * lg)
    chunk_decay = jnp.exp(float(c) * lg)
    rel = (_iota((c, c), 0) - _iota((c, c), 1)).astype(F32)
    dec = jnp.concatenate([jnp.where(rel >= 0, jnp.exp(jnp.maximum(rel, 0.0) * _RET_LOG_GAMMA[hd]), 0.0)
                           for hd in range(N_HEADS)], axis=0)
    head_blk = _head_ones()
    outs = []
    for n in range(t // c):
        rows = slice(n * c, (n + 1) * c)
        qn, kn, vn = q[rows], k[rows], v[rows].astype(BF16)
        qs = jnp.concatenate([jnp.where(lane_head == hd, qn, 0.0) for hd in range(N_HEADS)], axis=0).astype(BF16)
        sc = _dot_nt(qs, kn.astype(BF16)) * dec
        res = _dot(sc.astype(BF16), vn)
        s_old = state[...]
        o_n = _dot(qn.astype(BF16), s_old.astype(BF16)) * xi
        for hd in range(N_HEADS):
            o_n = o_n + jnp.where(lane_head == hd, res[hd * c:(hd + 1) * c], 0.0)
        outs.append(o_n)
        kv = _dot((kn * zeta).T.astype(BF16), vn)
        state[...] = chunk_decay * s_old + jnp.where(head_blk, kv, 0.0)
    o = jnp.concatenate(outs, axis=0)
    head_ones = head_blk.astype(BF16)
    mu = _head_sum(o, head_ones) * (1.0 / HEAD_DIM)
    xc = o - mu
    var = _head_sum(xc * xc, head_ones) * (1.0 / HEAD_DIM)
    y = xc * lax.rsqrt(var + LN_EPS) * gnw_ref[...]
    o_ref[0] = (_silu(g) * y).astype(o_ref.dtype)


def _retention(r, positions, gn_w):
    bsz, s, _ = r.shape
    t = min(RET_T, s)
    return pl.pallas_call(
        _ret_body, grid=(bsz, s // t),
        in_specs=[pl.BlockSpec((1, t, 4 * MIX_W), lambda b, i: (b, i, 0)),
                  pl.BlockSpec((1, t, 1), lambda b, i: (b, i, 0)),
                  pl.BlockSpec((1, MIX_W), lambda b, i: (0, 0))],
        out_specs=pl.BlockSpec((1, t, MIX_W), lambda b, i: (b, i, 0)),
        out_shape=jax.ShapeDtypeStruct((bsz, s, MIX_W), BF16),
        scratch_shapes=[pltpu.VMEM((MIX_W, MIX_W), F32)],
        compiler_params=_cp(("arbitrary", "arbitrary")), name="retention",
    )(r, positions.reshape(bsz, s, 1), gn_w.reshape(1, MIX_W))


def _nsacmp_body(kc_ref, vc_ref, pe_ref, kw1_ref, kw2_ref, vw1_ref, vw2_ref, ko_ref, vo_ref):
    half = NSA_CMP_STRIDE * HEAD_DIM
    nc = kc_ref.shape[1]
    def hidden(x_ref, w1_ref):
        x = x_ref[0]
        top = _dot_hp(x, w1_ref[0:half, :])
        bot = _dot_hp(x, w1_ref[half:2 * half, :])
        bias = _dot_hp(pe_ref[...], w1_ref[...])[0:1, :]
        return _silu(top + pltpu.roll(bot, nc - 1, 0) + bias)

    ko_ref[0] = _dot_hp(hidden(kc_ref, kw1_ref), kw2_ref[...])
    a1, a2, _ = _split3(vw2_ref[...])
    b1, b2, _ = _split3(hidden(vc_ref, vw1_ref))
    vo_ref[0] = _dot_nt(a1, b1) + (_dot_nt(a1, b2) + _dot_nt(a2, b1))


def _nsa_compress(kc, vc, pe, ck_w1, ck_w2, cv_w1, cv_w2):
    bsz, s, hd = kc.shape
    nc = s // NSA_CMP_STRIDE
    wide = NSA_CMP_STRIDE * hd
    pe8 = jnp.broadcast_to(pe.reshape(1, NSA_CMP_LEN * hd), (8, NSA_CMP_LEN * hd))
    cv_w2t = cv_w2.T
    const = lambda a: pl.BlockSpec(a.shape, lambda b: (0,) * a.ndim)
    xspec = pl.BlockSpec((1, nc, wide), lambda b: (b, 0, 0))
    return pl.pallas_call(
        _nsacmp_body, grid=(bsz,),
        in_specs=[xspec, xspec, const(pe8), const(ck_w1), const(ck_w2), const(cv_w1), const(cv_w2t)],
        out_specs=[pl.BlockSpec((1, nc, hd), lambda b: (b, 0, 0)), pl.BlockSpec((1, hd, nc), lambda b: (b, 0, 0))],
        out_shape=[jax.ShapeDtypeStruct((bsz, nc, hd), F32), jax.ShapeDtypeStruct((bsz, hd, nc), F32)],
        compiler_params=_cp(("arbitrary",)), name="nsa_compress",
    )(kc.reshape(bsz, nc, wide), vc.reshape(bsz, nc, wide), pe8, ck_w1, ck_w2, cv_w1, cv_w2t)


NSA_TQ = 256
NSA_TK = 512
NSA_UNROLL = 2


def _softmax_weights(s, valid):
    s = jnp.where(valid, s, NEG_BIG)
    m = jnp.max(s, axis=0, keepdims=True)
    m = jnp.where(m > 0.5 * NEG_BIG, m, 0.0)
    return jnp.exp2(s - m)


def _with_ones(vt):
    return jnp.concatenate([vt, jnp.ones((SUM_ROWS, vt.shape[1]), BF16)], axis=0)


def _nsa_body(qt_ref, kc_ref, vct_ref, ks_ref, vst_ref, kw_ref, vwt_ref, smt_ref, o_ref, kaug):
    i = pl.program_id(1)
    tq, hd = qt_ref.shape[2], HEAD_DIM
    s, nc = ks_ref.shape[1], kc_ref.shape[1]
    nsel = s // NSA_SEL_LEN
    n_top = min(NSA_TOP_N, nsel)
    cols = N_HEADS * tq

    @pl.when(i == 0)
    def _():
        place = (_iota((hd, LANES), 0) == _iota((hd, LANES), 1)).astype(BF16)
        onehot = (_iota((s, LANES), 1) - hd) == _div(_iota((s, LANES), 0), NSA_SEL_LEN)
        kaug[...] = (_dot(ks_ref[0], place) + onehot.astype(F32)).astype(BF16)

    t0 = i * tq
    qt = qt_ref[0]
    q4 = jnp.concatenate([qt[h * hd:(h + 1) * hd, :] for h in range(N_HEADS)], axis=1)
    tpos = t0 + _iota((1, tq), 1)
    tpos4 = t0 + _mod(_iota((1, cols), 1), tq)

    def cmp_branch():
        sc = _dot(kc_ref[0].astype(BF16), q4)
        yield
        cmp_end = _iota((nc, 1), 0) * NSA_CMP_STRIDE + (NSA_CMP_LEN - 1)
        e = _softmax_weights(sc, cmp_end <= tpos4)
        den = jnp.sum(e, axis=0, keepdims=True)
        p = e * (1.0 / jnp.where(den > 0.0, den, 1.0))
        yield
        return p, _dot(vct_ref[0].astype(BF16), p.astype(BF16))

    def win_branch():
        wlen = NSA_WINDOW + tq
        start = pl.multiple_of(jnp.maximum(t0 - NSA_WINDOW, 0), tq)
        sc = _dot(kw_ref[0, pl.ds(start, wlen), :], q4)
        yield
        kpos = start + _iota((wlen, 1), 0)
        e = _softmax_weights(sc, (kpos <= tpos4) & (kpos > tpos4 - NSA_WINDOW))
        yield
        r = _dot(_with_ones(vwt_ref[0, :, pl.ds(start, wlen)]), e.astype(BF16))
        return r[:hd] / r[hd:hd + 1]

    (p_cmp, o_cmp), o_win = _lockstep([cmp_branch(), win_branch()])

    psum = p_cmp[:, 0:tq] + p_cmp[:, tq:2 * tq] + p_cmp[:, 2 * tq:3 * tq] + p_cmp[:, 3 * tq:4 * tq]
    ss, cs = _iota((hd, nc), 0) * NSA_SEL_LEN, _iota((hd, nc), 1) * NSA_CMP_STRIDE
    overlap = jnp.clip(jnp.minimum(cs + NSA_CMP_LEN, ss + NSA_SEL_LEN) - jnp.maximum(cs, ss), 0, NSA_CMP_LEN)
    importance = _xdot((overlap.astype(F32) * (1.0 / NSA_CMP_LEN)).astype(BF16), psum)
    blk = _iota((hd, 1), 0)
    cur = _div(tpos, NSA_SEL_LEN)
    forced = (blk == 0) | (blk == cur) | (blk == cur - 1)
    score = jnp.where(forced, NSA_FORCE_SCORE, jnp.where(blk * NSA_SEL_LEN <= tpos, importance, -1.0))
    score = jnp.where(blk < nsel, score, -2.0)
    groups = [score[8 * g:8 * g + 8, :] for g in range(hd // 8)]
    gblk = _iota((8, 1), 0)
    cnts = [jnp.zeros((8, tq), jnp.int32) for _ in groups]
    for j in range(nsel):
        row = score[j:j + 1, :]
        for g in range(hd // 8):
            if 8 * g > j:
                ahead = row >= groups[g]
            elif 8 * g + 7 <= j:
                ahead = row > groups[g]
            else:
                ahead = (row > groups[g]) | ((row == groups[g]) & (gblk + 8 * g > j))
            cnts[g] = cnts[g] + ahead.astype(jnp.int32)
    bias = jnp.where(jnp.concatenate(cnts, axis=0) < n_top, 0.0, SEL_MASK_BIAS).astype(BF16)
    q_aug = jnp.concatenate([q4, jnp.concatenate([bias] * N_HEADS, axis=1)], axis=0)

    def sel_keys(c):
        return pl.ds(pl.multiple_of(c * NSA_TK, NSA_TK), NSA_TK)

    def sel_scores(c):
        return _dot(kaug[sel_keys(c), :], q_aug)

    def sel_absorb(c, sc, carry):
        m, acc = carry
        m_new = jnp.maximum(m, jnp.max(sc, axis=0, keepdims=True))
        alpha = jnp.exp2(m - m_new)
        p = jnp.exp2(sc - m_new)
        return m_new, alpha * acc + _dot(_with_ones(vst_ref[0, :, sel_keys(c)]), p.astype(BF16))

    def sel_step(c, carry, masked=False):
        sc = sel_scores(c)
        if masked:
            sc = jnp.where(c * NSA_TK + _iota((NSA_TK, 1), 0) <= tpos4, sc, NEG_BIG)
        return sel_absorb(c, sc, carry)

    def sel_group(g, carry):
        chunks = [g * NSA_UNROLL + u for u in range(NSA_UNROLL)]
        ahead = sel_scores(chunks[0])
        for u, c in enumerate(chunks):
            sc, ahead = ahead, (sel_scores(chunks[u + 1]) if u + 1 < NSA_UNROLL else None)
            carry = sel_absorb(c, sc, carry)
        return carry

    n_full = _div(t0, NSA_TK)
    n_groups = _div(n_full, NSA_UNROLL)
    carry = (jnp.full((1, cols), NEG_BIG, F32), jnp.zeros((hd + SUM_ROWS, cols), F32))
    carry = lax.fori_loop(0, n_groups, sel_group, carry)
    carry = lax.fori_loop(n_groups * NSA_UNROLL, n_full, sel_step, carry)
    _, acc_sel = sel_step(n_full, carry, masked=True)
    o_sel = acc_sel[:hd] / acc_sel[hd:hd + 1]

    gates =_sigmoid(smt_ref[0][SM_GATE:SM_GATE + 16, :])
    outs = []
    for h in range(N_HEADS):
        c = slice(h * tq, (h + 1) * tq)
        outs.append(gates[3 * h:3 * h + 1, :] * o_cmp[:, c] + gates[3 * h + 1:3 * h + 2, :] * o_sel[:, c]
                    + gates[3 * h + 2:3 * h + 3, :] * o_win[:, c])
    o_ref[0] = jnp.concatenate(outs, axis=0).T.astype(o_ref.dtype)


def _nsa_attn(qt, k_cmp, v_cmp_t, ks, vst, kw, vwt, small_t):
    bsz, _, s = qt.shape
    tq, hd = NSA_TQ, HEAD_DIM
    assert s % NSA_TK == 0 and NSA_TK % tq == 0 and s >= NSA_WINDOW + tq and s // NSA_SEL_LEN <= hd
    nc = k_cmp.shape[1]
    keys = pl.BlockSpec((1, s, hd), lambda b, i: (b, 0, 0))
    vals = pl.BlockSpec((1, hd, s), lambda b, i: (b, 0, 0))
    return pl.pallas_call(
        _nsa_body, grid=(bsz, s // tq),
        in_specs=[pl.BlockSpec((1, MIX_W, tq), lambda b, i: (b, 0, i)),
                  pl.BlockSpec((1, nc, hd), lambda b, i: (b, 0, 0)),
                  pl.BlockSpec((1, hd, nc), lambda b, i: (b, 0, 0)),
                  keys, vals, keys, vals,
                  pl.BlockSpec((1, LANES, tq), lambda b, i: (b, 0, i))],
        out_specs=pl.BlockSpec((1, tq, MIX_W), lambda b, i: (b, i, 0)),
        out_shape=jax.ShapeDtypeStruct((bsz, s, MIX_W), BF16),
        scratch_shapes=[pltpu.VMEM((s, LANES), BF16)],
        compiler_params=_cp(("arbitrary", "arbitrary")), name="nsa_attn",
    )(qt, k_cmp, v_cmp_t, ks, vst, kw, vwt, small_t)


MERGE_T = 1024


def _layer_norm(r, g, b):
    mu = jnp.mean(r, axis=-1, keepdims=True)
    xc = r - mu
    var = jnp.mean(xc * xc, axis=-1, keepdims=True)
    return xc * lax.rsqrt(var + LN_EPS) * g + b


def _merge_body(alpha, x_ref, sc_ref, sh_ref, gt_ref, og_ref, or_ref, on_ref, of_ref, wg_ref, bp_ref, wo_ref,
                lng_ref, lnb_ref, sc2_ref, sh2_ref, rwt_ref, o_ref, hp_ref, lg_ref):
    x = x_ref[0]
    h = (x * sc_ref[0] + sh_ref[0]).astype(BF16)

    merged = None
    for br, o_br in enumerate((og_ref, or_ref, on_ref, of_ref)):
        term = _sigmoid(_dot(h, wg_ref[br])) * _dot(o_br[0], bp_ref[br])
        merged = term if merged is None else merged + term
    y = _dot(merged.astype(BF16), wo_ref[...])
    x_new = _layer_norm(alpha * x + gt_ref[0] * y, lng_ref[...], lnb_ref[...])
    o_ref[0] = x_new
    h2 = x_new * sc2_ref[0] + sh2_ref[0]
    half = h2.shape[1] // 2
    hp_ref[0] = _pack_bf16_pair(h2[:, :half], h2[:, half:])
    a1, a2, _ = _split3(rwt_ref[...])
    b1, b2, _ = _split3(h2)
    lg_ref[0] = _dot_nt(a1, b1) + (_dot_nt(a1, b2) + _dot_nt(a2, b1))


def _merge(x, sc1p, shift, gate, o_gdn, o_ret, o_nsa, o_fox, w_gate, branch_proj, w_out, ln_g, ln_b,
           sc1p_moe, shift_moe, router_w, alpha):
    bsz, s, d = x.shape
    t = min(MERGE_T, s)
    tok = lambda w: pl.BlockSpec((1, t, w), lambda b, i: (b, i, 0))
    vec = pl.BlockSpec((1, 1, d), lambda b, i: (b, 0, 0))
    const = lambda a: pl.BlockSpec(a.shape, lambda b, i: (0,) * a.ndim, pipeline_mode=pl.Buffered(1))
    lng, lnb, rwt = ln_g.reshape(1, d), ln_b.reshape(1, d), router_w.T
    return pl.pallas_call(
        functools.partial(_merge_body, alpha), grid=(bsz, s // t),
        in_specs=[tok(d), vec, vec, vec, tok(MIX_W), tok(MIX_W), tok(MIX_W), tok(MIX_W),
                  const(w_gate), const(branch_proj), const(w_out), const(lng), const(lnb), vec, vec, const(rwt)],
        out_specs=[tok(d), tok(d // 2), pl.BlockSpec((1, N_EXPERTS, t), lambda b, i: (b, 0, i))],
        out_shape=[jax.ShapeDtypeStruct((bsz, s, d), F32), jax.ShapeDtypeStruct((bsz, s, d // 2), jnp.uint32),
                   jax.ShapeDtypeStruct((bsz, N_EXPERTS, s), F32)],
        compiler_params=_cp(("arbitrary", "arbitrary"), VMEM_LIMIT), name="merge",
    )(x, sc1p, shift, gate, o_gdn, o_ret, o_nsa, o_fox, w_gate, branch_proj, w_out, lng, lnb,
      sc1p_moe, shift_moe, rwt)


MOE_TM = 256


def _pack_bf16_pair(a, b):
    hi = lax.bitcast_convert_type(a.astype(BF16).astype(F32), jnp.uint32)
    lo = lax.bitcast_convert_type(b.astype(BF16).astype(F32), jnp.uint32)
    return hi | lax.shift_right_logical(lo, jnp.uint32(16))


def _unpack_bf16_pair(w):
    a = lax.bitcast_convert_type(w & jnp.uint32(0xFFFF0000), F32)
    b = lax.bitcast_convert_type(lax.shift_left(w, jnp.uint32(16)), F32)
    return a, b


def _router_body(tm, lg_ref, rb_ref, pos_ref, wcol_ref, tile_ref):
    s = lg_ref.shape[2]
    scores = _sigmoid(lg_ref[0])
    biased = scores + rb_ref[...]
    b = [biased[e:e + 1, :] for e in range(N_EXPERTS)]
    sc = [scores[e:e + 1, :] for e in range(N_EXPERTS)]
    gs = []
    for g in range(N_GROUPS):
        m = [b[EXPERTS_PER_GROUP * g + j] for j in range(EXPERTS_PER_GROUP)]
        best = m[0] + m[1]
        for u in range(EXPERTS_PER_GROUP):
            for v in range(u + 1, EXPERTS_PER_GROUP):
                if (u, v) != (0, 1):
                    best = jnp.maximum(best, m[u] + m[v])
        gs.append(best)
    gsel, best = jnp.zeros((1, s), jnp.int32), gs[0]
    for g in range(1, N_GROUPS):
        take = gs[g] > best
        gsel = jnp.where(take, g, gsel)
        best = jnp.where(take, gs[g], best)
    first, second = [], []
    for e in range(N_EXPERTS):
        g = e // EXPERTS_PER_GROUP
        cnt = jnp.zeros((1, s), jnp.int32)
        for j in range(EXPERTS_PER_GROUP * g, EXPERTS_PER_GROUP * (g + 1)):
            if j != e:
                ahead = (b[j] >= b[e]) if j < e else (b[j] > b[e])
                cnt = cnt + ahead.astype(jnp.int32)
        first.append((gsel == g) & (cnt == 0))
        second.append((gsel == g) & (cnt == 1))
    s0 = sum(jnp.where(first[e], sc[e], 0.0) for e in range(N_EXPERTS))
    s1 = sum(jnp.where(second[e], sc[e], 0.0) for e in range(N_EXPERTS))
    den = s0 + s1
    w0, w1 = s0 / den, s1 / den
    blk = min(512, s)
    eye = (_iota((blk, blk), 0) == _iota((blk, blk), 1)).astype(BF16)
    rowid = _iota((LANES, blk), 0)
    for j in range(s // blk):
        cols = slice(j * blk, (j + 1) * blk)
        wb = jnp.where(rowid == 0, w0[:, cols], jnp.where(rowid == 1, w1[:, cols], 0.0))
        wcol_ref[0, cols, :] = sum(_dot_nt(eye, part) for part in _split3(wb))
    onehot = jnp.concatenate([(first[e] | second[e]).astype(F32) for e in range(N_EXPERTS)], axis=0)
    before = (_iota((blk, blk), 0) < _iota((blk, blk), 1)).astype(BF16)
    carry, ranks = jnp.zeros((N_EXPERTS, 1), F32), []
    for j in range(s // blk):
        ob = onehot[:, j * blk:(j + 1) * blk]
        ranks.append(_dot(ob.astype(BF16), before) + carry)
        carry = carry + jnp.sum(ob, axis=1, keepdims=True)
    rank = jnp.concatenate(ranks, axis=1)
    ntile = jnp.floor((carry + (tm - 1.0)) * (1.0 / tm))
    lower = (_iota((N_EXPERTS, N_EXPERTS), 1) < _iota((N_EXPERTS, N_EXPERTS), 0)).astype(BF16)
    toff = _dot(lower, jnp.broadcast_to(ntile, (N_EXPERTS, LANES)).astype(BF16))[:, 0:1]
    slot = toff * float(tm) + rank
    pos0 = sum(jnp.where(first[e], slot[e:e + 1, :], 0.0) for e in range(N_EXPERTS))
    pos1 = sum(jnp.where(second[e], slot[e:e + 1, :], 0.0) for e in range(N_EXPERTS))
    pos_ref[0] = jnp.concatenate([pos0, pos1], axis=1).astype(jnp.int32)
    tend = toff + ntile
    tid = _iota((1, LANES), 1).astype(F32)
    texp = jnp.sum((tend <= tid).astype(F32), axis=0, keepdims=True)
    ntot = jnp.broadcast_to(jnp.sum(ntile, axis=0, keepdims=True), (1, LANES))
    diag = _iota((N_EXPERTS, LANES), 0) == _iota((N_EXPERTS, LANES), 1)
    to_lanes = lambda col: jnp.sum(jnp.where(diag, col, 0.0), axis=0, keepdims=True)
    tile_ref[0] = jnp.concatenate([jnp.minimum(texp, N_EXPERTS - 1.0), ntot, to_lanes(carry), to_lanes(toff)],
                                  axis=0).astype(jnp.int32)


def _router(logits, router_b, tm):
    bsz, _, s = logits.shape
    return pl.pallas_call(
        functools.partial(_router_body, tm), grid=(bsz,),
        in_specs=[pl.BlockSpec((1, N_EXPERTS, s), lambda b: (b, 0, 0)),
                  pl.BlockSpec((N_EXPERTS, 1), lambda b: (0, 0))],
        out_specs=[pl.BlockSpec((1, 1, 2 * s), lambda b: (b, 0, 0)),
                   pl.BlockSpec((1, s, LANES), lambda b: (b, 0, 0)),
                   pl.BlockSpec((1, 4, LANES), lambda b: (b, 0, 0))],
        out_shape=[jax.ShapeDtypeStruct((bsz, 1, 2 * s), jnp.int32),
                   jax.ShapeDtypeStruct((bsz, s, LANES), F32),
                   jax.ShapeDtypeStruct((bsz, 4, LANES), jnp.int32)],
        compiler_params=_cp(("arbitrary",)), name="router",
    )(logits, router_b.reshape(N_EXPERTS, 1))


LN_ROWS = 512


def _moe_body(alpha, tm, nt, tiles_ref, hp_ref, x_ref, wcol_ref, gt_ref, pos_ref, posn_ref,
              w1_ref, w3_ref, w2_ref, lng_ref, lnb_ref, o_ref, src, xs, ysall, st0, st1):
    b, i = pl.program_id(0), pl.program_id(1)
    s = hp_ref.shape[1]
    tb = b * (4 * LANES)
    span = src.shape[0] // 2
    mine, nxt = lax.rem(b, 2) * span, lax.rem(b + 1, 2) * span

    def gather(buf, base, lo, hi):
        for r in range(lo, hi):
            xs[buf, pl.ds(r, 1), :] = hp_ref[0, pl.ds(src[mine + base + r], 1), :]

    @pl.when(i == 0)
    def _():
        @pl.when(b == 0)
        def _():
            def clear(p, c):
                src[p] = 0
                return c
            lax.fori_loop(0, src.shape[0], clear, 0, unroll=8)

            def fill(t, c):
                for k in range(2):
                    src[mine + pos_ref[0, 0, k * s + t]] = t
                return c
            lax.fori_loop(0, s, fill, 0, unroll=8)
        gather(0, 0, 0, tm)

    @pl.when(i < tiles_ref[tb + LANES])
    def _():
        base = pl.multiple_of(i * tm, tm)
        cur = lax.rem(i, 2)
        f, d = w1_ref.shape[2], w2_ref.shape[2]
        n_piece = 2 * (f // MXU_N) + d // MXU_N
        rows = tm // n_piece
        piece = iter(range(n_piece))

        fill_rows = tm // (2 * n_piece)

        def gather_ahead():
            p = next(piece)
            gather(1 - cur, base + tm, p * rows, (p + 1) * rows)
            for r in range(fill_rows):
                t = jnp.minimum(i * (tm // 2) + p * fill_rows + r, s - 1)
                for k in range(2):
                    src[nxt + posn_ref[0, 0, k * s + t]] = t

        xa, xb = _unpack_bf16_pair(xs[cur])
        x_t = jnp.concatenate([xa, xb], axis=1).astype(BF16)
        up = []
        for w_ref in (w1_ref, w3_ref):
            cols = []
            for j in range(f // MXU_N):
                cols.append(_dot(x_t, w_ref[0, :, j * MXU_N:(j + 1) * MXU_N]))
                gather_ahead()
            up.append(jnp.concatenate(cols, axis=1))
        hid = (_silu(up[0]) * up[1]).astype(BF16)
        ys = []
        for j in range(d // MXU_N):
            ys.append(_dot(hid, w2_ref[0, :, j * MXU_N:(j + 1) * MXU_N]))
            gather_ahead()
        half = d // (2 * MXU_N)
        ysall[pl.ds(base, tm), :] = _pack_bf16_pair(jnp.concatenate(ys[:half], axis=1),
                                                    jnp.concatenate(ys[half:], axis=1))

    @pl.when(i >= nt)
    def _():
        t0 = (i - nt) * LN_ROWS

        def fetch(r, c):
            st0[pl.ds(r, 1), :] = ysall[pl.ds(pos_ref[0, 0, t0 + r], 1), :]
            st1[pl.ds(r, 1), :] = ysall[pl.ds(pos_ref[0, 0, s + t0 + r], 1), :]
            return c
        lax.fori_loop(0, LN_ROWS, fetch, 0, unroll=True)
        a0, b0 = _unpack_bf16_pair(st0[...])
        a1, b1 = _unpack_bf16_pair(st1[...])
        w0, w1 = wcol_ref[0][:, 0:1], wcol_ref[0][:, 1:2]
        y = jnp.concatenate([w0 * a0 + w1 * a1, w0 * b0 + w1 * b1], axis=1)
        o_ref[0] = _layer_norm(alpha * x_ref[0] + gt_ref[0] * y, lng_ref[...], lnb_ref[...])


def _moe(x, gate, hp, pos, wcol, tiles, w1, w3, w2, ln_g, ln_b, alpha, tm):
    bsz, s, d = x.shape
    f = w1.shape[2]
    nt = 2 * s // tm + N_EXPERTS
    assert nt <= LANES and tm & (tm - 1) == 0 and s % LN_ROWS == 0
    blk = lambda i: jnp.maximum(i - nt, 0)
    vec = pl.BlockSpec((1, 1, d), lambda b, i, tl: (b, 0, 0))
    tok = lambda w: pl.BlockSpec((1, LN_ROWS, w), lambda b, i, tl: (b, blk(i), 0))
    row = pl.BlockSpec((1, d), lambda b, i, tl: (0, 0))
    expert = lambda shape: pl.BlockSpec(shape, lambda b, i, tl: (tl[b * (4 * LANES) + jnp.minimum(i, nt - 1)], 0, 0))
    grid_spec = pltpu.PrefetchScalarGridSpec(
        num_scalar_prefetch=1, grid=(bsz, nt + s // LN_ROWS),
        in_specs=[pl.BlockSpec((1, s, d // 2), lambda b, i, tl: (b, 0, 0), pipeline_mode=pl.Buffered(1)),
                  tok(d), tok(LANES), vec,
                  pl.BlockSpec((1, 1, 2 * s), lambda b, i, tl: (b, 0, 0), memory_space=pltpu.SMEM),
                  pl.BlockSpec((1, 1, 2 * s), lambda b, i, tl: (jnp.minimum(b + 1, bsz - 1), 0, 0),
                               memory_space=pltpu.SMEM),
                  expert((1, d, f)), expert((1, d, f)), expert((1, f, d)), row, row],
        out_specs=tok(d),
        scratch_shapes=[pltpu.SMEM((2 * (nt + 1) * tm,), jnp.int32), pltpu.VMEM((2, tm, d // 2), jnp.uint32),
                        pltpu.VMEM((nt * tm, d // 2), jnp.uint32),
                        pltpu.VMEM((LN_ROWS, d // 2), jnp.uint32), pltpu.VMEM((LN_ROWS, d // 2), jnp.uint32)])
    return pl.pallas_call(
        functools.partial(_moe_body, alpha, tm, nt), grid_spec=grid_spec,
        out_shape=jax.ShapeDtypeStruct((bsz, s, d), F32),
        compiler_params=_cp(("arbitrary", "arbitrary"), VMEM_LIMIT), name="moe_experts",
    )(tiles.reshape(-1), hp, x, wcol, gate, pos, pos, w1, w3, w2, ln_g.reshape(1, d), ln_b.reshape(1, d))


def kernel(x, c, positions, ada_w, ada_b, w_in, gdn_conv_w, gdn_a_log, gdn_dt_bias, gdn_norm_w, ret_gn_w,
           nsa_cmp_pe, nsa_ck_w1, nsa_ck_w2, nsa_cv_w1, nsa_cv_w2, fox_f_bias, branch_proj, w_gate, w_out,
           ln_g, ln_b, router_w, router_b, exp_w1, exp_w3, exp_w2):
    depth, d = w_in.shape[0], x.shape[-1]
    alpha = (2.0 * depth) ** 0.25
    mod = _ada_mod(c, ada_w, ada_b)
    for l in range(depth):
        shift, sc1p, gate = mod[l, 0, :, :, :d], mod[l, 0, :, :, d:2 * d], mod[l, 0, :, :, 2 * d:]
        wcat, wcat_t = _cat_in_weights(w_in[l])
        (gqkv, gz, r, kc, vc, ks, kw, fq, fk, sm,
         smt, nqt, vst, vwt, fvt) = _in_proj(x, sc1p, shift, wcat, wcat_t)
        o_gdn = _gdn(gqkv, sm, smt, gz, gdn_conv_w[l], gdn_a_log[l], gdn_dt_bias[l], gdn_norm_w[l])
        o_ret = _retention(r, positions, ret_gn_w[l])
        k_cmp, v_cmp_t = _nsa_compress(kc, vc, nsa_cmp_pe[l], nsa_ck_w1[l], nsa_ck_w2[l], nsa_cv_w1[l], nsa_cv_w2[l])
        o_nsa = _nsa_attn(nqt, k_cmp, v_cmp_t, ks, vst, kw, vwt, smt)
        o_fox = _fox_attn(fq, _fox_cum(sm, fk, fox_f_bias[l]), fvt)
        shift2, sc1p2, gate2 = mod[l, 1, :, :, :d], mod[l, 1, :, :, d:2 * d], mod[l, 1, :, :, 2 * d:]
        x, hp, logits = _merge(x, sc1p, shift, gate, o_gdn, o_ret, o_nsa, o_fox, w_gate[l].astype(BF16),
                               branch_proj[l].astype(BF16), w_out[l].astype(BF16), ln_g[l, 0], ln_b[l, 0],
                               sc1p2, shift2, router_w, alpha)
        pos, wcol, tiles = _router(logits, router_b, MOE_TM)
        x = _moe(x, gate2, hp, pos, wcol, tiles, exp_w1[l].astype(BF16), exp_w3[l].astype(BF16),
                 exp_w2[l].astype(BF16), ln_g[l, 1], ln_b[l, 1], alpha, MOE_TM)
    return x
```
